```python
import math
import jax, jax.numpy as jnp
from jax import lax
import numpy as np

D_MODEL = 1024
BATCH = 4
SEQ = 4096
DEPTH = 2

CHUNK = 64
NORM_EPS = 1e-6
ATT_HEADS = 8
ATT_HEAD_DIM = 64
ATT_WIDTH = ATT_HEADS * ATT_HEAD_DIM
LEFT_CHUNKS = 8
BAND_CHUNKS = LEFT_CHUNKS + 1
MAX_REL_DIST = 256
N_REL = 2 * MAX_REL_DIST + 1
MLSTM_HEADS = 4
MLSTM_HEAD_DIM = 128
MLSTM_WIDTH = MLSTM_HEADS * MLSTM_HEAD_DIM
CONV_WIDTH = 4
D_FF = 4 * D_MODEL
IN_SIZES = (ATT_WIDTH, ATT_WIDTH, ATT_WIDTH,
            MLSTM_WIDTH, MLSTM_WIDTH, MLSTM_WIDTH, MLSTM_WIDTH,
            MLSTM_HEADS, MLSTM_HEADS,
            D_MODEL, D_MODEL)
D_IN = 3 * ATT_WIDTH + 4 * MLSTM_WIDTH + 2 * MLSTM_HEADS + 2 * D_MODEL

kernel_name = "hybrid_gated_chunkattn_mlstm_block"


def rms_norm(x, g):
    xf = x.astype(jnp.float32)
    y = xf * lax.rsqrt(jnp.mean(xf * xf, axis=-1, keepdims=True) + NORM_EPS)
    return (y * g.astype(jnp.float32)).astype(x.dtype)


def causal_depthwise_conv(x, w, bias):
    c = x.shape[-1]
    out = lax.conv_general_dilated(
        x, w[:, None, :].astype(x.dtype), window_strides=(1,),
        padding=[(CONV_WIDTH - 1, 0)], dimension_numbers=('NWC', 'WIO', 'NWC'),
        feature_group_count=c)
    return out + bias.astype(x.dtype)


def chunk_band_attention(q, k, v, rel_bias):
    b, s, h, d = q.shape
    nc = s // CHUNK
    f32 = jnp.float32
    qc = q.astype(f32).reshape(b, nc, CHUNK, h, d)
    pad = ((0, 0), (LEFT_CHUNKS, 0), (0, 0), (0, 0), (0, 0))
    kp = jnp.pad(k.astype(f32).reshape(b, nc, CHUNK, h, d), pad)
    vp = jnp.pad(v.astype(f32).reshape(b, nc, CHUNK, h, d), pad)
    kband = jnp.concatenate([kp[:, j:j + nc] for j in range(BAND_CHUNKS)], axis=2)
    vband = jnp.concatenate([vp[:, j:j + nc] for j in range(BAND_CHUNKS)], axis=2)
    scores = jnp.einsum('bclhd,bckhd->bchlk', qc, kband) * (1.0 / math.sqrt(d))
    t_idx = jnp.arange(CHUNK)[:, None]
    kb_idx = jnp.arange(BAND_CHUNKS * CHUNK)[None, :]
    rel = t_idx - kb_idx + LEFT_CHUNKS * CHUNK
    rel_idx = jnp.clip(rel, -MAX_REL_DIST, MAX_REL_DIST) + MAX_REL_DIST
    bias = rel_bias.astype(f32)[:, rel_idx]
    key_chunk = jnp.arange(nc)[:, None] - LEFT_CHUNKS + kb_idx // CHUNK
    valid = key_chunk >= 0
    scores = scores + bias[None, None]
    scores = jnp.where(valid[None, :, None, None, :], scores, -1e30)
    p = jax.nn.softmax(scores, axis=-1)
    out = jnp.einsum('bchlk,bckhd->bclhd', p, vband)
    return out.reshape(b, s, h * d).astype(q.dtype)


def mlstm_chunkwise(q, k, v, i_pre, f_pre):
    b, s, h, d = q.shape
    nc = s // CHUNK
    L = CHUNK
    f32 = jnp.float32
    q = q.astype(f32).reshape(b, nc, L, h, d)
    k = k.astype(f32).reshape(b, nc, L, h, d) * (1.0 / math.sqrt(d))
    v = v.astype(f32).reshape(b, nc, L, h, d)
    ig = i_pre.astype(f32).reshape(b, nc, L, h)
    logf = jax.nn.log_sigmoid(f_pre.astype(f32)).reshape(b, nc, L, h)
    bcum = jnp.cumsum(logf, axis=2)
    b_last = bcum[:, :, -1]
    a = b_last[:, :, None] - bcum + ig
    m_loc = jnp.max(a, axis=2)
    w = jnp.exp(a - m_loc[:, :, None])
    c_loc = jnp.einsum('bclh,bclhv,bclhk->bchvk', w, v, k)
    n_loc = jnp.einsum('bclh,bclhk->bchk', w, k)

    def step(carry, xs):
        c_prev, n_prev, m_prev = carry
        c_l, n_l, m_l, bl = xs
        m_new = jnp.maximum(bl + m_prev, m_l)
        s_prev = jnp.exp(bl + m_prev - m_new)
        s_loc = jnp.exp(m_l - m_new)
        c_new = s_prev[..., None, None] * c_prev + s_loc[..., None, None] * c_l
        n_new = s_prev[..., None] * n_prev + s_loc[..., None] * n_l
        return (c_new, n_new, m_new), (c_prev, n_prev, m_prev)

    init = (jnp.zeros((b, h, d, d), f32), jnp.zeros((b, h, d), f32), jnp.zeros((b, h), f32))
    xs = (jnp.moveaxis(c_loc, 1, 0), jnp.moveaxis(n_loc, 1, 0),
          jnp.moveaxis(m_loc, 1, 0), jnp.moveaxis(b_last, 1, 0))
    _, (c_st, n_st, m_st) = lax.scan(step, init, xs)
    c_st = jnp.moveaxis(c_st, 0, 1)
    n_st = jnp.moveaxis(n_st, 0, 1)
    m_st = jnp.moveaxis(m_st, 0, 1)

    g = bcum + m_st[:, :, None]
    dmat = bcum[:, :, :, None, :] - bcum[:, :, None, :, :] + ig[:, :, None, :, :]
    causal = jnp.tril(jnp.ones((L, L), dtype=bool))
    dmat = jnp.where(causal[None, None, :, :, None], dmat, -jnp.inf)
    m_t = jnp.maximum(g, jnp.max(dmat, axis=3))
    dw = jnp.exp(dmat - m_t[:, :, :, None, :])
    qk = jnp.einsum('bcthd,bcshd->bctsh', q, k) * dw
    inter = jnp.exp(g - m_t)
    num = (inter[..., None] * jnp.einsum('bchvk,bcthk->bcthv', c_st, q)
           + jnp.einsum('bctsh,bcshv->bcthv', qk, v))
    den = inter * jnp.einsum('bchk,bcthk->bcth', n_st, q) + jnp.sum(qk, axis=3)
    hout = num / jnp.maximum(jnp.abs(den), jnp.exp(-m_t))[..., None]
    return hout.reshape(b, s, h * d)


def hybrid_layer(x, mix_norm_g, w_in, conv_w, conv_b, b_igate, b_fgate, rel_bias,
                 mh_norm_g, w_att_proj, w_mlstm_proj, w_out, ffn_norm_g, w_up, w_down):
    bsz, s, _ = x.shape
    xn = rms_norm(x, mix_norm_g)
    proj = xn @ w_in
    points = []
    acc = 0
    for size in IN_SIZES[:-1]:
        acc += size
        points.append(acc)
    aq, ak, av, mq, mk, mv, mo, ipre, fpre, ga, gm = jnp.split(proj, points, axis=-1)

    att = chunk_band_attention(
        aq.reshape(bsz, s, ATT_HEADS, ATT_HEAD_DIM),
        ak.reshape(bsz, s, ATT_HEADS, ATT_HEAD_DIM),
        av.reshape(bsz, s, ATT_HEADS, ATT_HEAD_DIM), rel_bias)

    mqk = jax.nn.silu(causal_depthwise_conv(jnp.concatenate([mq, mk], axis=-1), conv_w, conv_b))
    mq_c, mk_c = jnp.split(mqk, [MLSTM_WIDTH], axis=-1)
    hm = mlstm_chunkwise(
        mq_c.reshape(bsz, s, MLSTM_HEADS, MLSTM_HEAD_DIM),
        mk_c.reshape(bsz, s, MLSTM_HEADS, MLSTM_HEAD_DIM),
        mv.reshape(bsz, s, MLSTM_HEADS, MLSTM_HEAD_DIM),
        ipre + b_igate, fpre + b_fgate)
    hm = rms_norm(hm.reshape(bsz, s, MLSTM_HEADS, MLSTM_HEAD_DIM),
                  mh_norm_g.reshape(MLSTM_HEADS, MLSTM_HEAD_DIM)).reshape(bsz, s, MLSTM_WIDTH)
    mlstm_out = (jax.nn.sigmoid(mo.astype(jnp.float32)) * hm).astype(x.dtype)

    y = jax.nn.sigmoid(ga) * (att @ w_att_proj) + jax.nn.sigmoid(gm) * (mlstm_out @ w_mlstm_proj)
    x = x + y @ w_out

    hn = rms_norm(x, ffn_norm_g)
    x = x + jnp.square(jax.nn.relu(hn @ w_up)) @ w_down
    return x


def setup_inputs(seed: int = 0) -> dict:
    key = jax.random.key(seed)
    ks = jax.random.split(key, 16)
    f32 = jnp.float32
    nrm = lambda k, shape, scale: jax.random.normal(k, shape, f32) * scale
    return {
        "x": nrm(ks[0], (BATCH, SEQ, D_MODEL), 1.0),
        "mix_norm_g": 1.0 + nrm(ks[1], (DEPTH, D_MODEL), 0.02),
        "w_in": nrm(ks[2], (DEPTH, D_MODEL, D_IN), D_MODEL ** -0.5),
        "conv_w": nrm(ks[3], (DEPTH, CONV_WIDTH, 2 * MLSTM_WIDTH), CONV_WIDTH ** -0.5),
        "conv_b": nrm(ks[4], (DEPTH, 2 * MLSTM_WIDTH), 0.01),
        "b_igate": nrm(ks[5], (DEPTH, MLSTM_HEADS), 0.1),
        "b_fgate": 3.0 + 3.0 * jax.random.uniform(ks[6], (DEPTH, MLSTM_HEADS), f32),
        "rel_bias": nrm(ks[7], (DEPTH, ATT_HEADS, N_REL), 0.2),
        "mh_norm_g": 1.0 + nrm(ks[8], (DEPTH, MLSTM_WIDTH), 0.02),
        "w_att_proj": nrm(ks[9], (DEPTH, ATT_WIDTH, D_MODEL), ATT_WIDTH ** -0.5),
        "w_mlstm_proj": nrm(ks[10], (DEPTH, MLSTM_WIDTH, D_MODEL), MLSTM_WIDTH ** -0.5),
        "w_out": nrm(ks[11], (DEPTH, D_MODEL, D_MODEL), D_MODEL ** -0.5),
        "ffn_norm_g": 1.0 + nrm(ks[12], (DEPTH, D_MODEL), 0.02),
        "w_up": nrm(ks[13], (DEPTH, D_MODEL, D_FF), D_MODEL ** -0.5),
        "w_down": nrm(ks[14], (DEPTH, D_FF, D_MODEL), D_FF ** -0.5),
        "final_norm_g": 1.0 + nrm(ks[15], (D_MODEL,), 0.02),
    }


def reference(x, mix_norm_g, w_in, conv_w, conv_b, b_igate, b_fgate, rel_bias, mh_norm_g,
              w_att_proj, w_mlstm_proj, w_out, ffn_norm_g, w_up, w_down, final_norm_g):
    for l in range(DEPTH):
        x = hybrid_layer(x, mix_norm_g[l], w_in[l], conv_w[l], conv_b[l], b_igate[l], b_fgate[l],
                         rel_bias[l], mh_norm_g[l], w_att_proj[l], w_mlstm_proj[l], w_out[l],
                         ffn_norm_g[l], w_up[l], w_down[l])
    return rms_norm(x, final_norm_g)
```

```python
import functools
import math

import jax
import jax.numpy as jnp
from jax import lax
from jax.experimental import pallas as pl
from jax.experimental.pallas import tpu as pltpu

D_MODEL = 1024
BATCH = 4
SEQ = 4096
DEPTH = 2
CHUNK = 64
NORM_EPS = 1e-6
ATT_HEADS = 8
ATT_HEAD_DIM = 64
ATT_WIDTH = ATT_HEADS * ATT_HEAD_DIM
LEFT_CHUNKS = 8
BAND_CHUNKS = LEFT_CHUNKS + 1
BAND = BAND_CHUNKS * CHUNK
MAX_REL_DIST = 256
MLSTM_HEADS = 4
MLSTM_HEAD_DIM = 128
MLSTM_WIDTH = MLSTM_HEADS * MLSTM_HEAD_DIM
CONV_WIDTH = 4
D_FF = 4 * D_MODEL

TOKENS = BATCH * SEQ
NUM_CHUNKS = SEQ // CHUNK
PAD_ROWS = LEFT_CHUNKS * CHUNK
SEQ_PAD = SEQ + PAD_ROWS
MAIN_COLS = 3 * ATT_WIDTH + 4 * MLSTM_WIDTH + 2 * D_MODEL
GATE_LANES = 128
SUBLANES = 8

TOKEN_TILE = 512
FF_TILE = 1024

F32 = jnp.float32
BF16 = jnp.bfloat16
MIB = 1024 * 1024


def _resident(shape):
    nd = len(shape)
    return pl.BlockSpec(shape, lambda *_: (0,) * nd, pipeline_mode=pl.Buffered(1))


def _rms_norm(x, g):
    ms = jnp.mean(x * x, axis=-1, keepdims=True)
    return x * lax.rsqrt(ms + NORM_EPS) * g


def _dot(a, b):
    return jnp.dot(a, b, preferred_element_type=F32)


def _dot_nt(a, b):
    return lax.dot_general(a, b, (((1,), (1,)), ((), ())), preferred_element_type=F32)


def _dot_tn(a, b):
    return lax.dot_general(a, b, (((0,), (0,)), ((), ())), preferred_element_type=F32)


def _in_proj_kernel(x_ref, g_ref, w_ref, wg_ref, kz_ref, vz_ref,
                    q_ref, k_ref, v_ref, mqk_ref, mv_ref, mo_ref, gate_ref, gcol_ref):
    del kz_ref, vz_ref
    xn = _rms_norm(x_ref[...], g_ref[...]).astype(BF16)

    def seg(lo, width):
        return _dot(xn, w_ref[:, lo:lo + width])

    a = ATT_WIDTH
    m = MLSTM_WIDTH
    q_ref[...] = seg(0, a).astype(BF16)
    k_ref[0] = seg(a, a).astype(BF16)
    v_ref[0] = seg(2 * a, a).astype(BF16)
    mqk_ref[:, :m] = seg(3 * a, m).astype(BF16)
    mqk_ref[:, m:] = seg(3 * a + m, m).astype(BF16)
    mv_ref[...] = seg(3 * a + 2 * m, m).astype(BF16)
    mo_ref[...] = seg(3 * a + 3 * m, m).astype(BF16)
    base = 3 * a + 4 * m
    for j in range(2 * D_MODEL // 512):
        gate_ref[:, j * 512:(j + 1) * 512] = seg(base + j * 512, 512).astype(BF16)
    gcol_ref[...] = _dot(xn, wg_ref[...])


def _in_proj(x2d, g, w_main, w_gate, kz, vz):
    tm = TOKEN_TILE
    tiles_per_batch = SEQ // tm
    pad_tiles = PAD_ROWS // tm
    row = lambda width: pl.BlockSpec((tm, width), lambda i: (i, 0))
    padded = pl.BlockSpec((1, tm, ATT_WIDTH),
                          lambda i: (i // tiles_per_batch, i % tiles_per_batch + pad_tiles, 0))
    out_shape = (
        jax.ShapeDtypeStruct((TOKENS, ATT_WIDTH), BF16),
        jax.ShapeDtypeStruct((BATCH, SEQ_PAD, ATT_WIDTH), BF16),
        jax.ShapeDtypeStruct((BATCH, SEQ_PAD, ATT_WIDTH), BF16),
        jax.ShapeDtypeStruct((TOKENS, 2 * MLSTM_WIDTH), BF16),
        jax.ShapeDtypeStruct((TOKENS, MLSTM_WIDTH), BF16),
        jax.ShapeDtypeStruct((TOKENS, MLSTM_WIDTH), BF16),
        jax.ShapeDtypeStruct((TOKENS, 2 * D_MODEL), BF16),
        jax.ShapeDtypeStruct((TOKENS, GATE_LANES), F32),
    )
    return pl.pallas_call(
        _in_proj_kernel,
        grid=(TOKENS // tm,),
        in_specs=[
            row(D_MODEL),
            _resident((1, D_MODEL)),
            _resident((D_MODEL, MAIN_COLS)),
            _resident((D_MODEL, GATE_LANES)),
            pl.BlockSpec(memory_space=pl.ANY),
            pl.BlockSpec(memory_space=pl.ANY),
        ],
        out_specs=(row(ATT_WIDTH), padded, padded, row(2 * MLSTM_WIDTH), row(MLSTM_WIDTH),
                   row(MLSTM_WIDTH), row(2 * D_MODEL), row(GATE_LANES)),
        out_shape=out_shape,
        input_output_aliases={4: 1, 5: 2},
        compiler_params=pltpu.CompilerParams(
            dimension_semantics=("arbitrary",), vmem_limit_bytes=48 * MIB),
        name="in_proj",
    )(x2d, g, w_main, w_gate, kz, vz)


def _band_attn_kernel(q_ref, k_ref, v_ref, bias_ref, o_ref):
    c = pl.program_id(1)
    start = pl.multiple_of(c * CHUNK, CHUNK)
    q = q_ref[0]
    kw = k_ref[0, pl.ds(start, BAND), :]
    vw = v_ref[0, pl.ds(start, BAND), :]
    key_row = start + lax.broadcasted_iota(jnp.int32, (1, BAND), 1)
    valid = key_row >= PAD_ROWS
    outs = []
    for h in range(ATT_HEADS):
        sl = slice(h * ATT_HEAD_DIM, (h + 1) * ATT_HEAD_DIM)
        s = _dot_nt(q[:, sl], kw[:, sl]) + bias_ref[h]
        s = jnp.where(valid, s, -1e30)
        p = jnp.exp(s - jnp.max(s, axis=-1, keepdims=True))
        denom = jnp.sum(p, axis=-1, keepdims=True)
        outs.append(_dot(p.astype(BF16), vw[:, sl]) / denom)
    o_ref[0] = jnp.concatenate(outs, axis=-1).astype(BF16)


def _band_attn(q, k_pad, v_pad, bias):
    blk = pl.BlockSpec((1, CHUNK, ATT_WIDTH), lambda b, c: (b, c, 0))
    whole = pl.BlockSpec((1, SEQ_PAD, ATT_WIDTH), lambda b, c: (b, 0, 0))
    return pl.pallas_call(
        _band_attn_kernel,
        grid=(BATCH, NUM_CHUNKS),
        in_specs=[blk, whole, whole, _resident((ATT_HEADS, CHUNK, BAND))],
        out_specs=blk,
        out_shape=jax.ShapeDtypeStruct((BATCH, SEQ, ATT_WIDTH), BF16),
        compiler_params=pltpu.CompilerParams(
            dimension_semantics=("arbitrary", "arbitrary"), vmem_limit_bytes=40 * MIB),
        name="band_attn",
    )(q, k_pad, v_pad, bias)


def _cumsum_rows(x):
    row = lax.broadcasted_iota(jnp.int32, x.shape, 0)
    d = 1
    while d < x.shape[0]:
        x = x + jnp.where(row >= d, pltpu.roll(x, d, axis=0), 0.0)
        d *= 2
    return x


def _mlstm_kernel(mqk_ref, mv_ref, mo_ref, gcol_ref, cw_ref, cb_ref, gb_ref, ng_ref, o_ref,
                  ct_scr, n_scr, m_scr, conv_scr):
    c = pl.program_id(0)
    hd = MLSTM_HEAD_DIM
    tail = SUBLANES

    @pl.when(c == 0)
    def _():
        ct_scr[...] = jnp.zeros_like(ct_scr)
        n_scr[...] = jnp.zeros_like(n_scr)
        m_scr[...] = jnp.zeros_like(m_scr)
        conv_scr[:, :tail, :] = jnp.zeros((BATCH, tail, 2 * MLSTM_WIDTH), F32)

    t_idx = lax.broadcasted_iota(jnp.int32, (CHUNK, CHUNK), 0)
    s_idx = lax.broadcasted_iota(jnp.int32, (CHUNK, CHUNK), 1)
    causal = s_idx <= t_idx
    k_scale = 1.0 / math.sqrt(hd)

    for b in range(BATCH):
        xb = mqk_ref[b].astype(F32)
        conv_scr[b, tail:, :] = xb
        acc = cb_ref[...]
        for j in range(CONV_WIDTH):
            lo = tail - (CONV_WIDTH - 1) + j
            acc = acc + conv_scr[b, lo:lo + CHUNK, :] * cw_ref[j:j + 1, :]
        conv_scr[b, :tail, :] = xb[CHUNK - tail:, :]
        act = acc * jax.nn.sigmoid(acc)
        q_all = act[:, :MLSTM_WIDTH]
        k_all = act[:, MLSTM_WIDTH:] * k_scale

        gates = gcol_ref[b] + gb_ref[...]
        logf = jnp.minimum(gates, 0.0) - jnp.log1p(jnp.exp(-jnp.abs(gates)))
        bcum = _cumsum_rows(logf)
        r_cols = gates - pltpu.roll(bcum, GATE_LANES - MLSTM_HEADS, axis=1)
        r_rows = jnp.transpose(jnp.concatenate([r_cols, jnp.zeros_like(r_cols)], axis=0))

        v_all = mv_ref[b]
        mo_all = mo_ref[b].astype(F32)
        for h in range(MLSTM_HEADS):
            u = b * MLSTM_HEADS + h
            sl = slice(h * hd, (h + 1) * hd)
            qh = q_all[:, sl]
            kh = k_all[:, sl]
            vh = v_all[:, sl]
            qh_b = qh.astype(BF16)
            ig_c = gates[:, h:h + 1]
            bc_c = bcum[:, MLSTM_HEADS + h:MLSTM_HEADS + h + 1]
            b_last = bc_c[CHUNK - 1:CHUNK, :]

            a_c = b_last - bc_c + ig_c
            m_loc = jnp.max(a_c, axis=0, keepdims=True)
            kw = kh * jnp.exp(a_c - m_loc)
            ct_loc = _dot_tn(kw.astype(BF16), vh)
            n_loc = jnp.sum(kw, axis=0, keepdims=True)

            m_prev = m_scr[u, 0:1, 0:1]
            ct_prev = ct_scr[u]
            n_prev = n_scr[u, 0:1, :]
            m_new = jnp.maximum(b_last + m_prev, m_loc)
            s_prev = jnp.exp(b_last + m_prev - m_new)
            s_loc = jnp.exp(m_loc - m_new)
            ct_scr[u] = s_prev * ct_prev + s_loc * ct_loc
            n_scr[u, 0:1, :] = s_prev * n_prev + s_loc * n_loc
            m_scr[u] = jnp.broadcast_to(m_new, (SUBLANES, GATE_LANES))

            g_c = bc_c + m_prev
            dmat = jnp.where(causal, bc_c + r_rows[h:h + 1, :CHUNK], -jnp.inf)
            m_t = jnp.maximum(g_c, jnp.max(dmat, axis=1, keepdims=True))
            dw = jnp.exp(dmat - m_t)
            qk = _dot_nt(qh_b, kh.astype(BF16)) * dw
            inter = jnp.exp(g_c - m_t)
            num = inter * _dot(qh_b, ct_prev.astype(BF16)) + _dot(qk.astype(BF16), vh)
            den = (inter * jnp.sum(qh * n_prev, axis=1, keepdims=True)
                   + jnp.sum(qk, axis=1, keepdims=True))
            hout = num / jnp.maximum(jnp.abs(den), jnp.exp(-m_t))
            y = _rms_norm(hout, ng_ref[:, sl])
            o_ref[b, :, sl] = (jax.nn.sigmoid(mo_all[:, sl]) * y).astype(BF16)


def _mlstm(mqk, mv, mo, gcol, conv_w, conv_b, gate_bias, norm_g):
    blk = lambda width: pl.BlockSpec((BATCH, CHUNK, width), lambda c: (0, c, 0))
    units = BATCH * MLSTM_HEADS
    return pl.pallas_call(
        _mlstm_kernel,
        grid=(NUM_CHUNKS,),
        in_specs=[
            blk(2 * MLSTM_WIDTH), blk(MLSTM_WIDTH), blk(MLSTM_WIDTH), blk(GATE_LANES),
            _resident((CONV_WIDTH, 2 * MLSTM_WIDTH)),
            _resident((1, 2 * MLSTM_WIDTH)),
            _resident((1, GATE_LANES)),
            _resident((1, MLSTM_WIDTH)),
        ],
        out_specs=blk(MLSTM_WIDTH),
        out_shape=jax.ShapeDtypeStruct((BATCH, SEQ, MLSTM_WIDTH), BF16),
        scratch_shapes=[
            pltpu.VMEM((units, MLSTM_HEAD_DIM, MLSTM_HEAD_DIM), F32),
            pltpu.VMEM((units, SUBLANES, MLSTM_HEAD_DIM), F32),
            pltpu.VMEM((units, SUBLANES, GATE_LANES), F32),
            pltpu.VMEM((BATCH, SUBLANES + CHUNK, 2 * MLSTM_WIDTH), F32),
        ],
        compiler_params=pltpu.CompilerParams(
            dimension_semantics=("arbitrary",), vmem_limit_bytes=32 * MIB),
        name="mlstm",
    )(mqk, mv, mo, gcol, conv_w, conv_b, gate_bias, norm_g)


def _merge_kernel(x_ref, att_ref, ml_ref, gate_ref, wa_ref, wm_ref, wo_ref, o_ref):
    ga = gate_ref[:, :D_MODEL].astype(F32)
    gm = gate_ref[:, D_MODEL:].astype(F32)
    y = (jax.nn.sigmoid(ga) * _dot(att_ref[...], wa_ref[...])
         + jax.nn.sigmoid(gm) * _dot(ml_ref[...], wm_ref[...]))
    o_ref[...] = x_ref[...] + _dot(y.astype(BF16), wo_ref[...])


def _merge(x2d, att, ml, gates, wa, wm, wo):
    tm = TOKEN_TILE
    row = lambda width: pl.BlockSpec((tm, width), lambda i: (i, 0))
    return pl.pallas_call(
        _merge_kernel,
        grid=(TOKENS // tm,),
        in_specs=[row(D_MODEL), row(ATT_WIDTH), row(MLSTM_WIDTH), row(2 * D_MODEL),
                  _resident((ATT_WIDTH, D_MODEL)), _resident((MLSTM_WIDTH, D_MODEL)),
                  _resident((D_MODEL, D_MODEL))],
        out_specs=row(D_MODEL),
        out_shape=jax.ShapeDtypeStruct((TOKENS, D_MODEL), F32),
        compiler_params=pltpu.CompilerParams(
            dimension_semantics=("arbitrary",), vmem_limit_bytes=40 * MIB),
        name="merge",
    )(x2d, att, ml, gates, wa, wm, wo)


def _ffn_kernel(x_ref, g_ref, wu_ref, wd_ref, fg_ref, o_ref, *, final_norm):
    x = x_ref[...]
    hn = _rms_norm(x, g_ref[...]).astype(BF16)
    acc = x
    for j in range(D_FF // FF_TILE):
        cols = slice(j * FF_TILE, (j + 1) * FF_TILE)
        h = jnp.maximum(_dot(hn, wu_ref[:, cols]), 0.0)
        acc = acc + _dot((h * h).astype(BF16), wd_ref[cols, :])
    if final_norm:
        acc = _rms_norm(acc, fg_ref[...])
    o_ref[...] = acc


def _ffn(x2d, g, wu, wd, final_g, final_norm):
    tm = TOKEN_TILE
    row = pl.BlockSpec((tm, D_MODEL), lambda i: (i, 0))
    return pl.pallas_call(
        functools.partial(_ffn_kernel, final_norm=final_norm),
        grid=(TOKENS // tm,),
        in_specs=[row, _resident((1, D_MODEL)), _resident((D_MODEL, D_FF)),
                  _resident((D_FF, D_MODEL)), _resident((1, D_MODEL))],
        out_specs=row,
        out_shape=jax.ShapeDtypeStruct((TOKENS, D_MODEL), F32),
        compiler_params=pltpu.CompilerParams(
            dimension_semantics=("arbitrary",), vmem_limit_bytes=48 * MIB),
        name="ffn",
    )(x2d, g, wu, wd, final_g)


def _rel_bias_table(rel_bias):
    t_idx = jnp.arange(CHUNK)[:, None]
    kb_idx = jnp.arange(BAND)[None, :]
    rel = jnp.clip(t_idx - kb_idx + PAD_ROWS, -MAX_REL_DIST, MAX_REL_DIST) + MAX_REL_DIST
    return rel_bias.astype(F32)[:, rel]


def kernel(x, mix_norm_g, w_in, conv_w, conv_b, b_igate, b_fgate, rel_bias, mh_norm_g,
           w_att_proj, w_mlstm_proj, w_out, ffn_norm_g, w_up, w_down, final_norm_g):
    h = x.reshape(TOKENS, D_MODEL)
    gate_lo = 3 * ATT_WIDTH + 4 * MLSTM_WIDTH
    gate_hi = gate_lo + 2 * MLSTM_HEADS
    att_scale = 1.0 / math.sqrt(ATT_HEAD_DIM)
    final_g = final_norm_g.reshape(1, D_MODEL)
    for l in range(DEPTH):
        w = w_in[l]
        w_main = jnp.concatenate(
            [w[:, :ATT_WIDTH] * att_scale, w[:, ATT_WIDTH:gate_lo], w[:, gate_hi:]],
            axis=1).astype(BF16)
        w_gate = jnp.pad(w[:, gate_lo:gate_hi],
                         ((0, 0), (0, GATE_LANES - 2 * MLSTM_HEADS))).astype(BF16)
        gate_bias = jnp.pad(jnp.concatenate([b_igate[l], b_fgate[l]]),
                            (0, GATE_LANES - 2 * MLSTM_HEADS)).reshape(1, GATE_LANES)

        q, k_pad, v_pad, mqk, mv, mo, gates, gcol = _in_proj(
            h, mix_norm_g[l].reshape(1, D_MODEL), w_main, w_gate,
            jnp.zeros((BATCH, SEQ_PAD, ATT_WIDTH), BF16),
            jnp.zeros((BATCH, SEQ_PAD, ATT_WIDTH), BF16))
        att = _band_attn(q.reshape(BATCH, SEQ, ATT_WIDTH), k_pad, v_pad,
                         _rel_bias_table(rel_bias[l]))
        ml = _mlstm(mqk.reshape(BATCH, SEQ, 2 * MLSTM_WIDTH),
                    mv.reshape(BATCH, SEQ, MLSTM_WIDTH),
                    mo.reshape(BATCH, SEQ, MLSTM_WIDTH),
                    gcol.reshape(BATCH, SEQ, GATE_LANES),
                    conv_w[l], conv_b[l].reshape(1, 2 * MLSTM_WIDTH), gate_bias,
                    mh_norm_g[l].reshape(1, MLSTM_WIDTH))
        h = _merge(h, att.reshape(TOKENS, ATT_WIDTH), ml.reshape(TOKENS, MLSTM_WIDTH), gates,
                   w_att_proj[l].astype(BF16), w_mlstm_proj[l].astype(BF16),
                   w_out[l].astype(BF16))
        h = _ffn(h, ffn_norm_g[l].reshape(1, D_MODEL), w_up[l].astype(BF16),
                 w_down[l].astype(BF16), final_g, final_norm=(l == DEPTH - 1))
    return h.reshape(BATCH, SEQ, D_MODEL)
```

```python
import functools
import math

import jax
import jax.numpy as jnp
from jax import lax
from jax.experimental import pallas as pl
from jax.experimental.pallas import tpu as pltpu

D_MODEL = 1024
BATCH = 4
SEQ = 4096
DEPTH = 2
CHUNK = 64
NORM_EPS = 1e-6
ATT_HEADS = 8
ATT_HEAD_DIM = 64
ATT_WIDTH = ATT_HEADS * ATT_HEAD_DIM
LEFT_CHUNKS = 8
BAND_CHUNKS = LEFT_CHUNKS + 1
BAND = BAND_CHUNKS * CHUNK
MAX_REL_DIST = 256
MLSTM_HEADS = 4
MLSTM_HEAD_DIM = 128
MLSTM_WIDTH = MLSTM_HEADS * MLSTM_HEAD_DIM
CONV_WIDTH = 4
D_FF = 4 * D_MODEL

LANES = 128
SUBLANES = 8

TOKENS = BATCH * SEQ
NUM_CHUNKS = SEQ // CHUNK
PAD_ROWS = LEFT_CHUNKS * CHUNK
SEQ_PAD = SEQ + PAD_ROWS
ROW_COLS = ATT_WIDTH + 4 * MLSTM_WIDTH + 2 * D_MODEL
COL_ROWS = 2 * ATT_WIDTH
GATE_LANES = LANES

Q_TILE = 2 * CHUNK
WINDOW = BAND + CHUNK
WINDOW_TILES = WINDOW // LANES
ATT_STEPS = SEQ // Q_TILE
REL_ROW = 1024
LOG2E = 1.4426950408889634
MASKED = -1e30

TOKEN_TILE = 512
FF_TILE = 1024

F32 = jnp.float32
BF16 = jnp.bfloat16
MIB = 1024 * 1024


def _resident(shape):
    nd = len(shape)
    return pl.BlockSpec(shape, lambda *_: (0,) * nd, pipeline_mode=pl.Buffered(1))


def _rms_norm(x, g):
    ms = jnp.mean(x * x, axis=-1, keepdims=True)
    return x * lax.rsqrt(ms + NORM_EPS) * g


def _dot(a, b):
    return jnp.dot(a, b, preferred_element_type=F32)


def _dot_nt(a, b):
    return lax.dot_general(a, b, (((1,), (1,)), ((), ())), preferred_element_type=F32)


def _dot_tn(a, b):
    return lax.dot_general(a, b, (((0,), (0,)), ((), ())), preferred_element_type=F32)


def _in_proj_kernel(x_ref, g_ref, w_ref, wt_ref, wg_ref, kz_ref, vz_ref,
                    qt_ref, k_ref, vt_ref, mqk_ref, mv_ref, mo_ref, gate_ref, gcol_ref):
    del kz_ref, vz_ref
    xn = _rms_norm(x_ref[...], g_ref[...]).astype(BF16)

    def seg(lo, width):
        return _dot(xn, w_ref[:, lo:lo + width])

    a = ATT_WIDTH
    m = MLSTM_WIDTH
    qt_ref[...] = (_dot_nt(wt_ref[:a, :], xn) * (LOG2E / math.sqrt(ATT_HEAD_DIM))).astype(BF16)
    vt = _dot_nt(wt_ref[a:, :], xn).astype(BF16)
    for j in range(TOKEN_TILE // LANES):
        vt_ref[0, j] = vt[:, j * LANES:(j + 1) * LANES]
    k_ref[0] = seg(0, a).astype(BF16)
    mqk_ref[:, :m] = seg(a, m).astype(BF16)
    mqk_ref[:, m:] = seg(a + m, m).astype(BF16)
    mv_ref[...] = seg(a + 2 * m, m).astype(BF16)
    mo_ref[...] = seg(a + 3 * m, m).astype(BF16)
    base = a + 4 * m
    for j in range(2 * D_MODEL // 512):
        gate_ref[:, j * 512:(j + 1) * 512] = seg(base + j * 512, 512).astype(BF16)
    gcol_ref[...] = _dot(xn, wg_ref[...])


def _in_proj(x2d, g, w_rows, w_cols, w_gate, kz, vz):
    tm = TOKEN_TILE
    tiles_per_batch = SEQ // tm
    pad_tiles = PAD_ROWS // tm
    lane_tiles = tm // LANES
    row = lambda width: pl.BlockSpec((tm, width), lambda i: (i, 0))
    k_spec = pl.BlockSpec((1, tm, ATT_WIDTH),
                          lambda i: (i // tiles_per_batch, i % tiles_per_batch + pad_tiles, 0))
    vt_spec = pl.BlockSpec((1, lane_tiles, ATT_WIDTH, LANES),
                           lambda i: (i // tiles_per_batch, i % tiles_per_batch + pad_tiles, 0, 0))
    out_shape = (
        jax.ShapeDtypeStruct((ATT_WIDTH, TOKENS), BF16),
        jax.ShapeDtypeStruct((BATCH, SEQ_PAD, ATT_WIDTH), BF16),
        jax.ShapeDtypeStruct((BATCH, SEQ_PAD // LANES, ATT_WIDTH, LANES), BF16),
        jax.ShapeDtypeStruct((TOKENS, 2 * MLSTM_WIDTH), BF16),
        jax.ShapeDtypeStruct((TOKENS, MLSTM_WIDTH), BF16),
        jax.ShapeDtypeStruct((TOKENS, MLSTM_WIDTH), BF16),
        jax.ShapeDtypeStruct((TOKENS, 2 * D_MODEL), BF16),
        jax.ShapeDtypeStruct((TOKENS, GATE_LANES), F32),
    )
    return pl.pallas_call(
        _in_proj_kernel,
        grid=(TOKENS // tm,),
        in_specs=[
            row(D_MODEL),
            _resident((1, D_MODEL)),
            _resident((D_MODEL, ROW_COLS)),
            _resident((COL_ROWS, D_MODEL)),
            _resident((D_MODEL, GATE_LANES)),
            pl.BlockSpec(memory_space=pl.ANY),
            pl.BlockSpec(memory_space=pl.ANY),
        ],
        out_specs=(pl.BlockSpec((ATT_WIDTH, tm), lambda i: (0, i)), k_spec, vt_spec,
                   row(2 * MLSTM_WIDTH), row(MLSTM_WIDTH), row(MLSTM_WIDTH), row(2 * D_MODEL),
                   row(GATE_LANES)),
        out_shape=out_shape,
        input_output_aliases={5: 1, 6: 2},
        compiler_params=pltpu.CompilerParams(
            dimension_semantics=("arbitrary",), vmem_limit_bytes=48 * MIB),
        name="in_proj",
    )(x2d, g, w_rows, w_cols, w_gate, kz, vz)


def _build_bias_table(rel_ref, bias_scr):
    kb = lax.broadcasted_iota(jnp.int32, (WINDOW, LANES), 0)
    lane = lax.broadcasted_iota(jnp.int32, (WINDOW, LANES), 1)
    band_lo = jnp.where(lane >= CHUNK, CHUNK, 0)
    in_band = (kb >= band_lo) & (kb < band_lo + BAND)
    for h in range(ATT_HEADS):
        rows = jnp.broadcast_to(rel_ref[h:h + 1, :], (LANES, REL_ROW))
        shifted = pltpu.roll(rows, 0, axis=1, stride=1, stride_axis=0)
        table = jnp.concatenate(
            [jnp.transpose(shifted[:, j * LANES:(j + 1) * LANES]) for j in range(WINDOW_TILES)],
            axis=0)
        bias_scr[h] = jnp.where(in_band, table, MASKED)


def _band_attn_kernel(qt_ref, k_ref, vt_ref, rel_ref, o_ref, bias_scr):
    step = pl.program_id(1)

    @pl.when((pl.program_id(0) == 0) & (step == 0))
    def _():
        _build_bias_table(rel_ref, bias_scr)

    start = pl.multiple_of(step * Q_TILE, Q_TILE)
    first_chunk = 2 * step
    zeros_half = jnp.zeros((ATT_HEAD_DIM, LANES), BF16)
    outs = []
    for pair in range(ATT_HEADS // 2):
        lanes = slice(pair * LANES, (pair + 1) * LANES)
        kwin = k_ref[0, pl.ds(start, WINDOW), lanes]
        for hh in range(2):
            h = 2 * pair + hh
            rows = slice(h * ATT_HEAD_DIM, (h + 1) * ATT_HEAD_DIM)
            qh = qt_ref[rows, :]
            qm = jnp.concatenate([qh, zeros_half] if hh == 0 else [zeros_half, qh], axis=0)
            st = _dot(kwin, qm)
            blocks = []
            for j in range(WINDOW // CHUNK):
                r = slice(j * CHUNK, (j + 1) * CHUNK)
                blk = st[r] + bias_scr[h, r, :]
                if j < LEFT_CHUNKS:
                    blk = jnp.where(first_chunk + j >= LEFT_CHUNKS, blk, MASKED)
                blocks.append(blk)
            m = jnp.max(functools.reduce(jnp.maximum, blocks), axis=0, keepdims=True)
            probs = [jnp.exp2(blk - m) for blk in blocks]
            denom = jnp.sum(functools.reduce(jnp.add, probs), axis=0, keepdims=True)
            pt = jnp.concatenate(probs, axis=0).astype(BF16)
            acc = None
            for j in range(WINDOW_TILES):
                part = _dot(vt_ref[0, step + j, rows, :], pt[j * LANES:(j + 1) * LANES])
                acc = part if acc is None else acc + part
            outs.append(acc / denom)
    o_ref[0] = jnp.transpose(jnp.concatenate(outs, axis=0)).astype(BF16)


def _band_attn(qt, k_pad, vt_pad, rel_rows):
    return pl.pallas_call(
        _band_attn_kernel,
        grid=(BATCH, ATT_STEPS),
        in_specs=[
            pl.BlockSpec((ATT_WIDTH, Q_TILE), lambda b, s: (0, b * ATT_STEPS + s)),
            pl.BlockSpec((1, SEQ_PAD, ATT_WIDTH), lambda b, s: (b, 0, 0)),
            pl.BlockSpec((1, SEQ_PAD // LANES, ATT_WIDTH, LANES), lambda b, s: (b, 0, 0, 0)),
            _resident((ATT_HEADS, REL_ROW)),
        ],
        out_specs=pl.BlockSpec((1, Q_TILE, ATT_WIDTH), lambda b, s: (b, s, 0)),
        out_shape=jax.ShapeDtypeStruct((BATCH, SEQ, ATT_WIDTH), BF16),
        scratch_shapes=[pltpu.VMEM((ATT_HEADS, WINDOW, LANES), F32)],
        compiler_params=pltpu.CompilerParams(
            dimension_semantics=("arbitrary", "arbitrary"), vmem_limit_bytes=40 * MIB),
        name="band_attn",
    )(qt, k_pad, vt_pad, rel_rows)


def _cumsum_rows(x):
    row = lax.broadcasted_iota(jnp.int32, x.shape, 0)
    d = 1
    while d < x.shape[0]:
        x = x + jnp.where(row >= d, pltpu.roll(x, d, axis=0), 0.0)
        d *= 2
    return x


def _mlstm_kernel(mqk_ref, mv_ref, mo_ref, gcol_ref, cw_ref, cb_ref, gb_ref, ng_ref, o_ref,
                  ct_scr, n_scr, m_scr, conv_scr):
    c = pl.program_id(0)
    hd = MLSTM_HEAD_DIM
    tail = SUBLANES

    @pl.when(c == 0)
    def _():
        ct_scr[...] = jnp.zeros_like(ct_scr)
        n_scr[...] = jnp.zeros_like(n_scr)
        m_scr[...] = jnp.zeros_like(m_scr)
        conv_scr[:, :tail, :] = jnp.zeros((BATCH, tail, 2 * MLSTM_WIDTH), F32)

    t_idx = lax.broadcasted_iota(jnp.int32, (CHUNK, CHUNK), 0)
    s_idx = lax.broadcasted_iota(jnp.int32, (CHUNK, CHUNK), 1)
    causal = s_idx <= t_idx
    k_scale = 1.0 / math.sqrt(hd)

    for b in range(BATCH):
        xb = mqk_ref[b].astype(F32)
        conv_scr[b, tail:, :] = xb
        acc = cb_ref[...]
        for j in range(CONV_WIDTH):
            lo = tail - (CONV_WIDTH - 1) + j
            acc = acc + conv_scr[b, lo:lo + CHUNK, :] * cw_ref[j:j + 1, :]
        conv_scr[b, :tail, :] = xb[CHUNK - tail:, :]
        act = acc * jax.nn.sigmoid(acc)
        q_all = act[:, :MLSTM_WIDTH]
        k_all = act[:, MLSTM_WIDTH:] * k_scale

        gates = gcol_ref[b] + gb_ref[...]
        logf = jnp.minimum(gates, 0.0) - jnp.log1p(jnp.exp(-jnp.abs(gates)))
        bcum = _cumsum_rows(logf)
        r_cols = gates - pltpu.roll(bcum, GATE_LANES - MLSTM_HEADS, axis=1)
        r_rows = jnp.transpose(jnp.concatenate([r_cols, jnp.zeros_like(r_cols)], axis=0))

        v_all = mv_ref[b]
        mo_all = mo_ref[b].astype(F32)
        for h in range(MLSTM_HEADS):
            u = b * MLSTM_HEADS + h
            sl = slice(h * hd, (h + 1) * hd)
            qh = q_all[:, sl]
            kh = k_all[:, sl]
            vh = v_all[:, sl]
            qh_b = qh.astype(BF16)
            ig_c = gates[:, h:h + 1]
            bc_c = bcum[:, MLSTM_HEADS + h:MLSTM_HEADS + h + 1]
            b_last = bc_c[CHUNK - 1:CHUNK, :]

            a_c = b_last - bc_c + ig_c
            m_loc = jnp.max(a_c, axis=0, keepdims=True)
            kw = kh * jnp.exp(a_c - m_loc)
            ct_loc = _dot_tn(kw.astype(BF16), vh)
            n_loc = jnp.sum(kw, axis=0, keepdims=True)

            m_prev = m_scr[u, 0:1, 0:1]
            ct_prev = ct_scr[u]
            n_prev = n_scr[u, 0:1, :]
            m_new = jnp.maximum(b_last + m_prev, m_loc)
            s_prev = jnp.exp(b_last + m_prev - m_new)
            s_loc = jnp.exp(m_loc - m_new)
            ct_scr[u] = s_prev * ct_prev + s_loc * ct_loc
            n_scr[u, 0:1, :] = s_prev * n_prev + s_loc * n_loc
            m_scr[u] = jnp.broadcast_to(m_new, (SUBLANES, GATE_LANES))

            g_c = bc_c + m_prev
            dmat = jnp.where(causal, bc_c + r_rows[h:h + 1, :CHUNK], -jnp.inf)
            m_t = jnp.maximum(g_c, jnp.max(dmat, axis=1, keepdims=True))
            dw = jnp.exp(dmat - m_t)
            qk = _dot_nt(qh_b, kh.astype(BF16)) * dw
            inter = jnp.exp(g_c - m_t)
            num = inter * _dot(qh_b, ct_prev.astype(BF16)) + _dot(qk.astype(BF16), vh)
            den = (inter * jnp.sum(qh * n_prev, axis=1, keepdims=True)
                   + jnp.sum(qk, axis=1, keepdims=True))
            hout = num / jnp.maximum(jnp.abs(den), jnp.exp(-m_t))
            y = _rms_norm(hout, ng_ref[:, sl])
            o_ref[b, :, sl] = (jax.nn.sigmoid(mo_all[:, sl]) * y).astype(BF16)


def _mlstm(mqk, mv, mo, gcol, conv_w, conv_b, gate_bias, norm_g):
    blk = lambda width: pl.BlockSpec((BATCH, CHUNK, width), lambda c: (0, c, 0))
    units = BATCH * MLSTM_HEADS
    return pl.pallas_call(
        _mlstm_kernel,
        grid=(NUM_CHUNKS,),
        in_specs=[
            blk(2 * MLSTM_WIDTH), blk(MLSTM_WIDTH), blk(MLSTM_WIDTH), blk(GATE_LANES),
            _resident((CONV_WIDTH, 2 * MLSTM_WIDTH)),
            _resident((1, 2 * MLSTM_WIDTH)),
            _resident((1, GATE_LANES)),
            _resident((1, MLSTM_WIDTH)),
        ],
        out_specs=blk(MLSTM_WIDTH),
        out_shape=jax.ShapeDtypeStruct((BATCH, SEQ, MLSTM_WIDTH), BF16),
        scratch_shapes=[
            pltpu.VMEM((units, MLSTM_HEAD_DIM, MLSTM_HEAD_DIM), F32),
            pltpu.VMEM((units, SUBLANES, MLSTM_HEAD_DIM), F32),
            pltpu.VMEM((units, SUBLANES, GATE_LANES), F32),
            pltpu.VMEM((BATCH, SUBLANES + CHUNK, 2 * MLSTM_WIDTH), F32),
        ],
        compiler_params=pltpu.CompilerParams(
            dimension_semantics=("arbitrary",), vmem_limit_bytes=32 * MIB),
        name="mlstm",
    )(mqk, mv, mo, gcol, conv_w, conv_b, gate_bias, norm_g)


def _merge_kernel(x_ref, att_ref, ml_ref, gate_ref, wa_ref, wm_ref, wo_ref, o_ref):
    ga = gate_ref[:, :D_MODEL].astype(F32)
    gm = gate_ref[:, D_MODEL:].astype(F32)
    y = (jax.nn.sigmoid(ga) * _dot(att_ref[...], wa_ref[...])
         + jax.nn.sigmoid(gm) * _dot(ml_ref[...], wm_ref[...]))
    o_ref[...] = x_ref[...] + _dot(y.astype(BF16), wo_ref[...])


def _merge(x2d, att, ml, gates, wa, wm, wo):
    tm = TOKEN_TILE
    row = lambda width: pl.BlockSpec((tm, width), lambda i: (i, 0))
    return pl.pallas_call(
        _merge_kernel,
        grid=(TOKENS // tm,),
        in_specs=[row(D_MODEL), row(ATT_WIDTH), row(MLSTM_WIDTH), row(2 * D_MODEL),
                  _resident((ATT_WIDTH, D_MODEL)), _resident((MLSTM_WIDTH, D_MODEL)),
                  _resident((D_MODEL, D_MODEL))],
        out_specs=row(D_MODEL),
        out_shape=jax.ShapeDtypeStruct((TOKENS, D_MODEL), F32),
        compiler_params=pltpu.CompilerParams(
            dimension_semantics=("arbitrary",), vmem_limit_bytes=40 * MIB),
        name="merge",
    )(x2d, att, ml, gates, wa, wm, wo)


def _ffn_kernel(x_ref, g_ref, wu_ref, wd_ref, fg_ref, o_ref, *, final_norm):
    x = x_ref[...]
    hn = _rms_norm(x, g_ref[...]).astype(BF16)
    acc = x
    for j in range(D_FF // FF_TILE):
        cols = slice(j * FF_TILE, (j + 1) * FF_TILE)
        h = jnp.maximum(_dot(hn, wu_ref[:, cols]), 0.0)
        acc = acc + _dot((h * h).astype(BF16), wd_ref[cols, :])
    if final_norm:
        acc = _rms_norm(acc, fg_ref[...])
    o_ref[...] = acc


def _ffn(x2d, g, wu, wd, final_g, final_norm):
    tm = TOKEN_TILE
    row = pl.BlockSpec((tm, D_MODEL), lambda i: (i, 0))
    return pl.pallas_call(
        functools.partial(_ffn_kernel, final_norm=final_norm),
        grid=(TOKENS // tm,),
        in_specs=[row, _resident((1, D_MODEL)), _resident((D_MODEL, D_FF)),
                  _resident((D_FF, D_MODEL)), _resident((1, D_MODEL))],
        out_specs=row,
        out_shape=jax.ShapeDtypeStruct((TOKENS, D_MODEL), F32),
        compiler_params=pltpu.CompilerParams(
            dimension_semantics=("arbitrary",), vmem_limit_bytes=48 * MIB),
        name="ffn",
    )(x2d, g, wu, wd, final_g)


def _rel_bias_rows(rel_bias):
    far = jnp.broadcast_to(rel_bias[:, 2 * MAX_REL_DIST:], (ATT_HEADS, MAX_REL_DIST))
    near = rel_bias[:, MAX_REL_DIST + 1 - Q_TILE:][:, ::-1]
    rest = jnp.broadcast_to(rel_bias[:, 2 * MAX_REL_DIST:],
                            (ATT_HEADS, REL_ROW - MAX_REL_DIST - near.shape[1]))
    return jnp.concatenate([far, near, rest], axis=1).astype(F32) * LOG2E


def kernel(x, mix_norm_g, w_in, conv_w, conv_b, b_igate, b_fgate, rel_bias, mh_norm_g,
           w_att_proj, w_mlstm_proj, w_out, ffn_norm_g, w_up, w_down, final_norm_g):
    h = x.reshape(TOKENS, D_MODEL)
    a = ATT_WIDTH
    gate_lo = 3 * ATT_WIDTH + 4 * MLSTM_WIDTH
    gate_hi = gate_lo + 2 * MLSTM_HEADS
    final_g = final_norm_g.reshape(1, D_MODEL)
    for l in range(DEPTH):
        w = w_in[l]
        w_rows = jnp.concatenate([w[:, a:2 * a], w[:, 3 * a:gate_lo], w[:, gate_hi:]],
                                 axis=1).astype(BF16)
        w_cols = jnp.concatenate([w[:, :a], w[:, 2 * a:3 * a]], axis=1).T.astype(BF16)
        w_gate = jnp.pad(w[:, gate_lo:gate_hi],
                         ((0, 0), (0, GATE_LANES - 2 * MLSTM_HEADS))).astype(BF16)
        gate_bias = jnp.pad(jnp.concatenate([b_igate[l], b_fgate[l]]),
                            (0, GATE_LANES - 2 * MLSTM_HEADS)).reshape(1, GATE_LANES)

        qt, k_pad, vt_pad, mqk, mv, mo, gates, gcol = _in_proj(
            h, mix_norm_g[l].reshape(1, D_MODEL), w_rows, w_cols, w_gate,
            jnp.zeros((BATCH, SEQ_PAD, ATT_WIDTH), BF16),
            jnp.zeros((BATCH, SEQ_PAD // LANES, ATT_WIDTH, LANES), BF16))
        att = _band_attn(qt, k_pad, vt_pad, _rel_bias_rows(rel_bias[l]))
        ml = _mlstm(mqk.reshape(BATCH, SEQ, 2 * MLSTM_WIDTH),
                    mv.reshape(BATCH, SEQ, MLSTM_WIDTH),
                    mo.reshape(BATCH, SEQ, MLSTM_WIDTH),
                    gcol.reshape(BATCH, SEQ, GATE_LANES),
                    conv_w[l], conv_b[l].reshape(1, 2 * MLSTM_WIDTH), gate_bias,
                    mh_norm_g[l].reshape(1, MLSTM_WIDTH))
        h = _merge(h, att.reshape(TOKENS, ATT_WIDTH), ml.reshape(TOKENS, MLSTM_WIDTH), gates,
                   w_att_proj[l].astype(BF16), w_mlstm_proj[l].astype(BF16),
                   w_out[l].astype(BF16))
        h = _ffn(h, ffn_norm_g[l].reshape(1, D_MODEL), w_up[l].astype(BF16),
                 w_down[l].astype(BF16), final_g, final_norm=(l == DEPTH - 1))
    return h.reshape(BATCH, SEQ, D_MODEL)
```

```python
import functools
import math

import jax
import jax.numpy as jnp
from jax import lax
from jax.experimental import pallas as pl
from jax.experimental.pallas import tpu as pltpu

D_MODEL = 1024
BATCH = 4
SEQ = 4096
DEPTH = 2
CHUNK = 64
NORM_EPS = 1e-6
ATT_HEADS = 8
ATT_HEAD_DIM = 64
ATT_WIDTH = ATT_HEADS * ATT_HEAD_DIM
LEFT_CHUNKS = 8
BAND_CHUNKS = LEFT_CHUNKS + 1
BAND = BAND_CHUNKS * CHUNK
MAX_REL_DIST = 256
MLSTM_HEADS = 4
MLSTM_HEAD_DIM = 128
MLSTM_WIDTH = MLSTM_HEADS * MLSTM_HEAD_DIM
CONV_WIDTH = 4
D_FF = 4 * D_MODEL

LANES = 128
SUBLANES = 8

TOKENS = BATCH * SEQ
NUM_CHUNKS = SEQ // CHUNK
PAD_ROWS = LEFT_CHUNKS * CHUNK
SEQ_PAD = SEQ + PAD_ROWS
ROW_COLS = ATT_WIDTH + 2 * MLSTM_WIDTH + 2 * D_MODEL
COL_ROWS = 2 * ATT_WIDTH + 2 * MLSTM_WIDTH
GATE_ROWS = 2 * MLSTM_HEADS

Q_TILE = 2 * CHUNK
WINDOW = BAND + CHUNK
WINDOW_TILES = WINDOW // LANES
ATT_STEPS = SEQ // Q_TILE
REL_ROW = 1024
LOG2E = 1.4426950408889634
MASKED = -1e30

ML_CHUNK = LANES
ML_STEPS = SEQ // ML_CHUNK
ML_UNITS = BATCH * MLSTM_HEADS
CONV_TAIL = 16
SPLIT_ROWS = 16

TOKEN_TILE = 512
FF_TILE = 1024

F32 = jnp.float32
BF16 = jnp.bfloat16
MIB = 1024 * 1024


def _resident(shape):
    nd = len(shape)
    return pl.BlockSpec(shape, lambda *_: (0,) * nd, pipeline_mode=pl.Buffered(1))


def _rms_norm(x, g):
    ms = jnp.mean(x * x, axis=-1, keepdims=True)
    return x * lax.rsqrt(ms + NORM_EPS) * g


def _dot(a, b):
    return jnp.dot(a, b, preferred_element_type=F32)


def _dot_nt(a, b):
    return lax.dot_general(a, b, (((1,), (1,)), ((), ())), preferred_element_type=F32)


def _dot_tn(a, b):
    return lax.dot_general(a, b, (((0,), (0,)), ((), ())), preferred_element_type=F32)


def _in_proj_kernel(x_ref, g_ref, w_ref, wt_ref, wg_ref, kz_ref, vz_ref,
                    qt_ref, k_ref, vt_ref, mqk_ref, mvt_ref, mot_ref, gate_ref, gt_ref):
    del kz_ref, vz_ref
    xn = _rms_norm(x_ref[...], g_ref[...]).astype(BF16)

    def seg(lo, width):
        return _dot(xn, w_ref[:, lo:lo + width])

    def seg_t(lo, width):
        return _dot_nt(wt_ref[lo:lo + width, :], xn)

    a = ATT_WIDTH
    m = MLSTM_WIDTH
    qt_ref[...] = (seg_t(0, a) * (LOG2E / math.sqrt(ATT_HEAD_DIM))).astype(BF16)
    vt = seg_t(a, a).astype(BF16)
    for j in range(TOKEN_TILE // LANES):
        vt_ref[0, j] = vt[:, j * LANES:(j + 1) * LANES]
    mvt_ref[0] = seg_t(2 * a, m).astype(BF16)
    mot_ref[0] = seg_t(2 * a + m, m).astype(BF16)
    gt_ref[0] = _dot_nt(wg_ref[...], xn)
    k_ref[0] = seg(0, a).astype(BF16)
    mqk_ref[:, :m] = seg(a, m).astype(BF16)
    mqk_ref[:, m:] = seg(a + m, m).astype(BF16)
    base = a + 2 * m
    for j in range(2 * D_MODEL // 512):
        gate_ref[:, j * 512:(j + 1) * 512] = seg(base + j * 512, 512).astype(BF16)


def _in_proj(x2d, g, w_rows, w_cols, w_gate, kz, vz):
    tm = TOKEN_TILE
    tiles_per_batch = SEQ // tm
    pad_tiles = PAD_ROWS // tm
    lane_tiles = tm // LANES
    row = lambda width: pl.BlockSpec((tm, width), lambda i: (i, 0))
    col = lambda height: pl.BlockSpec((1, height, tm),
                                      lambda i: (i // tiles_per_batch, 0, i % tiles_per_batch))
    k_spec = pl.BlockSpec((1, tm, ATT_WIDTH),
                          lambda i: (i // tiles_per_batch, i % tiles_per_batch + pad_tiles, 0))
    vt_spec = pl.BlockSpec((1, lane_tiles, ATT_WIDTH, LANES),
                           lambda i: (i // tiles_per_batch, i % tiles_per_batch + pad_tiles, 0, 0))
    out_shape = (
        jax.ShapeDtypeStruct((ATT_WIDTH, TOKENS), BF16),
        jax.ShapeDtypeStruct((BATCH, SEQ_PAD, ATT_WIDTH), BF16),
        jax.ShapeDtypeStruct((BATCH, SEQ_PAD // LANES, ATT_WIDTH, LANES), BF16),
        jax.ShapeDtypeStruct((TOKENS, 2 * MLSTM_WIDTH), BF16),
        jax.ShapeDtypeStruct((BATCH, MLSTM_WIDTH, SEQ), BF16),
        jax.ShapeDtypeStruct((BATCH, MLSTM_WIDTH, SEQ), BF16),
        jax.ShapeDtypeStruct((TOKENS, 2 * D_MODEL), BF16),
        jax.ShapeDtypeStruct((BATCH, GATE_ROWS, SEQ), F32),
    )
    return pl.pallas_call(
        _in_proj_kernel,
        grid=(TOKENS // tm,),
        in_specs=[
            row(D_MODEL),
            _resident((1, D_MODEL)),
            _resident((D_MODEL, ROW_COLS)),
            _resident((COL_ROWS, D_MODEL)),
            _resident((GATE_ROWS, D_MODEL)),
            pl.BlockSpec(memory_space=pl.ANY),
            pl.BlockSpec(memory_space=pl.ANY),
        ],
        out_specs=(pl.BlockSpec((ATT_WIDTH, tm), lambda i: (0, i)), k_spec, vt_spec,
                   row(2 * MLSTM_WIDTH), col(MLSTM_WIDTH), col(MLSTM_WIDTH), row(2 * D_MODEL),
                   col(GATE_ROWS)),
        out_shape=out_shape,
        input_output_aliases={5: 1, 6: 2},
        compiler_params=pltpu.CompilerParams(
            dimension_semantics=("arbitrary",), vmem_limit_bytes=48 * MIB),
        name="in_proj",
    )(x2d, g, w_rows, w_cols, w_gate, kz, vz)


def _build_bias_table(rel_ref, bias_scr):
    kb = lax.broadcasted_iota(jnp.int32, (WINDOW, LANES), 0)
    lane = lax.broadcasted_iota(jnp.int32, (WINDOW, LANES), 1)
    band_lo = jnp.where(lane >= CHUNK, CHUNK, 0)
    in_band = (kb >= band_lo) & (kb < band_lo + BAND)
    for h in range(ATT_HEADS):
        rows = jnp.broadcast_to(rel_ref[h:h + 1, :], (LANES, REL_ROW))
        shifted = pltpu.roll(rows, 0, axis=1, stride=1, stride_axis=0)
        table = jnp.concatenate(
            [jnp.transpose(shifted[:, j * LANES:(j + 1) * LANES]) for j in range(WINDOW_TILES)],
            axis=0)
        bias_scr[h] = jnp.where(in_band, table, MASKED)


def _band_attn_kernel(qt_ref, k_ref, vt_ref, rel_ref, o_ref, bias_scr):
    step = pl.program_id(1)

    @pl.when((pl.program_id(0) == 0) & (step == 0))
    def _():
        _build_bias_table(rel_ref, bias_scr)

    start = pl.multiple_of(step * Q_TILE, Q_TILE)
    first_chunk = 2 * step
    zeros_half = jnp.zeros((ATT_HEAD_DIM, LANES), BF16)
    outs = []
    for pair in range(ATT_HEADS // 2):
        lanes = slice(pair * LANES, (pair + 1) * LANES)
        kwin = k_ref[0, pl.ds(start, WINDOW), lanes]
        for hh in range(2):
            h = 2 * pair + hh
            rows = slice(h * ATT_HEAD_DIM, (h + 1) * ATT_HEAD_DIM)
            qh = qt_ref[rows, :]
            qm = jnp.concatenate([qh, zeros_half] if hh == 0 else [zeros_half, qh], axis=0)
            st = _dot(kwin, qm)
            blocks = []
            for j in range(WINDOW // CHUNK):
                r = slice(j * CHUNK, (j + 1) * CHUNK)
                blk = st[r] + bias_scr[h, r, :]
                if j < LEFT_CHUNKS:
                    blk = jnp.where(first_chunk + j >= LEFT_CHUNKS, blk, MASKED)
                blocks.append(blk)
            m = jnp.max(functools.reduce(jnp.maximum, blocks), axis=0, keepdims=True)
            probs = [jnp.exp2(blk - m) for blk in blocks]
            denom = jnp.sum(functools.reduce(jnp.add, probs), axis=0, keepdims=True)
            pt = jnp.concatenate(probs, axis=0).astype(BF16)
            acc = None
            for j in range(WINDOW_TILES):
                part = _dot(vt_ref[0, step + j, rows, :], pt[j * LANES:(j + 1) * LANES])
                acc = part if acc is None else acc + part
            outs.append(acc / denom)
    o_ref[0] = jnp.transpose(jnp.concatenate(outs, axis=0)).astype(BF16)


def _band_attn(qt, k_pad, vt_pad, rel_rows):
    return pl.pallas_call(
        _band_attn_kernel,
        grid=(BATCH, ATT_STEPS),
        in_specs=[
            pl.BlockSpec((ATT_WIDTH, Q_TILE), lambda b, s: (0, b * ATT_STEPS + s)),
            pl.BlockSpec((1, SEQ_PAD, ATT_WIDTH), lambda b, s: (b, 0, 0)),
            pl.BlockSpec((1, SEQ_PAD // LANES, ATT_WIDTH, LANES), lambda b, s: (b, 0, 0, 0)),
            _resident((ATT_HEADS, REL_ROW)),
        ],
        out_specs=pl.BlockSpec((1, Q_TILE, ATT_WIDTH), lambda b, s: (b, s, 0)),
        out_shape=jax.ShapeDtypeStruct((BATCH, SEQ, ATT_WIDTH), BF16),
        scratch_shapes=[pltpu.VMEM((ATT_HEADS, WINDOW, LANES), F32)],
        compiler_params=pltpu.CompilerParams(
            dimension_semantics=("arbitrary", "arbitrary"), vmem_limit_bytes=40 * MIB),
        name="band_attn",
    )(qt, k_pad, vt_pad, rel_rows)


def _cumsum_lanes(x):
    lane = lax.broadcasted_iota(jnp.int32, x.shape, 1)
    d = 1
    while d < x.shape[1]:
        x = x + jnp.where(lane >= d, pltpu.roll(x, d, axis=1), 0.0)
        d *= 2
    return x


def _split3(x):
    hi = x.astype(BF16).astype(F32)
    r1 = x - hi
    mid = r1.astype(BF16).astype(F32)
    lo = (r1 - mid).astype(BF16).astype(F32)
    return hi, mid, lo


def _mlstm_kernel(mqk_ref, vt_ref, mot_ref, gt_ref, cw_ref, cb_ref, gb_ref, ng_ref, o_ref,
                  c_scr, n_scr, m_scr, x_scr):
    step = pl.program_id(0)
    hd = MLSTM_HEAD_DIM
    rows_all = BATCH * GATE_ROWS

    @pl.when(step == 0)
    def _():
        c_scr[...] = jnp.zeros_like(c_scr)
        n_scr[...] = jnp.zeros_like(n_scr)
        m_scr[...] = jnp.zeros_like(m_scr)
        x_scr[:, :CONV_TAIL, :] = jnp.zeros((BATCH, CONV_TAIL, 2 * MLSTM_WIDTH), BF16)

    def bcast(col):
        return jnp.broadcast_to(col, (rows_all, ML_CHUNK))

    gates = jnp.concatenate([gt_ref[b] for b in range(BATCH)], axis=0) + gb_ref[...]
    logf = jnp.minimum(gates, 0.0) - jnp.log1p(jnp.exp(-jnp.abs(gates)))
    bcum = pltpu.roll(_cumsum_lanes(logf), rows_all - MLSTM_HEADS, axis=0)
    b_last = bcast(bcum[:, ML_CHUNK - 1:ML_CHUNK])
    a_row = b_last - bcum + gates
    m_loc = bcast(jnp.max(a_row, axis=1, keepdims=True))
    w_row = jnp.exp(a_row - m_loc)
    m_prev = m_scr[...]
    m_new = jnp.maximum(b_last + m_prev, m_loc)
    s_prev = jnp.exp(b_last + m_prev - m_new)
    s_loc = jnp.exp(m_loc - m_new)
    m_scr[...] = m_new
    g_row = bcum + m_prev
    n_prev = n_scr[...]
    w_bf = w_row.astype(BF16)
    n_prev_bf = n_prev.astype(BF16)
    r_parts = _split3(gates - bcum)
    b_parts = _split3(bcum)
    split_row = lax.broadcasted_iota(jnp.int32, (SPLIT_ROWS, ML_CHUNK), 0)

    def outer_sum_operand(parts, one, first):
        other = 3 - first
        out = jnp.where((split_row >= other) & (split_row < other + 3), 1.0, 0.0)
        for i, part in enumerate(parts):
            out = jnp.where(split_row == first + i, part[one], out)
        return out.astype(BF16)

    src = lax.broadcasted_iota(jnp.int32, (ML_CHUNK, ML_CHUNK), 0)
    qry = lax.broadcasted_iota(jnp.int32, (ML_CHUNK, ML_CHUNK), 1)
    causal = src <= qry
    out_row = lax.broadcasted_iota(jnp.int32, (ML_CHUNK, CONV_TAIL + ML_CHUNK), 0)
    in_row = lax.broadcasted_iota(jnp.int32, (ML_CHUNK, CONV_TAIL + ML_CHUNK), 1)
    shifts = [(in_row == out_row + (CONV_TAIL - (CONV_WIDTH - 1) + j)).astype(BF16)
              for j in range(CONV_WIDTH - 1)]
    k_scale = 1.0 / math.sqrt(hd)
    new_n = []

    for b in range(BATCH):
        cur = mqk_ref[b]
        x_scr[b, CONV_TAIL:, :] = cur
        window = x_scr[b]
        acc = cb_ref[...] + cur.astype(F32) * cw_ref[CONV_WIDTH - 1:CONV_WIDTH, :]
        for j in range(CONV_WIDTH - 1):
            acc = acc + _dot(shifts[j], window) * cw_ref[j:j + 1, :]
        x_scr[b, :CONV_TAIL, :] = cur[ML_CHUNK - CONV_TAIL:, :]
        act = acc * jax.nn.sigmoid(acc)
        q_all = act[:, :MLSTM_WIDTH].astype(BF16)
        k_all = (act[:, MLSTM_WIDTH:] * k_scale).astype(BF16)

        grp = slice(b * GATE_ROWS, (b + 1) * GATE_ROWS)
        outs = []
        for h in range(MLSTM_HEADS):
            u = b * MLSTM_HEADS + h
            r = b * GATE_ROWS + h
            one = slice(r, r + 1)
            sl = slice(h * hd, (h + 1) * hd)
            qh = q_all[:, sl]
            kh = k_all[:, sl]
            vt = vt_ref[b, sl, :]

            c_prev = c_scr[u]
            c_loc = _dot((vt.astype(F32) * w_row[one]).astype(BF16), kh)
            n_loc = _dot(w_bf[grp], kh)[h:h + 1]
            c_scr[u] = s_prev[one] * c_prev + s_loc[one] * c_loc
            new_n.append(s_prev[one] * n_prev[one] + s_loc[one] * n_loc)
            if h == MLSTM_HEADS - 1:
                new_n.append(jnp.zeros((GATE_ROWS - MLSTM_HEADS, hd), F32))

            lhs = outer_sum_operand(r_parts, one, 0)
            rhs = outer_sum_operand(b_parts, one, 3)
            dmat = jnp.where(causal, _dot_tn(lhs, rhs), -jnp.inf)
            m_t = jnp.maximum(g_row[one], jnp.max(dmat, axis=0, keepdims=True))
            pt = _dot_nt(kh, qh) * jnp.exp(dmat - m_t)
            inter = jnp.exp(g_row[one] - m_t)
            nq = _dot_nt(n_prev_bf[grp], qh)[h:h + 1]
            den = inter * nq + jnp.sum(pt, axis=0, keepdims=True)
            num = inter * _dot_nt(c_prev.astype(BF16), qh) + _dot(vt, pt.astype(BF16))
            hout = num * (1.0 / jnp.maximum(jnp.abs(den), jnp.exp(-m_t)))
            ms = jnp.mean(hout * hout, axis=0, keepdims=True)
            y = hout * lax.rsqrt(ms + NORM_EPS) * ng_ref[sl, :]
            outs.append(jax.nn.sigmoid(mot_ref[b, sl, :].astype(F32)) * y)
        o_ref[b] = jnp.transpose(jnp.concatenate(outs, axis=0)).astype(BF16)

    n_scr[...] = jnp.concatenate(new_n, axis=0)


def _mlstm(mqk, mvt, mot, gt, conv_w, conv_b, gate_bias, norm_g):
    tok = lambda width: pl.BlockSpec((BATCH, ML_CHUNK, width), lambda c: (0, c, 0))
    feat = lambda height: pl.BlockSpec((BATCH, height, ML_CHUNK), lambda c: (0, 0, c))
    rows_all = BATCH * GATE_ROWS
    return pl.pallas_call(
        _mlstm_kernel,
        grid=(ML_STEPS,),
        in_specs=[
            tok(2 * MLSTM_WIDTH), feat(MLSTM_WIDTH), feat(MLSTM_WIDTH), feat(GATE_ROWS),
            _resident((CONV_WIDTH, 2 * MLSTM_WIDTH)),
            _resident((1, 2 * MLSTM_WIDTH)),
            _resident((rows_all, ML_CHUNK)),
            _resident((MLSTM_WIDTH, ML_CHUNK)),
        ],
        out_specs=tok(MLSTM_WIDTH),
        out_shape=jax.ShapeDtypeStruct((BATCH, SEQ, MLSTM_WIDTH), BF16),
        scratch_shapes=[
            pltpu.VMEM((ML_UNITS, MLSTM_HEAD_DIM, MLSTM_HEAD_DIM), F32),
            pltpu.VMEM((rows_all, MLSTM_HEAD_DIM), F32),
            pltpu.VMEM((rows_all, ML_CHUNK), F32),
            pltpu.VMEM((BATCH, CONV_TAIL + ML_CHUNK, 2 * MLSTM_WIDTH), BF16),
        ],
        compiler_params=pltpu.CompilerParams(
            dimension_semantics=("arbitrary",), vmem_limit_bytes=32 * MIB),
        name="mlstm",
    )(mqk, mvt, mot, gt, conv_w, conv_b, gate_bias, norm_g)


def _merge_kernel(x_ref, att_ref, ml_ref, gate_ref, wa_ref, wm_ref, wo_ref, o_ref):
    ga = gate_ref[:, :D_MODEL].astype(F32)
    gm = gate_ref[:, D_MODEL:].astype(F32)
    y = (jax.nn.sigmoid(ga) * _dot(att_ref[...], wa_ref[...])
         + jax.nn.sigmoid(gm) * _dot(ml_ref[...], wm_ref[...]))
    o_ref[...] = x_ref[...] + _dot(y.astype(BF16), wo_ref[...])


def _merge(x2d, att, ml, gates, wa, wm, wo):
    tm = TOKEN_TILE
    row = lambda width: pl.BlockSpec((tm, width), lambda i: (i, 0))
    return pl.pallas_call(
        _merge_kernel,
        grid=(TOKENS // tm,),
        in_specs=[row(D_MODEL), row(ATT_WIDTH), row(MLSTM_WIDTH), row(2 * D_MODEL),
                  _resident((ATT_WIDTH, D_MODEL)), _resident((MLSTM_WIDTH, D_MODEL)),
                  _resident((D_MODEL, D_MODEL))],
        out_specs=row(D_MODEL),
        out_shape=jax.ShapeDtypeStruct((TOKENS, D_MODEL), F32),
        compiler_params=pltpu.CompilerParams(
            dimension_semantics=("arbitrary",), vmem_limit_bytes=40 * MIB),
        name="merge",
    )(x2d, att, ml, gates, wa, wm, wo)


def _ffn_kernel(x_ref, g_ref, wu_ref, wd_ref, fg_ref, o_ref, *, final_norm):
    x = x_ref[...]
    hn = _rms_norm(x, g_ref[...]).astype(BF16)
    acc = x
    for j in range(D_FF // FF_TILE):
        cols = slice(j * FF_TILE, (j + 1) * FF_TILE)
        h = jnp.maximum(_dot(hn, wu_ref[:, cols]), 0.0)
        acc = acc + _dot((h * h).astype(BF16), wd_ref[cols, :])
    if final_norm:
        acc = _rms_norm(acc, fg_ref[...])
    o_ref[...] = acc


def _ffn(x2d, g, wu, wd, final_g, final_norm):
    tm = TOKEN_TILE
    row = pl.BlockSpec((tm, D_MODEL), lambda i: (i, 0))
    return pl.pallas_call(
        functools.partial(_ffn_kernel, final_norm=final_norm),
        grid=(TOKENS // tm,),
        in_specs=[row, _resident((1, D_MODEL)), _resident((D_MODEL, D_FF)),
                  _resident((D_FF, D_MODEL)), _resident((1, D_MODEL))],
        out_specs=row,
        out_shape=jax.ShapeDtypeStruct((TOKENS, D_MODEL), F32),
        compiler_params=pltpu.CompilerParams(
            dimension_semantics=("arbitrary",), vmem_limit_bytes=48 * MIB),
        name="ffn",
    )(x2d, g, wu, wd, final_g)


def _rel_bias_rows(rel_bias):
    far = jnp.broadcast_to(rel_bias[:, 2 * MAX_REL_DIST:], (ATT_HEADS, MAX_REL_DIST))
    near = rel_bias[:, MAX_REL_DIST + 1 - Q_TILE:][:, ::-1]
    rest = jnp.broadcast_to(rel_bias[:, 2 * MAX_REL_DIST:],
                            (ATT_HEADS, REL_ROW - MAX_REL_DIST - near.shape[1]))
    return jnp.concatenate([far, near, rest], axis=1).astype(F32) * LOG2E


def kernel(x, mix_norm_g, w_in, conv_w, conv_b, b_igate, b_fgate, rel_bias, mh_norm_g,
           w_att_proj, w_mlstm_proj, w_out, ffn_norm_g, w_up, w_down, final_norm_g):
    h = x.reshape(TOKENS, D_MODEL)
    a = ATT_WIDTH
    m = MLSTM_WIDTH
    gate_lo = 3 * a + 4 * m
    gate_hi = gate_lo + GATE_ROWS
    final_g = final_norm_g.reshape(1, D_MODEL)
    for l in range(DEPTH):
        w = w_in[l]
        w_rows = jnp.concatenate([w[:, a:2 * a], w[:, 3 * a:3 * a + 2 * m], w[:, gate_hi:]],
                                 axis=1).astype(BF16)
        w_cols = jnp.concatenate([w[:, :a], w[:, 2 * a:3 * a], w[:, 3 * a + 2 * m:gate_lo]],
                                 axis=1).T.astype(BF16)
        w_gate = w[:, gate_lo:gate_hi].T.astype(BF16)
        gate_bias = jnp.tile(
            jnp.broadcast_to(jnp.concatenate([b_igate[l], b_fgate[l]])[:, None],
                             (GATE_ROWS, ML_CHUNK)), (BATCH, 1))
        norm_g = jnp.broadcast_to(mh_norm_g[l][:, None], (MLSTM_WIDTH, ML_CHUNK))

        qt, k_pad, vt_pad, mqk, mvt, mot, gates, gt = _in_proj(
            h, mix_norm_g[l].reshape(1, D_MODEL), w_rows, w_cols, w_gate,
            jnp.zeros((BATCH, SEQ_PAD, ATT_WIDTH), BF16),
            jnp.zeros((BATCH, SEQ_PAD // LANES, ATT_WIDTH, LANES), BF16))
        att = _band_attn(qt, k_pad, vt_pad, _rel_bias_rows(rel_bias[l]))
        ml = _mlstm(mqk.reshape(BATCH, SEQ, 2 * MLSTM_WIDTH), mvt, mot, gt,
                    conv_w[l], conv_b[l].reshape(1, 2 * MLSTM_WIDTH), gate_bias, norm_g)
        h = _merge(h, att.reshape(TOKENS, ATT_WIDTH), ml.reshape(TOKENS, MLSTM_WIDTH), gates,
                   w_att_proj[l].astype(BF16), w_mlstm_proj[l].astype(BF16),
                   w_out[l].astype(BF16))
        h = _ffn(h, ffn_norm_g[l].reshape(1, D_MODEL), w_up[l].astype(BF16),
                 w_down[l].astype(BF16), final_g, final_norm=(l == DEPTH - 1))
    return h.reshape(BATCH, SEQ, D_MODEL)
```

```python
import functools
import math

import jax
import jax.numpy as jnp
from jax import lax
from jax.experimental import pallas as pl
from jax.experimental.pallas import tpu as pltpu

D_MODEL = 1024
BATCH = 4
SEQ = 4096
DEPTH = 2
CHUNK = 64
NORM_EPS = 1e-6
ATT_HEADS = 8
ATT_HEAD_DIM = 64
ATT_WIDTH = ATT_HEADS * ATT_HEAD_DIM
LEFT_CHUNKS = 8
BAND_CHUNKS = LEFT_CHUNKS + 1
BAND = BAND_CHUNKS * CHUNK
MAX_REL_DIST = 256
MLSTM_HEADS = 4
MLSTM_HEAD_DIM = 128
MLSTM_WIDTH = MLSTM_HEADS * MLSTM_HEAD_DIM
CONV_WIDTH = 4
D_FF = 4 * D_MODEL

LANES = 128
SUBLANES = 8

TOKENS = BATCH * SEQ
NUM_CHUNKS = SEQ // CHUNK
LEFT_ROWS = LEFT_CHUNKS * CHUNK
ROW_COLS = ATT_WIDTH + 2 * MLSTM_WIDTH + 2 * D_MODEL
COL_ROWS = 2 * ATT_WIDTH + 2 * MLSTM_WIDTH
GATE_ROWS = 2 * MLSTM_HEADS

Q_TILE = 2 * CHUNK
WINDOW = BAND + CHUNK
WINDOW_TILES = WINDOW // LANES
ATT_STEPS = SEQ // Q_TILE
REL_ROW = 1024
LOG2E = 1.4426950408889634
MASKED = -1e30

ML_CHUNK = LANES
ML_STEPS = SEQ // ML_CHUNK
ML_UNITS = BATCH * MLSTM_HEADS
CONV_TAIL = 16
SPLIT_ROWS = 16

TOKEN_TILE = 512
FF_TILE = 1024

F32 = jnp.float32
BF16 = jnp.bfloat16
MIB = 1024 * 1024


def _resident(shape):
    nd = len(shape)
    return pl.BlockSpec(shape, lambda *_: (0,) * nd, pipeline_mode=pl.Buffered(1))


def _rms_norm(x, g):
    ms = jnp.mean(x * x, axis=-1, keepdims=True)
    return x * lax.rsqrt(ms + NORM_EPS) * g


def _dot(a, b):
    return jnp.dot(a, b, preferred_element_type=F32)


def _dot_nt(a, b):
    return lax.dot_general(a, b, (((1,), (1,)), ((), ())), preferred_element_type=F32)


def _dot_tn(a, b):
    return lax.dot_general(a, b, (((0,), (0,)), ((), ())), preferred_element_type=F32)


def _in_proj_kernel(x_ref, g_ref, w_ref, wt_ref, wg_ref,
                    qt_ref, k_ref, vt_ref, mqk_ref, mvt_ref, mot_ref, gate_ref, gt_ref):
    xn = _rms_norm(x_ref[...], g_ref[...]).astype(BF16)

    def seg(lo, width):
        return _dot(xn, w_ref[:, lo:lo + width])

    def seg_t(lo, width):
        return _dot_nt(wt_ref[lo:lo + width, :], xn)

    a = ATT_WIDTH
    m = MLSTM_WIDTH
    qt_ref[...] = (seg_t(0, a) * (LOG2E / math.sqrt(ATT_HEAD_DIM))).astype(BF16)
    vt = seg_t(a, a).astype(BF16)
    for j in range(TOKEN_TILE // LANES):
        vt_ref[0, j] = vt[:, j * LANES:(j + 1) * LANES]
    mvt_ref[0] = seg_t(2 * a, m).astype(BF16)
    mot_ref[0] = seg_t(2 * a + m, m).astype(BF16)
    gt_ref[0] = _dot_nt(wg_ref[...], xn)
    k_ref[0] = seg(0, a).astype(BF16)
    mqk_ref[:, :m] = seg(a, m).astype(BF16)
    mqk_ref[:, m:] = seg(a + m, m).astype(BF16)
    base = a + 2 * m
    for j in range(2 * D_MODEL // 512):
        gate_ref[:, j * 512:(j + 1) * 512] = seg(base + j * 512, 512).astype(BF16)


def _in_proj(x2d, g, w_rows, w_cols, w_gate):
    tm = TOKEN_TILE
    tiles_per_batch = SEQ // tm
    lane_tiles = tm // LANES
    row = lambda width: pl.BlockSpec((tm, width), lambda i: (i, 0))
    col = lambda height: pl.BlockSpec((1, height, tm),
                                      lambda i: (i // tiles_per_batch, 0, i % tiles_per_batch))
    k_spec = pl.BlockSpec((1, tm, ATT_WIDTH),
                          lambda i: (i // tiles_per_batch, i % tiles_per_batch, 0))
    vt_spec = pl.BlockSpec((1, lane_tiles, ATT_WIDTH, LANES),
                           lambda i: (i // tiles_per_batch, i % tiles_per_batch, 0, 0))
    out_shape = (
        jax.ShapeDtypeStruct((ATT_WIDTH, TOKENS), BF16),
        jax.ShapeDtypeStruct((BATCH, SEQ, ATT_WIDTH), BF16),
        jax.ShapeDtypeStruct((BATCH, SEQ // LANES, ATT_WIDTH, LANES), BF16),
        jax.ShapeDtypeStruct((TOKENS, 2 * MLSTM_WIDTH), BF16),
        jax.ShapeDtypeStruct((BATCH, MLSTM_WIDTH, SEQ), BF16),
        jax.ShapeDtypeStruct((BATCH, MLSTM_WIDTH, SEQ), BF16),
        jax.ShapeDtypeStruct((TOKENS, 2 * D_MODEL), BF16),
        jax.ShapeDtypeStruct((BATCH, GATE_ROWS, SEQ), F32),
    )
    return pl.pallas_call(
        _in_proj_kernel,
        grid=(TOKENS // tm,),
        in_specs=[
            row(D_MODEL),
            _resident((1, D_MODEL)),
            _resident((D_MODEL, ROW_COLS)),
            _resident((COL_ROWS, D_MODEL)),
            _resident((GATE_ROWS, D_MODEL)),
        ],
        out_specs=(pl.BlockSpec((ATT_WIDTH, tm), lambda i: (0, i)), k_spec, vt_spec,
                   row(2 * MLSTM_WIDTH), col(MLSTM_WIDTH), col(MLSTM_WIDTH), row(2 * D_MODEL),
                   col(GATE_ROWS)),
        out_shape=out_shape,
        compiler_params=pltpu.CompilerParams(
            dimension_semantics=("arbitrary",), vmem_limit_bytes=48 * MIB),
        name="in_proj",
    )(x2d, g, w_rows, w_cols, w_gate)


def _build_bias_table(rel_ref, bias_scr):
    kb = lax.broadcasted_iota(jnp.int32, (WINDOW, LANES), 0)
    lane = lax.broadcasted_iota(jnp.int32, (WINDOW, LANES), 1)
    band_lo = jnp.where(lane >= CHUNK, CHUNK, 0)
    in_band = (kb >= band_lo) & (kb < band_lo + BAND)
    for h in range(ATT_HEADS):
        rows = jnp.broadcast_to(rel_ref[h:h + 1, :], (LANES, REL_ROW))
        shifted = pltpu.roll(rows, 0, axis=1, stride=1, stride_axis=0)
        table = jnp.concatenate(
            [jnp.transpose(shifted[:, j * LANES:(j + 1) * LANES]) for j in range(WINDOW_TILES)],
            axis=0)
        bias_scr[h, :WINDOW, :] = jnp.where(in_band, table, MASKED)
        bias_scr[h, WINDOW:, :] = jnp.full((LEFT_ROWS, LANES), MASKED, F32)


def _band_attn_kernel(qt_ref, k_ref, vt_ref, rel_ref, o_ref, bias_scr):
    step = pl.program_id(1)

    @pl.when((pl.program_id(0) == 0) & (step == 0))
    def _():
        _build_bias_table(rel_ref, bias_scr)

    first_tile = jnp.maximum(step - LEFT_ROWS // Q_TILE, 0)
    start = pl.multiple_of(first_tile * Q_TILE, Q_TILE)
    shift = pl.multiple_of(LEFT_ROWS - (step - first_tile) * Q_TILE, Q_TILE)
    zeros_half = jnp.zeros((ATT_HEAD_DIM, LANES), BF16)
    outs = []
    for pair in range(ATT_HEADS // 2):
        lanes = slice(pair * LANES, (pair + 1) * LANES)
        kwin = k_ref[0, pl.ds(start, WINDOW), lanes]
        for hh in range(2):
            h = 2 * pair + hh
            rows = slice(h * ATT_HEAD_DIM, (h + 1) * ATT_HEAD_DIM)
            qh = qt_ref[rows, :]
            qm = jnp.concatenate([qh, zeros_half] if hh == 0 else [zeros_half, qh], axis=0)
            st = _dot(kwin, qm)
            blocks = [st[j * CHUNK:(j + 1) * CHUNK]
                      + bias_scr[h, pl.ds(shift + j * CHUNK, CHUNK), :]
                      for j in range(WINDOW // CHUNK)]
            m = jnp.max(functools.reduce(jnp.maximum, blocks), axis=0, keepdims=True)
            probs = [jnp.exp2(blk - m) for blk in blocks]
            denom = jnp.sum(functools.reduce(jnp.add, probs), axis=0, keepdims=True)
            pt = jnp.concatenate(probs, axis=0).astype(BF16)
            acc = None
            for j in range(WINDOW_TILES):
                part = _dot(vt_ref[0, first_tile + j, rows, :], pt[j * LANES:(j + 1) * LANES])
                acc = part if acc is None else acc + part
            outs.append(acc / denom)
    o_ref[0] = jnp.transpose(jnp.concatenate(outs, axis=0)).astype(BF16)


def _band_attn(qt, k_pad, vt_pad, rel_rows):
    return pl.pallas_call(
        _band_attn_kernel,
        grid=(BATCH, ATT_STEPS),
        in_specs=[
            pl.BlockSpec((ATT_WIDTH, Q_TILE), lambda b, s: (0, b * ATT_STEPS + s)),
            pl.BlockSpec((1, SEQ, ATT_WIDTH), lambda b, s: (b, 0, 0)),
            pl.BlockSpec((1, SEQ // LANES, ATT_WIDTH, LANES), lambda b, s: (b, 0, 0, 0)),
            _resident((ATT_HEADS, REL_ROW)),
        ],
        out_specs=pl.BlockSpec((1, Q_TILE, ATT_WIDTH), lambda b, s: (b, s, 0)),
        out_shape=jax.ShapeDtypeStruct((BATCH, SEQ, ATT_WIDTH), BF16),
        scratch_shapes=[pltpu.VMEM((ATT_HEADS, WINDOW + LEFT_ROWS, LANES), F32)],
        compiler_params=pltpu.CompilerParams(
            dimension_semantics=("arbitrary", "arbitrary"), vmem_limit_bytes=40 * MIB),
        name="band_attn",
    )(qt, k_pad, vt_pad, rel_rows)


def _cumsum_lanes(x):
    lane = lax.broadcasted_iota(jnp.int32, x.shape, 1)
    d = 1
    while d < x.shape[1]:
        x = x + jnp.where(lane >= d, pltpu.roll(x, d, axis=1), 0.0)
        d *= 2
    return x


def _split3(x):
    hi = x.astype(BF16).astype(F32)
    r1 = x - hi
    mid = r1.astype(BF16).astype(F32)
    lo = (r1 - mid).astype(BF16).astype(F32)
    return hi, mid, lo


def _mlstm_kernel(mqk_ref, vt_ref, mot_ref, gt_ref, cw_ref, cb_ref, gb_ref, ng_ref, o_ref,
                  c_scr, n_scr, m_scr, x_scr):
    step = pl.program_id(0)
    hd = MLSTM_HEAD_DIM
    rows_all = BATCH * GATE_ROWS

    @pl.when(step == 0)
    def _():
        c_scr[...] = jnp.zeros_like(c_scr)
        n_scr[...] = jnp.zeros_like(n_scr)
        m_scr[...] = jnp.zeros_like(m_scr)
        x_scr[:, :CONV_TAIL, :] = jnp.zeros((BATCH, CONV_TAIL, 2 * MLSTM_WIDTH), BF16)

    def bcast(col):
        return jnp.broadcast_to(col, (rows_all, ML_CHUNK))

    gates = jnp.concatenate([gt_ref[b] for b in range(BATCH)], axis=0) + gb_ref[...]
    logf = jnp.minimum(gates, 0.0) - jnp.log1p(jnp.exp(-jnp.abs(gates)))
    bcum = pltpu.roll(_cumsum_lanes(logf), rows_all - MLSTM_HEADS, axis=0)
    b_last = bcast(bcum[:, ML_CHUNK - 1:ML_CHUNK])
    a_row = b_last - bcum + gates
    m_loc = bcast(jnp.max(a_row, axis=1, keepdims=True))
    w_row = jnp.exp(a_row - m_loc)
    m_prev = m_scr[...]
    m_new = jnp.maximum(b_last + m_prev, m_loc)
    s_prev = jnp.exp(b_last + m_prev - m_new)
    s_loc = jnp.exp(m_loc - m_new)
    m_scr[...] = m_new
    g_row = bcum + m_prev
    n_prev = n_scr[...]
    w_bf = w_row.astype(BF16)
    n_prev_bf = n_prev.astype(BF16)
    r_parts = _split3(gates - bcum)
    b_parts = _split3(bcum)
    split_row = lax.broadcasted_iota(jnp.int32, (SPLIT_ROWS, ML_CHUNK), 0)

    def outer_sum_operand(parts, one, first):
        other = 3 - first
        out = jnp.where((split_row >= other) & (split_row < other + 3), 1.0, 0.0)
        for i, part in enumerate(parts):
            out = jnp.where(split_row == first + i, part[one], out)
        return out.astype(BF16)

    src = lax.broadcasted_iota(jnp.int32, (ML_CHUNK, ML_CHUNK), 0)
    qry = lax.broadcasted_iota(jnp.int32, (ML_CHUNK, ML_CHUNK), 1)
    causal = src <= qry
    out_row = lax.broadcasted_iota(jnp.int32, (ML_CHUNK, CONV_TAIL + ML_CHUNK), 0)
    in_row = lax.broadcasted_iota(jnp.int32, (ML_CHUNK, CONV_TAIL + ML_CHUNK), 1)
    shifts = [(in_row == out_row + (CONV_TAIL - (CONV_WIDTH - 1) + j)).astype(BF16)
              for j in range(CONV_WIDTH - 1)]
    k_scale = 1.0 / math.sqrt(hd)
    new_n = []

    for b in range(BATCH):
        cur = mqk_ref[b]
        x_scr[b, CONV_TAIL:, :] = cur
        window = x_scr[b]
        acc = cb_ref[...] + cur.astype(F32) * cw_ref[CONV_WIDTH - 1:CONV_WIDTH, :]
        for j in range(CONV_WIDTH - 1):
            acc = acc + _dot(shifts[j], window) * cw_ref[j:j + 1, :]
        x_scr[b, :CONV_TAIL, :] = cur[ML_CHUNK - CONV_TAIL:, :]
        act = acc * jax.nn.sigmoid(acc)
        q_all = act[:, :MLSTM_WIDTH].astype(BF16)
        k_all = (act[:, MLSTM_WIDTH:] * k_scale).astype(BF16)

        grp = slice(b * GATE_ROWS, (b + 1) * GATE_ROWS)
        outs = []
        for h in range(MLSTM_HEADS):
            u = b * MLSTM_HEADS + h
            r = b * GATE_ROWS + h
            one = slice(r, r + 1)
            sl = slice(h * hd, (h + 1) * hd)
            qh = q_all[:, sl]
            kh = k_all[:, sl]
            vt = vt_ref[b, sl, :]

            c_prev = c_scr[u]
            c_loc = _dot((vt.astype(F32) * w_row[one]).astype(BF16), kh)
            n_loc = _dot(w_bf[grp], kh)[h:h + 1]
            c_scr[u] = s_prev[one] * c_prev + s_loc[one] * c_loc
            new_n.append(s_prev[one] * n_prev[one] + s_loc[one] * n_loc)
            if h == MLSTM_HEADS - 1:
                new_n.append(jnp.zeros((GATE_ROWS - MLSTM_HEADS, hd), F32))

            lhs = outer_sum_operand(r_parts, one, 0)
            rhs = outer_sum_operand(b_parts, one, 3)
            dmat = jnp.where(causal, _dot_tn(lhs, rhs), -jnp.inf)
            m_t = jnp.maximum(g_row[one], jnp.max(dmat, axis=0, keepdims=True))
            pt = _dot_nt(kh, qh) * jnp.exp(dmat - m_t)
            inter = jnp.exp(g_row[one] - m_t)
            nq = _dot_nt(n_prev_bf[grp], qh)[h:h + 1]
            den = inter * nq + jnp.sum(pt, axis=0, keepdims=True)
            num = inter * _dot_nt(c_prev.astype(BF16), qh) + _dot(vt, pt.astype(BF16))
            hout = num * (1.0 / jnp.maximum(jnp.abs(den), jnp.exp(-m_t)))
            ms = jnp.mean(hout * hout, axis=0, keepdims=True)
            y = hout * lax.rsqrt(ms + NORM_EPS) * ng_ref[sl, :]
            outs.append(jax.nn.sigmoid(mot_ref[b, sl, :].astype(F32)) * y)
        o_ref[b] = jnp.transpose(jnp.concatenate(outs, axis=0)).astype(BF16)

    n_scr[...] = jnp.concatenate(new_n, axis=0)


def _mlstm(mqk, mvt, mot, gt, conv_w, conv_b, gate_bias, norm_g):
    tok = lambda width: pl.BlockSpec((BATCH, ML_CHUNK, width), lambda c: (0, c, 0))
    feat = lambda height: pl.BlockSpec((BATCH, height, ML_CHUNK), lambda c: (0, 0, c))
    rows_all = BATCH * GATE_ROWS
    return pl.pallas_call(
        _mlstm_kernel,
        grid=(ML_STEPS,),
        in_specs=[
            tok(2 * MLSTM_WIDTH), feat(MLSTM_WIDTH), feat(MLSTM_WIDTH), feat(GATE_ROWS),
            _resident((CONV_WIDTH, 2 * MLSTM_WIDTH)),
            _resident((1, 2 * MLSTM_WIDTH)),
            _resident((rows_all, ML_CHUNK)),
            _resident((MLSTM_WIDTH, ML_CHUNK)),
        ],
        out_specs=tok(MLSTM_WIDTH),
        out_shape=jax.ShapeDtypeStruct((BATCH, SEQ, MLSTM_WIDTH), BF16),
        scratch_shapes=[
            pltpu.VMEM((ML_UNITS, MLSTM_HEAD_DIM, MLSTM_HEAD_DIM), F32),
            pltpu.VMEM((rows_all, MLSTM_HEAD_DIM), F32),
            pltpu.VMEM((rows_all, ML_CHUNK), F32),
            pltpu.VMEM((BATCH, CONV_TAIL + ML_CHUNK, 2 * MLSTM_WIDTH), BF16),
        ],
        compiler_params=pltpu.CompilerParams(
            dimension_semantics=("arbitrary",), vmem_limit_bytes=32 * MIB),
        name="mlstm",
    )(mqk, mvt, mot, gt, conv_w, conv_b, gate_bias, norm_g)


def _merge_ffn_kernel(x_ref, att_ref, ml_ref, gate_ref, wa_ref, wm_ref, wo_ref,
                      g_ref, wu_ref, wd_ref, fg_ref, o_ref, *, final_norm):
    ga = gate_ref[:, :D_MODEL].astype(F32)
    gm = gate_ref[:, D_MODEL:].astype(F32)
    y = (jax.nn.sigmoid(ga) * _dot(att_ref[...], wa_ref[...])
         + jax.nn.sigmoid(gm) * _dot(ml_ref[...], wm_ref[...]))
    x = x_ref[...] + _dot(y.astype(BF16), wo_ref[...])
    hn = _rms_norm(x, g_ref[...]).astype(BF16)
    acc = x
    for j in range(D_FF // FF_TILE):
        cols = slice(j * FF_TILE, (j + 1) * FF_TILE)
        h = jnp.maximum(_dot(hn, wu_ref[:, cols]), 0.0)
        acc = acc + _dot((h * h).astype(BF16), wd_ref[cols, :])
    if final_norm:
        acc = _rms_norm(acc, fg_ref[...])
    o_ref[...] = acc


def _merge_ffn(x2d, att, ml, gates, wa, wm, wo, g, wu, wd, final_g, final_norm):
    tm = TOKEN_TILE
    row = lambda width: pl.BlockSpec((tm, width), lambda i: (i, 0))
    return pl.pallas_call(
        functools.partial(_merge_ffn_kernel, final_norm=final_norm),
        grid=(TOKENS // tm,),
        in_specs=[row(D_MODEL), row(ATT_WIDTH), row(MLSTM_WIDTH), row(2 * D_MODEL),
                  _resident((ATT_WIDTH, D_MODEL)), _resident((MLSTM_WIDTH, D_MODEL)),
                  _resident((D_MODEL, D_MODEL)), _resident((1, D_MODEL)),
                  _resident((D_MODEL, D_FF)), _resident((D_FF, D_MODEL)),
                  _resident((1, D_MODEL))],
        out_specs=row(D_MODEL),
        out_shape=jax.ShapeDtypeStruct((TOKENS, D_MODEL), F32),
        compiler_params=pltpu.CompilerParams(
            dimension_semantics=("arbitrary",), vmem_limit_bytes=56 * MIB),
        name="merge_ffn",
    )(x2d, att, ml, gates, wa, wm, wo, g, wu, wd, final_g)


def _rel_bias_rows(rel_bias):
    far = jnp.broadcast_to(rel_bias[:, 2 * MAX_REL_DIST:], (ATT_HEADS, MAX_REL_DIST))
    near = rel_bias[:, MAX_REL_DIST + 1 - Q_TILE:][:, ::-1]
    rest = jnp.broadcast_to(rel_bias[:, 2 * MAX_REL_DIST:],
                            (ATT_HEADS, REL_ROW - MAX_REL_DIST - near.shape[1]))
    return jnp.concatenate([far, near, rest], axis=1).astype(F32) * LOG2E


def kernel(x, mix_norm_g, w_in, conv_w, conv_b, b_igate, b_fgate, rel_bias, mh_norm_g,
           w_att_proj, w_mlstm_proj, w_out, ffn_norm_g, w_up, w_down, final_norm_g):
    h = x.reshape(TOKENS, D_MODEL)
    a = ATT_WIDTH
    m = MLSTM_WIDTH
    gate_lo = 3 * a + 4 * m
    gate_hi = gate_lo + GATE_ROWS
    final_g = final_norm_g.reshape(1, D_MODEL)
    for l in range(DEPTH):
        w = w_in[l]
        w_rows = jnp.concatenate([w[:, a:2 * a], w[:, 3 * a:3 * a + 2 * m], w[:, gate_hi:]],
                                 axis=1).astype(BF16)
        w_cols = jnp.concatenate([w[:, :a], w[:, 2 * a:3 * a], w[:, 3 * a + 2 * m:gate_lo]],
                                 axis=1).T.astype(BF16)
        w_gate = w[:, gate_lo:gate_hi].T.astype(BF16)
        gate_bias = jnp.tile(
            jnp.broadcast_to(jnp.concatenate([b_igate[l], b_fgate[l]])[:, None],
                             (GATE_ROWS, ML_CHUNK)), (BATCH, 1))
        norm_g = jnp.broadcast_to(mh_norm_g[l][:, None], (MLSTM_WIDTH, ML_CHUNK))

        qt, k_pad, vt_pad, mqk, mvt, mot, gates, gt = _in_proj(
            h, mix_norm_g[l].reshape(1, D_MODEL), w_rows, w_cols, w_gate)
        att = _band_attn(qt, k_pad, vt_pad, _rel_bias_rows(rel_bias[l]))
        ml = _mlstm(mqk.reshape(BATCH, SEQ, 2 * MLSTM_WIDTH), mvt, mot, gt,
                    conv_w[l], conv_b[l].reshape(1, 2 * MLSTM_WIDTH), gate_bias, norm_g)
        h = _merge_ffn(h, att.reshape(TOKENS, ATT_WIDTH), ml.reshape(TOKENS, MLSTM_WIDTH), gates,
                       w_att_proj[l].astype(BF16), w_mlstm_proj[l].astype(BF16),
                       w_out[l].astype(BF16), ffn_norm_g[l].reshape(1, D_MODEL),
                       w_up[l].astype(BF16), w_down[l].astype(BF16), final_g,
                       final_norm=(l == DEPTH - 1))
    return h.reshape(BATCH, SEQ, D_MODEL)
```

```python
import functools
import math

import jax
import jax.numpy as jnp
from jax import lax
from jax.experimental import pallas as pl
from jax.experimental.pallas import tpu as pltpu

D_MODEL = 1024
BATCH = 4
SEQ = 4096
DEPTH = 2
CHUNK = 64
NORM_EPS = 1e-6
ATT_HEADS = 8
ATT_HEAD_DIM = 64
ATT_WIDTH = ATT_HEADS * ATT_HEAD_DIM
LEFT_CHUNKS = 8
BAND_CHUNKS = LEFT_CHUNKS + 1
BAND = BAND_CHUNKS * CHUNK
MAX_REL_DIST = 256
MLSTM_HEADS = 4
MLSTM_HEAD_DIM = 128
MLSTM_WIDTH = MLSTM_HEADS * MLSTM_HEAD_DIM
CONV_WIDTH = 4
D_FF = 4 * D_MODEL

LANES = 128
SUBLANES = 8

TOKENS = BATCH * SEQ
NUM_CHUNKS = SEQ // CHUNK
LEFT_ROWS = LEFT_CHUNKS * CHUNK
ROW_COLS = ATT_WIDTH + 2 * MLSTM_WIDTH + 2 * D_MODEL
COL_ROWS = 2 * ATT_WIDTH + 2 * MLSTM_WIDTH
GATE_ROWS = 2 * MLSTM_HEADS

Q_TILE = 2 * CHUNK
WINDOW = BAND + CHUNK
WINDOW_TILES = WINDOW // LANES
ATT_STEPS = SEQ // Q_TILE
REL_ROW = 1024
LOG2E = 1.4426950408889634
MASKED = -1e30

ML_CHUNK = LANES
ML_STEPS = SEQ // ML_CHUNK
ML_UNITS = BATCH * MLSTM_HEADS
CONV_TAIL = 16
CONV_ROWS = 16
CONV_COLS = 128
SPLIT_ROWS = 16

TOKEN_TILE = 512
FF_TILE = 1024

F32 = jnp.float32
BF16 = jnp.bfloat16
MIB = 1024 * 1024


def _resident(shape):
    nd = len(shape)
    return pl.BlockSpec(shape, lambda *_: (0,) * nd, pipeline_mode=pl.Buffered(1))


def _rms_norm(x, g):
    ms = jnp.mean(x * x, axis=-1, keepdims=True)
    return x * lax.rsqrt(ms + NORM_EPS) * g


def _dot(a, b):
    return jnp.dot(a, b, preferred_element_type=F32)


def _dot_nt(a, b):
    return lax.dot_general(a, b, (((1,), (1,)), ((), ())), preferred_element_type=F32)


def _dot_tn(a, b):
    return lax.dot_general(a, b, (((0,), (0,)), ((), ())), preferred_element_type=F32)


def _in_proj_kernel(x_ref, g_ref, w_ref, wt_ref, wg_ref, cw_ref, cb_ref,
                    qt_ref, k_ref, vt_ref, mqk_ref, mvt_ref, mot_ref, gate_ref, gt_ref,
                    conv_scr):
    a = ATT_WIDTH
    m = MLSTM_WIDTH

    @pl.when(pl.program_id(0) % (SEQ // TOKEN_TILE) == 0)
    def _():
        conv_scr[:CONV_TAIL, :] = jnp.zeros((CONV_TAIL, 2 * m), F32)

    xn = _rms_norm(x_ref[...], g_ref[...]).astype(BF16)

    def seg(lo, width):
        return _dot(xn, w_ref[:, lo:lo + width])

    def seg_t(lo, width):
        return _dot_nt(wt_ref[lo:lo + width, :], xn)

    conv_scr[CONV_TAIL:, :m] = seg(a, m)
    conv_scr[CONV_TAIL:, m:] = seg(a + m, m)
    for c0 in range(0, 2 * m, CONV_COLS):
        cols = slice(c0, c0 + CONV_COLS)
        post_scale = 1.0 / math.sqrt(MLSTM_HEAD_DIM) if c0 >= m else None
        for r0 in range(0, TOKEN_TILE, CONV_ROWS):
            lo = CONV_TAIL - SUBLANES + r0
            ext = conv_scr[lo:lo + SUBLANES + CONV_ROWS, cols]
            acc = cb_ref[:, cols] + ext[SUBLANES:] * cw_ref[CONV_WIDTH - 1:CONV_WIDTH, cols]
            for d in range(1, CONV_WIDTH):
                tap = CONV_WIDTH - 1 - d
                acc = acc + pltpu.roll(ext, d, axis=0)[SUBLANES:] * cw_ref[tap:tap + 1, cols]
            act = acc * jax.nn.sigmoid(acc)
            if post_scale is not None:
                act = act * post_scale
            mqk_ref[r0:r0 + CONV_ROWS, cols] = act.astype(BF16)
    conv_scr[:CONV_TAIL, :] = conv_scr[TOKEN_TILE:, :]

    qt_ref[...] = (seg_t(0, a) * (LOG2E / math.sqrt(ATT_HEAD_DIM))).astype(BF16)
    vt = seg_t(a, a).astype(BF16)
    for j in range(TOKEN_TILE // LANES):
        vt_ref[0, j] = vt[:, j * LANES:(j + 1) * LANES]
    mvt_ref[0] = seg_t(2 * a, m).astype(BF16)
    mot_ref[0] = seg_t(2 * a + m, m).astype(BF16)
    gt_ref[0] = _dot_nt(wg_ref[...], xn)
    k_ref[0] = seg(0, a).astype(BF16)
    base = a + 2 * m
    for j in range(2 * D_MODEL // 512):
        gate_ref[:, j * 512:(j + 1) * 512] = seg(base + j * 512, 512).astype(BF16)


def _in_proj(x2d, g, w_rows, w_cols, w_gate, conv_w, conv_b):
    tm = TOKEN_TILE
    tiles_per_batch = SEQ // tm
    lane_tiles = tm // LANES
    row = lambda width: pl.BlockSpec((tm, width), lambda i: (i, 0))
    col = lambda height: pl.BlockSpec((1, height, tm),
                                      lambda i: (i // tiles_per_batch, 0, i % tiles_per_batch))
    k_spec = pl.BlockSpec((1, tm, ATT_WIDTH),
                          lambda i: (i // tiles_per_batch, i % tiles_per_batch, 0))
    vt_spec = pl.BlockSpec((1, lane_tiles, ATT_WIDTH, LANES),
                           lambda i: (i // tiles_per_batch, i % tiles_per_batch, 0, 0))
    out_shape = (
        jax.ShapeDtypeStruct((ATT_WIDTH, TOKENS), BF16),
        jax.ShapeDtypeStruct((BATCH, SEQ, ATT_WIDTH), BF16),
        jax.ShapeDtypeStruct((BATCH, SEQ // LANES, ATT_WIDTH, LANES), BF16),
        jax.ShapeDtypeStruct((TOKENS, 2 * MLSTM_WIDTH), BF16),
        jax.ShapeDtypeStruct((BATCH, MLSTM_WIDTH, SEQ), BF16),
        jax.ShapeDtypeStruct((BATCH, MLSTM_WIDTH, SEQ), BF16),
        jax.ShapeDtypeStruct((TOKENS, 2 * D_MODEL), BF16),
        jax.ShapeDtypeStruct((BATCH, GATE_ROWS, SEQ), F32),
    )
    return pl.pallas_call(
        _in_proj_kernel,
        grid=(TOKENS // tm,),
        in_specs=[
            row(D_MODEL),
            _resident((1, D_MODEL)),
            _resident((D_MODEL, ROW_COLS)),
            _resident((COL_ROWS, D_MODEL)),
            _resident((GATE_ROWS, D_MODEL)),
            _resident((CONV_WIDTH, 2 * MLSTM_WIDTH)),
            _resident((1, 2 * MLSTM_WIDTH)),
        ],
        out_specs=(pl.BlockSpec((ATT_WIDTH, tm), lambda i: (0, i)), k_spec, vt_spec,
                   row(2 * MLSTM_WIDTH), col(MLSTM_WIDTH), col(MLSTM_WIDTH), row(2 * D_MODEL),
                   col(GATE_ROWS)),
        out_shape=out_shape,
        scratch_shapes=[pltpu.VMEM((CONV_TAIL + tm, 2 * MLSTM_WIDTH), F32)],
        compiler_params=pltpu.CompilerParams(
            dimension_semantics=("arbitrary",), vmem_limit_bytes=48 * MIB),
        name="in_proj",
    )(x2d, g, w_rows, w_cols, w_gate, conv_w, conv_b)


def _build_bias_table(rel_ref, bias_scr):
    kb = lax.broadcasted_iota(jnp.int32, (WINDOW, LANES), 0)
    lane = lax.broadcasted_iota(jnp.int32, (WINDOW, LANES), 1)
    band_lo = jnp.where(lane >= CHUNK, CHUNK, 0)
    in_band = (kb >= band_lo) & (kb < band_lo + BAND)
    for h in range(ATT_HEADS):
        rows = jnp.broadcast_to(rel_ref[h:h + 1, :], (LANES, REL_ROW))
        shifted = pltpu.roll(rows, 0, axis=1, stride=1, stride_axis=0)
        table = jnp.concatenate(
            [jnp.transpose(shifted[:, j * LANES:(j + 1) * LANES]) for j in range(WINDOW_TILES)],
            axis=0)
        bias_scr[h, :WINDOW, :] = jnp.where(in_band, table, MASKED)
        bias_scr[h, WINDOW:, :] = jnp.full((LEFT_ROWS, LANES), MASKED, F32)


def _band_attn_kernel(qt_ref, k_ref, vt_ref, rel_ref, o_ref, bias_scr):
    step = pl.program_id(1)

    @pl.when((pl.program_id(0) == 0) & (step == 0))
    def _():
        _build_bias_table(rel_ref, bias_scr)

    first_tile = jnp.maximum(step - LEFT_ROWS // Q_TILE, 0)
    start = pl.multiple_of(first_tile * Q_TILE, Q_TILE)
    shift = pl.multiple_of(LEFT_ROWS - (step - first_tile) * Q_TILE, Q_TILE)
    zeros_half = jnp.zeros((ATT_HEAD_DIM, LANES), BF16)
    outs = []
    for pair in range(ATT_HEADS // 2):
        lanes = slice(pair * LANES, (pair + 1) * LANES)
        kwin = k_ref[0, pl.ds(start, WINDOW), lanes]
        for hh in range(2):
            h = 2 * pair + hh
            rows = slice(h * ATT_HEAD_DIM, (h + 1) * ATT_HEAD_DIM)
            qh = qt_ref[rows, :]
            qm = jnp.concatenate([qh, zeros_half] if hh == 0 else [zeros_half, qh], axis=0)
            st = _dot(kwin, qm)
            blocks = [st[j * CHUNK:(j + 1) * CHUNK]
                      + bias_scr[h, pl.ds(shift + j * CHUNK, CHUNK), :]
                      for j in range(WINDOW // CHUNK)]
            m = jnp.max(functools.reduce(jnp.maximum, blocks), axis=0, keepdims=True)
            probs = [jnp.exp2(blk - m) for blk in blocks]
            denom = jnp.sum(functools.reduce(jnp.add, probs), axis=0, keepdims=True)
            pt = jnp.concatenate(probs, axis=0).astype(BF16)
            acc = None
            for j in range(WINDOW_TILES):
                part = _dot(vt_ref[0, first_tile + j, rows, :], pt[j * LANES:(j + 1) * LANES])
                acc = part if acc is None else acc + part
            outs.append(acc / denom)
    o_ref[0] = jnp.transpose(jnp.concatenate(outs, axis=0)).astype(BF16)


def _band_attn(qt, k_pad, vt_pad, rel_rows):
    return pl.pallas_call(
        _band_attn_kernel,
        grid=(BATCH, ATT_STEPS),
        in_specs=[
            pl.BlockSpec((ATT_WIDTH, Q_TILE), lambda b, s: (0, b * ATT_STEPS + s)),
            pl.BlockSpec((1, SEQ, ATT_WIDTH), lambda b, s: (b, 0, 0)),
            pl.BlockSpec((1, SEQ // LANES, ATT_WIDTH, LANES), lambda b, s: (b, 0, 0, 0)),
            _resident((ATT_HEADS, REL_ROW)),
        ],
        out_specs=pl.BlockSpec((1, Q_TILE, ATT_WIDTH), lambda b, s: (b, s, 0)),
        out_shape=jax.ShapeDtypeStruct((BATCH, SEQ, ATT_WIDTH), BF16),
        scratch_shapes=[pltpu.VMEM((ATT_HEADS, WINDOW + LEFT_ROWS, LANES), F32)],
        compiler_params=pltpu.CompilerParams(
            dimension_semantics=("arbitrary", "arbitrary"), vmem_limit_bytes=40 * MIB),
        name="band_attn",
    )(qt, k_pad, vt_pad, rel_rows)


def _cumsum_lanes(x):
    lane = lax.broadcasted_iota(jnp.int32, x.shape, 1)
    d = 1
    while d < x.shape[1]:
        x = x + jnp.where(lane >= d, pltpu.roll(x, d, axis=1), 0.0)
        d *= 2
    return x


def _split3(x):
    hi = x.astype(BF16).astype(F32)
    r1 = x - hi
    mid = r1.astype(BF16).astype(F32)
    lo = (r1 - mid).astype(BF16).astype(F32)
    return hi, mid, lo


def _mlstm_kernel(mqk_ref, vt_ref, mot_ref, gt_ref, gb_ref, ng_ref, o_ref,
                  c_scr, n_scr, m_scr):
    step = pl.program_id(0)
    hd = MLSTM_HEAD_DIM
    rows_all = BATCH * GATE_ROWS

    @pl.when(step == 0)
    def _():
        c_scr[...] = jnp.zeros_like(c_scr)
        n_scr[...] = jnp.zeros_like(n_scr)
        m_scr[...] = jnp.zeros_like(m_scr)

    def bcast(col):
        return jnp.broadcast_to(col, (rows_all, ML_CHUNK))

    gates = jnp.concatenate([gt_ref[b] for b in range(BATCH)], axis=0) + gb_ref[...]
    logf = jnp.minimum(gates, 0.0) - jnp.log1p(jnp.exp(-jnp.abs(gates)))
    bcum = pltpu.roll(_cumsum_lanes(logf), rows_all - MLSTM_HEADS, axis=0)
    b_last = bcast(bcum[:, ML_CHUNK - 1:ML_CHUNK])
    a_row = b_last - bcum + gates
    m_loc = bcast(jnp.max(a_row, axis=1, keepdims=True))
    w_row = jnp.exp(a_row - m_loc)
    m_prev = m_scr[...]
    m_new = jnp.maximum(b_last + m_prev, m_loc)
    s_prev = jnp.exp(b_last + m_prev - m_new)
    s_loc = jnp.exp(m_loc - m_new)
    m_scr[...] = m_new
    g_row = bcum + m_prev
    n_prev = n_scr[...]
    w_bf = w_row.astype(BF16)
    n_prev_bf = n_prev.astype(BF16)
    r_parts = _split3(gates - bcum)
    b_parts = _split3(bcum)
    split_row = lax.broadcasted_iota(jnp.int32, (SPLIT_ROWS, ML_CHUNK), 0)

    def outer_sum_operand(parts, one, first):
        other = 3 - first
        out = jnp.where((split_row >= other) & (split_row < other + 3), 1.0, 0.0)
        for i, part in enumerate(parts):
            out = jnp.where(split_row == first + i, part[one], out)
        return out.astype(BF16)

    src = lax.broadcasted_iota(jnp.int32, (ML_CHUNK, ML_CHUNK), 0)
    qry = lax.broadcasted_iota(jnp.int32, (ML_CHUNK, ML_CHUNK), 1)
    causal = src <= qry
    new_n = []

    for b in range(BATCH):
        grp = slice(b * GATE_ROWS, (b + 1) * GATE_ROWS)
        outs = []
        for h in range(MLSTM_HEADS):
            u = b * MLSTM_HEADS + h
            r = b * GATE_ROWS + h
            one = slice(r, r + 1)
            sl = slice(h * hd, (h + 1) * hd)
            qh = mqk_ref[b, :, sl]
            kh = mqk_ref[b, :, MLSTM_WIDTH + h * hd:MLSTM_WIDTH + (h + 1) * hd]
            vt = vt_ref[b, sl, :]

            c_prev = c_scr[u]
            c_loc = _dot((vt.astype(F32) * w_row[one]).astype(BF16), kh)
            n_loc = _dot(w_bf[grp], kh)[h:h + 1]
            c_scr[u] = s_prev[one] * c_prev + s_loc[one] * c_loc
            new_n.append(s_prev[one] * n_prev[one] + s_loc[one] * n_loc)
            if h == MLSTM_HEADS - 1:
                new_n.append(jnp.zeros((GATE_ROWS - MLSTM_HEADS, hd), F32))

            lhs = outer_sum_operand(r_parts, one, 0)
            rhs = outer_sum_operand(b_parts, one, 3)
            dmat = jnp.where(causal, _dot_tn(lhs, rhs), -jnp.inf)
            m_t = jnp.maximum(g_row[one], jnp.max(dmat, axis=0, keepdims=True))
            pt = _dot_nt(kh, qh) * jnp.exp(dmat - m_t)
            inter = jnp.exp(g_row[one] - m_t)
            nq = _dot_nt(n_prev_bf[grp], qh)[h:h + 1]
            den = inter * nq + jnp.sum(pt, axis=0, keepdims=True)
            num = inter * _dot_nt(c_prev.astype(BF16), qh) + _dot(vt, pt.astype(BF16))
            hout = num * (1.0 / jnp.maximum(jnp.abs(den), jnp.exp(-m_t)))
            ms = jnp.mean(hout * hout, axis=0, keepdims=True)
            y = hout * lax.rsqrt(ms + NORM_EPS) * ng_ref[sl, :]
            outs.append(jax.nn.sigmoid(mot_ref[b, sl, :].astype(F32)) * y)
        o_ref[b] = jnp.transpose(jnp.concatenate(outs, axis=0)).astype(BF16)

    n_scr[...] = jnp.concatenate(new_n, axis=0)


def _mlstm(mqk, mvt, mot, gt, gate_bias, norm_g):
    tok = lambda width: pl.BlockSpec((BATCH, ML_CHUNK, width), lambda c: (0, c, 0))
    feat = lambda height: pl.BlockSpec((BATCH, height, ML_CHUNK), lambda c: (0, 0, c))
    rows_all = BATCH * GATE_ROWS
    return pl.pallas_call(
        _mlstm_kernel,
        grid=(ML_STEPS,),
        in_specs=[
            tok(2 * MLSTM_WIDTH), feat(MLSTM_WIDTH), feat(MLSTM_WIDTH), feat(GATE_ROWS),
            _resident((rows_all, ML_CHUNK)),
            _resident((MLSTM_WIDTH, ML_CHUNK)),
        ],
        out_specs=tok(MLSTM_WIDTH),
        out_shape=jax.ShapeDtypeStruct((BATCH, SEQ, MLSTM_WIDTH), BF16),
        scratch_shapes=[
            pltpu.VMEM((ML_UNITS, MLSTM_HEAD_DIM, MLSTM_HEAD_DIM), F32),
            pltpu.VMEM((rows_all, MLSTM_HEAD_DIM), F32),
            pltpu.VMEM((rows_all, ML_CHUNK), F32),
        ],
        compiler_params=pltpu.CompilerParams(
            dimension_semantics=("arbitrary",), vmem_limit_bytes=32 * MIB),
        name="mlstm",
    )(mqk, mvt, mot, gt, gate_bias, norm_g)


def _merge_ffn_kernel(x_ref, att_ref, ml_ref, gate_ref, wa_ref, wm_ref, wo_ref,
                      g_ref, wu_ref, wd_ref, fg_ref, o_ref, *, final_norm):
    ga = gate_ref[:, :D_MODEL].astype(F32)
    gm = gate_ref[:, D_MODEL:].astype(F32)
    y = (jax.nn.sigmoid(ga) * _dot(att_ref[...], wa_ref[...])
         + jax.nn.sigmoid(gm) * _dot(ml_ref[...], wm_ref[...]))
    x = x_ref[...] + _dot(y.astype(BF16), wo_ref[...])
    hn = _rms_norm(x, g_ref[...]).astype(BF16)
    acc = x
    for j in range(D_FF // FF_TILE):
        cols = slice(j * FF_TILE, (j + 1) * FF_TILE)
        h = jnp.maximum(_dot(hn, wu_ref[:, cols]), 0.0)
        acc = acc + _dot((h * h).astype(BF16), wd_ref[cols, :])
    if final_norm:
        acc = _rms_norm(acc, fg_ref[...])
    o_ref[...] = acc


def _merge_ffn(x2d, att, ml, gates, wa, wm, wo, g, wu, wd, final_g, final_norm):
    tm = TOKEN_TILE
    row = lambda width: pl.BlockSpec((tm, width), lambda i: (i, 0))
    return pl.pallas_call(
        functools.partial(_merge_ffn_kernel, final_norm=final_norm),
        grid=(TOKENS // tm,),
        in_specs=[row(D_MODEL), row(ATT_WIDTH), row(MLSTM_WIDTH), row(2 * D_MODEL),
                  _resident((ATT_WIDTH, D_MODEL)), _resident((MLSTM_WIDTH, D_MODEL)),
                  _resident((D_MODEL, D_MODEL)), _resident((1, D_MODEL)),
                  _resident((D_MODEL, D_FF)), _resident((D_FF, D_MODEL)),
                  _resident((1, D_MODEL))],
        out_specs=row(D_MODEL),
        out_shape=jax.ShapeDtypeStruct((TOKENS, D_MODEL), F32),
        compiler_params=pltpu.CompilerParams(
            dimension_semantics=("arbitrary",), vmem_limit_bytes=56 * MIB),
        name="merge_ffn",
    )(x2d, att, ml, gates, wa, wm, wo, g, wu, wd, final_g)


def _rel_bias_rows(rel_bias):
    far = jnp.broadcast_to(rel_bias[:, 2 * MAX_REL_DIST:], (ATT_HEADS, MAX_REL_DIST))
    near = rel_bias[:, MAX_REL_DIST + 1 - Q_TILE:][:, ::-1]
    rest = jnp.broadcast_to(rel_bias[:, 2 * MAX_REL_DIST:],
                            (ATT_HEADS, REL_ROW - MAX_REL_DIST - near.shape[1]))
    return jnp.concatenate([far, near, rest], axis=1).astype(F32) * LOG2E


def kernel(x, mix_norm_g, w_in, conv_w, conv_b, b_igate, b_fgate, rel_bias, mh_norm_g,
           w_att_proj, w_mlstm_proj, w_out, ffn_norm_g, w_up, w_down, final_norm_g):
    h = x.reshape(TOKENS, D_MODEL)
    a = ATT_WIDTH
    m = MLSTM_WIDTH
    gate_lo = 3 * a + 4 * m
    gate_hi = gate_lo + GATE_ROWS
    final_g = final_norm_g.reshape(1, D_MODEL)
    for l in range(DEPTH):
        w = w_in[l]
        w_rows = jnp.concatenate([w[:, a:2 * a], w[:, 3 * a:3 * a + 2 * m], w[:, gate_hi:]],
                                 axis=1).astype(BF16)
        w_cols = jnp.concatenate([w[:, :a], w[:, 2 * a:3 * a], w[:, 3 * a + 2 * m:gate_lo]],
                                 axis=1).T.astype(BF16)
        w_gate = w[:, gate_lo:gate_hi].T.astype(BF16)
        gate_bias = jnp.tile(
            jnp.broadcast_to(jnp.concatenate([b_igate[l], b_fgate[l]])[:, None],
                             (GATE_ROWS, ML_CHUNK)), (BATCH, 1))
        norm_g = jnp.broadcast_to(mh_norm_g[l][:, None], (MLSTM_WIDTH, ML_CHUNK))

        qt, k_tok, vt_tiles, mqk, mvt, mot, gates, gt = _in_proj(
            h, mix_norm_g[l].reshape(1, D_MODEL), w_rows, w_cols, w_gate,
            conv_w[l], conv_b[l].reshape(1, 2 * MLSTM_WIDTH))
        att = _band_attn(qt, k_tok, vt_tiles, _rel_bias_rows(rel_bias[l]))
        ml = _mlstm(mqk.reshape(BATCH, SEQ, 2 * MLSTM_WIDTH), mvt, mot, gt, gate_bias, norm_g)
        h = _merge_ffn(h, att.reshape(TOKENS, ATT_WIDTH), ml.reshape(TOKENS, MLSTM_WIDTH), gates,
                       w_att_proj[l].astype(BF16), w_mlstm_proj[l].astype(BF16),
                       w_out[l].astype(BF16), ffn_norm_g[l].reshape(1, D_MODEL),
                       w_up[l].astype(BF16), w_down[l].astype(BF16), final_g,
                       final_norm=(l == DEPTH - 1))
    return h.reshape(BATCH, SEQ, D_MODEL)
```

```python
import functools
import math

import jax
import jax.numpy as jnp
from jax import lax
from jax.experimental import pallas as pl
from jax.experimental.pallas import tpu as pltpu

D_MODEL = 1024
BATCH = 4
SEQ = 4096
DEPTH = 2
CHUNK = 64
NORM_EPS = 1e-6
ATT_HEADS = 8
ATT_HEAD_DIM = 64
ATT_WIDTH = ATT_HEADS * ATT_HEAD_DIM
LEFT_CHUNKS = 8
BAND_CHUNKS = LEFT_CHUNKS + 1
BAND = BAND_CHUNKS * CHUNK
MAX_REL_DIST = 256
MLSTM_HEADS = 4
MLSTM_HEAD_DIM = 128
MLSTM_WIDTH = MLSTM_HEADS * MLSTM_HEAD_DIM
CONV_WIDTH = 4
D_FF = 4 * D_MODEL

LANES = 128
SUBLANES = 8

TOKENS = BATCH * SEQ
NUM_CHUNKS = SEQ // CHUNK
LEFT_ROWS = LEFT_CHUNKS * CHUNK
ROW_COLS = ATT_WIDTH + 2 * MLSTM_WIDTH + 2 * D_MODEL
COL_ROWS = 2 * ATT_WIDTH + 2 * MLSTM_WIDTH
GATE_ROWS = 2 * MLSTM_HEADS

Q_TILE = 2 * CHUNK
WINDOW = BAND + CHUNK
WINDOW_TILES = WINDOW // LANES
ATT_TILES_PER_STEP = 4
ATT_STEPS = SEQ // (Q_TILE * ATT_TILES_PER_STEP)
REL_ROW = 1024
ONES_ROWS = ATT_HEAD_DIM
LOG2E = 1.4426950408889634
MASKED = -1e30

ML_CHUNK = LANES
ML_STEPS = SEQ // ML_CHUNK
ML_UNITS = BATCH * MLSTM_HEADS
CONV_TAIL = 16
CONV_ROWS = 16
CONV_COLS = 128
SPLIT_ROWS = 16

TOKEN_TILE = 512
FF_TILE = 1024

F32 = jnp.float32
BF16 = jnp.bfloat16
MIB = 1024 * 1024


def _resident(shape):
    nd = len(shape)
    return pl.BlockSpec(shape, lambda *_: (0,) * nd, pipeline_mode=pl.Buffered(1))


def _rms_norm(x, g):
    ms = jnp.mean(x * x, axis=-1, keepdims=True)
    return x * lax.rsqrt(ms + NORM_EPS) * g


def _dot(a, b):
    return jnp.dot(a, b, preferred_element_type=F32)


def _dot_nt(a, b):
    return lax.dot_general(a, b, (((1,), (1,)), ((), ())), preferred_element_type=F32)


def _dot_tn(a, b):
    return lax.dot_general(a, b, (((0,), (0,)), ((), ())), preferred_element_type=F32)


def _in_proj_kernel(x_ref, g_ref, w_ref, wt_ref, wg_ref, cw_ref, cb_ref,
                    qt_ref, k_ref, vt_ref, mqk_ref, mvt_ref, mot_ref, gate_ref, gt_ref,
                    conv_scr):
    a = ATT_WIDTH
    m = MLSTM_WIDTH

    @pl.when(pl.program_id(0) % (SEQ // TOKEN_TILE) == 0)
    def _():
        conv_scr[:CONV_TAIL, :] = jnp.zeros((CONV_TAIL, 2 * m), F32)

    xn = _rms_norm(x_ref[...], g_ref[...]).astype(BF16)

    def seg(lo, width):
        return _dot(xn, w_ref[:, lo:lo + width])

    def seg_t(lo, width):
        return _dot_nt(wt_ref[lo:lo + width, :], xn)

    conv_scr[CONV_TAIL:, :m] = seg(a, m)
    conv_scr[CONV_TAIL:, m:] = seg(a + m, m)
    for c0 in range(0, 2 * m, CONV_COLS):
        cols = slice(c0, c0 + CONV_COLS)
        post_scale = 1.0 / math.sqrt(MLSTM_HEAD_DIM) if c0 >= m else None
        for r0 in range(0, TOKEN_TILE, CONV_ROWS):
            lo = CONV_TAIL - SUBLANES + r0
            ext = conv_scr[lo:lo + SUBLANES + CONV_ROWS, cols]
            acc = cb_ref[:, cols] + ext[SUBLANES:] * cw_ref[CONV_WIDTH - 1:CONV_WIDTH, cols]
            for d in range(1, CONV_WIDTH):
                tap = CONV_WIDTH - 1 - d
                acc = acc + pltpu.roll(ext, d, axis=0)[SUBLANES:] * cw_ref[tap:tap + 1, cols]
            act = acc * jax.nn.sigmoid(acc)
            if post_scale is not None:
                act = act * post_scale
            mqk_ref[r0:r0 + CONV_ROWS, cols] = act.astype(BF16)
    conv_scr[:CONV_TAIL, :] = conv_scr[TOKEN_TILE:, :]

    qt_ref[...] = (seg_t(0, a) * (LOG2E / math.sqrt(ATT_HEAD_DIM))).astype(BF16)
    vt = seg_t(a, a).astype(BF16)
    for j in range(TOKEN_TILE // LANES):
        vt_ref[0, j] = vt[:, j * LANES:(j + 1) * LANES]
    mvt_ref[0] = seg_t(2 * a, m).astype(BF16)
    mot_ref[0] = seg_t(2 * a + m, m).astype(BF16)
    gt_ref[0] = _dot_nt(wg_ref[...], xn)
    k_ref[0] = seg(0, a).astype(BF16)
    base = a + 2 * m
    for j in range(2 * D_MODEL // 512):
        gate_ref[:, j * 512:(j + 1) * 512] = seg(base + j * 512, 512).astype(BF16)


def _in_proj(x2d, g, w_rows, w_cols, w_gate, conv_w, conv_b):
    tm = TOKEN_TILE
    tiles_per_batch = SEQ // tm
    lane_tiles = tm // LANES
    row = lambda width: pl.BlockSpec((tm, width), lambda i: (i, 0))
    col = lambda height: pl.BlockSpec((1, height, tm),
                                      lambda i: (i // tiles_per_batch, 0, i % tiles_per_batch))
    k_spec = pl.BlockSpec((1, tm, ATT_WIDTH),
                          lambda i: (i // tiles_per_batch, i % tiles_per_batch, 0))
    vt_spec = pl.BlockSpec((1, lane_tiles, ATT_WIDTH, LANES),
                           lambda i: (i // tiles_per_batch, i % tiles_per_batch, 0, 0))
    out_shape = (
        jax.ShapeDtypeStruct((ATT_WIDTH, TOKENS), BF16),
        jax.ShapeDtypeStruct((BATCH, SEQ, ATT_WIDTH), BF16),
        jax.ShapeDtypeStruct((BATCH, SEQ // LANES, ATT_WIDTH, LANES), BF16),
        jax.ShapeDtypeStruct((TOKENS, 2 * MLSTM_WIDTH), BF16),
        jax.ShapeDtypeStruct((BATCH, MLSTM_WIDTH, SEQ), BF16),
        jax.ShapeDtypeStruct((BATCH, MLSTM_WIDTH, SEQ), BF16),
        jax.ShapeDtypeStruct((TOKENS, 2 * D_MODEL), BF16),
        jax.ShapeDtypeStruct((BATCH, GATE_ROWS, SEQ), F32),
    )
    return pl.pallas_call(
        _in_proj_kernel,
        grid=(TOKENS // tm,),
        in_specs=[
            row(D_MODEL),
            _resident((1, D_MODEL)),
            _resident((D_MODEL, ROW_COLS)),
            _resident((COL_ROWS, D_MODEL)),
            _resident((GATE_ROWS, D_MODEL)),
            _resident((CONV_WIDTH, 2 * MLSTM_WIDTH)),
            _resident((1, 2 * MLSTM_WIDTH)),
        ],
        out_specs=(pl.BlockSpec((ATT_WIDTH, tm), lambda i: (0, i)), k_spec, vt_spec,
                   row(2 * MLSTM_WIDTH), col(MLSTM_WIDTH), col(MLSTM_WIDTH), row(2 * D_MODEL),
                   col(GATE_ROWS)),
        out_shape=out_shape,
        scratch_shapes=[pltpu.VMEM((CONV_TAIL + tm, 2 * MLSTM_WIDTH), F32)],
        compiler_params=pltpu.CompilerParams(
            dimension_semantics=("arbitrary",), vmem_limit_bytes=48 * MIB),
        name="in_proj",
    )(x2d, g, w_rows, w_cols, w_gate, conv_w, conv_b)


def _build_bias_table(rel_ref, bias_scr):
    kb = lax.broadcasted_iota(jnp.int32, (WINDOW, LANES), 0)
    lane = lax.broadcasted_iota(jnp.int32, (WINDOW, LANES), 1)
    band_lo = jnp.where(lane >= CHUNK, CHUNK, 0)
    in_band = (kb >= band_lo) & (kb < band_lo + BAND)
    for h in range(ATT_HEADS):
        rows = jnp.broadcast_to(rel_ref[h:h + 1, :], (LANES, REL_ROW))
        shifted = pltpu.roll(rows, 0, axis=1, stride=1, stride_axis=0)
        table = jnp.concatenate(
            [jnp.transpose(shifted[:, j * LANES:(j + 1) * LANES]) for j in range(WINDOW_TILES)],
            axis=0)
        bias_scr[h, :WINDOW, :] = jnp.where(in_band, table, MASKED)
        bias_scr[h, WINDOW:, :] = jnp.full((LEFT_ROWS, LANES), MASKED, F32)


def _band_attn_kernel(qt_ref, k_ref, vt_ref, rel_ref, o_ref, bias_scr):
    step = pl.program_id(1)

    @pl.when((pl.program_id(0) == 0) & (step == 0))
    def _():
        _build_bias_table(rel_ref, bias_scr)

    d = ATT_HEAD_DIM
    zeros_half = jnp.zeros((d, Q_TILE), BF16)
    ones_rows = jnp.ones((ONES_ROWS, LANES), BF16)
    for sub in range(ATT_TILES_PER_STEP):
        tile = step * ATT_TILES_PER_STEP + sub
        q_lanes = slice(sub * Q_TILE, (sub + 1) * Q_TILE)
        first_tile = jnp.maximum(tile - LEFT_ROWS // Q_TILE, 0)
        start = pl.multiple_of(first_tile * Q_TILE, Q_TILE)
        shift = pl.multiple_of(LEFT_ROWS - (tile - first_tile) * Q_TILE, Q_TILE)
        outs = []
        for h in range(ATT_HEADS):
            pair_dims = slice((h // 2) * 2 * d, (h // 2 + 1) * 2 * d)
            qh = qt_ref[h * d:(h + 1) * d, q_lanes]
            qm = jnp.concatenate([qh, zeros_half] if h % 2 == 0 else [zeros_half, qh], axis=0)
            tile_max, tile_out = [], []
            for j in range(WINDOW_TILES):
                keys = k_ref[0, pl.ds(start + j * LANES, LANES), pair_dims]
                st = _dot(keys, qm) + bias_scr[h, pl.ds(shift + j * LANES, LANES), :]
                m_j = jnp.max(st, axis=0, keepdims=True)
                p_j = jnp.exp2(st - m_j).astype(BF16)
                v_ones = jnp.concatenate(
                    [vt_ref[0, first_tile + j, h * d:(h + 1) * d, :], ones_rows], axis=0)
                tile_max.append(m_j)
                tile_out.append(_dot(v_ones, p_j))
            m = functools.reduce(jnp.maximum, tile_max)
            acc = functools.reduce(jnp.add, [jnp.exp2(m_j - m) * o
                                             for m_j, o in zip(tile_max, tile_out)])
            outs.append(acc[:d] / acc[d:d + 1])
        o_ref[0, q_lanes, :] = jnp.transpose(jnp.concatenate(outs, axis=0)).astype(BF16)


def _band_attn(qt, k_pad, vt_pad, rel_rows):
    return pl.pallas_call(
        _band_attn_kernel,
        grid=(BATCH, ATT_STEPS),
        in_specs=[
            pl.BlockSpec((ATT_WIDTH, ATT_TILES_PER_STEP * Q_TILE),
                         lambda b, s: (0, b * ATT_STEPS + s)),
            pl.BlockSpec((1, SEQ, ATT_WIDTH), lambda b, s: (b, 0, 0)),
            pl.BlockSpec((1, SEQ // LANES, ATT_WIDTH, LANES), lambda b, s: (b, 0, 0, 0)),
            _resident((ATT_HEADS, REL_ROW)),
        ],
        out_specs=pl.BlockSpec((1, ATT_TILES_PER_STEP * Q_TILE, ATT_WIDTH),
                               lambda b, s: (b, s, 0)),
        out_shape=jax.ShapeDtypeStruct((BATCH, SEQ, ATT_WIDTH), BF16),
        scratch_shapes=[pltpu.VMEM((ATT_HEADS, WINDOW + LEFT_ROWS, LANES), F32)],
        compiler_params=pltpu.CompilerParams(
            dimension_semantics=("arbitrary", "arbitrary"), vmem_limit_bytes=40 * MIB),
        name="band_attn",
    )(qt, k_pad, vt_pad, rel_rows)


def _scan_lanes(x, op, identity, segment):
    pos = lax.broadcasted_iota(jnp.int32, x.shape, 1) % segment
    d = 1
    while d < segment:
        x = op(x, jnp.where(pos >= d, pltpu.roll(x, d, axis=1), identity))
        d *= 2
    return x


def _split3(x):
    hi = x.astype(BF16).astype(F32)
    r1 = x - hi
    mid = r1.astype(BF16).astype(F32)
    lo = (r1 - mid).astype(BF16).astype(F32)
    return hi, mid, lo


GP_BCUM, GP_STAB, GP_W, GP_BLAST, GP_MLOC, GP_R_HI, GP_R_MID, GP_R_LO = range(8)
GP_PLANES = 8


def _gate_planes(gt_ref, gb_ref, planes_ref):
    rows_all = BATCH * GATE_ROWS
    gates = jnp.concatenate([gt_ref[b] for b in range(BATCH)], axis=0) + gb_ref[...]
    logf = jnp.minimum(gates, 0.0) - jnp.log1p(jnp.exp(-jnp.abs(gates)))
    bcum = pltpu.roll(_scan_lanes(logf, jnp.add, 0.0, ML_CHUNK), rows_all - MLSTM_HEADS, axis=0)
    b_last = jnp.broadcast_to(bcum[:, ML_CHUNK - 1:ML_CHUNK], (rows_all, ML_CHUNK))
    a_row = b_last - bcum + gates
    m_loc = jnp.broadcast_to(jnp.max(a_row, axis=1, keepdims=True), (rows_all, ML_CHUNK))
    r_row = gates - bcum
    planes_ref[GP_BCUM] = bcum
    planes_ref[GP_STAB] = bcum + _scan_lanes(r_row, jnp.maximum, -jnp.inf, ML_CHUNK)
    planes_ref[GP_W] = jnp.exp(a_row - m_loc)
    planes_ref[GP_BLAST] = b_last
    planes_ref[GP_MLOC] = m_loc
    for plane, part in zip((GP_R_HI, GP_R_MID, GP_R_LO), _split3(r_row)):
        planes_ref[plane] = part


def _mlstm_kernel(mqk_ref, vt_ref, mot_ref, gt_ref, gt_next_ref, gb_ref, ng_ref, o_ref,
                  c_scr, n_scr, m_scr, gp_scr):
    step = pl.program_id(0)
    hd = MLSTM_HEAD_DIM
    gp_ref = gp_scr

    @pl.when(step == 0)
    def _():
        c_scr[...] = jnp.zeros_like(c_scr)
        n_scr[...] = jnp.zeros_like(n_scr)
        m_scr[...] = jnp.zeros_like(m_scr)
        _gate_planes(gt_ref, gb_ref, gp_scr)

    bcum = gp_ref[GP_BCUM]
    b_last = gp_ref[GP_BLAST]
    m_loc = gp_ref[GP_MLOC]
    w_row = gp_ref[GP_W]
    m_prev = m_scr[...]
    m_new = jnp.maximum(b_last + m_prev, m_loc)
    s_prev = jnp.exp(b_last + m_prev - m_new)
    s_loc = jnp.exp(m_loc - m_new)
    m_scr[...] = m_new
    g_row = bcum + m_prev
    m_t = jnp.maximum(g_row, gp_ref[GP_STAB])
    inter = jnp.exp(g_row - m_t)
    inv_floor = jnp.exp(-m_t)
    n_prev = n_scr[...]
    w_bf = w_row.astype(BF16)
    n_prev_bf = n_prev.astype(BF16)
    r_parts = (gp_ref[GP_R_HI], gp_ref[GP_R_MID], gp_ref[GP_R_LO])
    e_parts = _split3(bcum - m_t)
    split_row = lax.broadcasted_iota(jnp.int32, (SPLIT_ROWS, ML_CHUNK), 0)

    def outer_sum_operand(parts, one, first):
        other = 3 - first
        out = jnp.where((split_row >= other) & (split_row < other + 3), 1.0, 0.0)
        for i, part in enumerate(parts):
            out = jnp.where(split_row == first + i, part[one], out)
        return out.astype(BF16)

    src = lax.broadcasted_iota(jnp.int32, (ML_CHUNK, ML_CHUNK), 0)
    qry = lax.broadcasted_iota(jnp.int32, (ML_CHUNK, ML_CHUNK), 1)
    causal = src <= qry
    new_n = []

    for b in range(BATCH):
        grp = slice(b * GATE_ROWS, (b + 1) * GATE_ROWS)
        outs = []
        for h in range(MLSTM_HEADS):
            u = b * MLSTM_HEADS + h
            r = b * GATE_ROWS + h
            one = slice(r, r + 1)
            sl = slice(h * hd, (h + 1) * hd)
            qh = mqk_ref[b, :, sl]
            kh = mqk_ref[b, :, MLSTM_WIDTH + h * hd:MLSTM_WIDTH + (h + 1) * hd]
            vt = vt_ref[b, sl, :]

            c_prev = c_scr[u]
            c_loc = _dot((vt.astype(F32) * w_row[one]).astype(BF16), kh)
            n_loc = _dot(w_bf[grp], kh)[h:h + 1]
            c_scr[u] = s_prev[one] * c_prev + s_loc[one] * c_loc
            new_n.append(s_prev[one] * n_prev[one] + s_loc[one] * n_loc)
            if h == MLSTM_HEADS - 1:
                new_n.append(jnp.zeros((GATE_ROWS - MLSTM_HEADS, hd), F32))

            lhs = outer_sum_operand(r_parts, one, 0)
            rhs = outer_sum_operand(e_parts, one, 3)
            decay = jnp.exp(jnp.where(causal, _dot_tn(lhs, rhs), -jnp.inf))
            pt = _dot_nt(kh, qh) * decay
            nq = _dot_nt(n_prev_bf[grp], qh)[h:h + 1]
            den = inter[one] * nq + jnp.sum(pt, axis=0, keepdims=True)
            num = inter[one] * _dot_nt(c_prev.astype(BF16), qh) + _dot(vt, pt.astype(BF16))
            hout = num * (1.0 / jnp.maximum(jnp.abs(den), inv_floor[one]))
            ms = jnp.mean(hout * hout, axis=0, keepdims=True)
            y = hout * lax.rsqrt(ms + NORM_EPS) * ng_ref[sl, :]
            outs.append(jax.nn.sigmoid(mot_ref[b, sl, :].astype(F32)) * y)
        o_ref[b] = jnp.transpose(jnp.concatenate(outs, axis=0)).astype(BF16)

    n_scr[...] = jnp.concatenate(new_n, axis=0)
    _gate_planes(gt_next_ref, gb_ref, gp_scr)


def _mlstm(mqk, mvt, mot, gt, gate_bias, norm_g):
    tok = lambda width: pl.BlockSpec((BATCH, ML_CHUNK, width), lambda c: (0, c, 0))
    feat = lambda height: pl.BlockSpec((BATCH, height, ML_CHUNK), lambda c: (0, 0, c))
    rows_all = BATCH * GATE_ROWS
    return pl.pallas_call(
        _mlstm_kernel,
        grid=(ML_STEPS,),
        in_specs=[
            tok(2 * MLSTM_WIDTH), feat(MLSTM_WIDTH), feat(MLSTM_WIDTH), feat(GATE_ROWS),
            pl.BlockSpec((BATCH, GATE_ROWS, ML_CHUNK),
                         lambda c: (0, 0, jnp.minimum(c + 1, ML_STEPS - 1))),
            _resident((rows_all, ML_CHUNK)),
            _resident((MLSTM_WIDTH, ML_CHUNK)),
        ],
        out_specs=tok(MLSTM_WIDTH),
        out_shape=jax.ShapeDtypeStruct((BATCH, SEQ, MLSTM_WIDTH), BF16),
        scratch_shapes=[
            pltpu.VMEM((ML_UNITS, MLSTM_HEAD_DIM, MLSTM_HEAD_DIM), F32),
            pltpu.VMEM((rows_all, MLSTM_HEAD_DIM), F32),
            pltpu.VMEM((rows_all, ML_CHUNK), F32),
            pltpu.VMEM((GP_PLANES, rows_all, ML_CHUNK), F32),
        ],
        compiler_params=pltpu.CompilerParams(
            dimension_semantics=("arbitrary",), vmem_limit_bytes=32 * MIB),
        name="mlstm",
    )(mqk, mvt, mot, gt, gt, gate_bias, norm_g)


def _merge_ffn_kernel(x_ref, att_ref, ml_ref, gate_ref, wa_ref, wm_ref, wo_ref,
                      g_ref, wu_ref, wd_ref, fg_ref, o_ref, *, final_norm):
    ga = gate_ref[:, :D_MODEL].astype(F32)
    gm = gate_ref[:, D_MODEL:].astype(F32)
    y = (jax.nn.sigmoid(ga) * _dot(att_ref[...], wa_ref[...])
         + jax.nn.sigmoid(gm) * _dot(ml_ref[...], wm_ref[...]))
    x = x_ref[...] + _dot(y.astype(BF16), wo_ref[...])
    hn = _rms_norm(x, g_ref[...]).astype(BF16)
    acc = x
    for j in range(D_FF // FF_TILE):
        cols = slice(j * FF_TILE, (j + 1) * FF_TILE)
        h = jnp.maximum(_dot(hn, wu_ref[:, cols]), 0.0)
        acc = acc + _dot((h * h).astype(BF16), wd_ref[cols, :])
    if final_norm:
        acc = _rms_norm(acc, fg_ref[...])
    o_ref[...] = acc


def _merge_ffn(x2d, att, ml, gates, wa, wm, wo, g, wu, wd, final_g, final_norm):
    tm = TOKEN_TILE
    row = lambda width: pl.BlockSpec((tm, width), lambda i: (i, 0))
    return pl.pallas_call(
        functools.partial(_merge_ffn_kernel, final_norm=final_norm),
        grid=(TOKENS // tm,),
        in_specs=[row(D_MODEL), row(ATT_WIDTH), row(MLSTM_WIDTH), row(2 * D_MODEL),
                  _resident((ATT_WIDTH, D_MODEL)), _resident((MLSTM_WIDTH, D_MODEL)),
                  _resident((D_MODEL, D_MODEL)), _resident((1, D_MODEL)),
                  _resident((D_MODEL, D_FF)), _resident((D_FF, D_MODEL)),
                  _resident((1, D_MODEL))],
        out_specs=row(D_MODEL),
        out_shape=jax.ShapeDtypeStruct((TOKENS, D_MODEL), F32),
        compiler_params=pltpu.CompilerParams(
            dimension_semantics=("arbitrary",), vmem_limit_bytes=56 * MIB),
        name="merge_ffn",
    )(x2d, att, ml, gates, wa, wm, wo, g, wu, wd, final_g)


def _rel_bias_rows(rel_bias):
    far = jnp.broadcast_to(rel_bias[:, 2 * MAX_REL_DIST:], (ATT_HEADS, MAX_REL_DIST))
    near = rel_bias[:, MAX_REL_DIST + 1 - Q_TILE:][:, ::-1]
    rest = jnp.broadcast_to(rel_bias[:, 2 * MAX_REL_DIST:],
                            (ATT_HEADS, REL_ROW - MAX_REL_DIST - near.shape[1]))
    return jnp.concatenate([far, near, rest], axis=1).astype(F32) * LOG2E


def kernel(x, mix_norm_g, w_in, conv_w, conv_b, b_igate, b_fgate, rel_bias, mh_norm_g,
           w_att_proj, w_mlstm_proj, w_out, ffn_norm_g, w_up, w_down, final_norm_g):
    h = x.reshape(TOKENS, D_MODEL)
    a = ATT_WIDTH
    m = MLSTM_WIDTH
    gate_lo = 3 * a + 4 * m
    gate_hi = gate_lo + GATE_ROWS
    final_g = final_norm_g.reshape(1, D_MODEL)
    for l in range(DEPTH):
        w = w_in[l]
        w_rows = jnp.concatenate([w[:, a:2 * a], w[:, 3 * a:3 * a + 2 * m], w[:, gate_hi:]],
                                 axis=1).astype(BF16)
        w_cols = jnp.concatenate([w[:, :a], w[:, 2 * a:3 * a], w[:, 3 * a + 2 * m:gate_lo]],
                                 axis=1).T.astype(BF16)
        w_gate = w[:, gate_lo:gate_hi].T.astype(BF16)
        gate_bias = jnp.tile(
            jnp.broadcast_to(jnp.concatenate([b_igate[l], b_fgate[l]])[:, None],
                             (GATE_ROWS, ML_CHUNK)), (BATCH, 1))
        norm_g = jnp.broadcast_to(mh_norm_g[l][:, None], (MLSTM_WIDTH, ML_CHUNK))

        qt, k_tok, vt_tiles, mqk, mvt, mot, gates, gt = _in_proj(
            h, mix_norm_g[l].reshape(1, D_MODEL), w_rows, w_cols, w_gate,
            conv_w[l], conv_b[l].reshape(1, 2 * MLSTM_WIDTH))
        att = _band_attn(qt, k_tok, vt_tiles, _rel_bias_rows(rel_bias[l]))
        ml = _mlstm(mqk.reshape(BATCH, SEQ, 2 * MLSTM_WIDTH), mvt, mot, gt, gate_bias, norm_g)
        h = _merge_ffn(h, att.reshape(TOKENS, ATT_WIDTH), ml.reshape(TOKENS, MLSTM_WIDTH), gates,
                       w_att_proj[l].astype(BF16), w_mlstm_proj[l].astype(BF16),
                       w_out[l].astype(BF16), ffn_norm_g[l].reshape(1, D_MODEL),
                       w_up[l].astype(BF16), w_down[l].astype(BF16), final_g,
                       final_norm=(l == DEPTH - 1))
    return h.reshape(BATCH, SEQ, D_MODEL)
```

```python
import functools
import math

import jax
import jax.numpy as jnp
from jax import lax
from jax.experimental import pallas as pl
from jax.experimental.pallas import tpu as pltpu

D_MODEL = 1024
BATCH = 4
SEQ = 4096
DEPTH = 2
CHUNK = 64
NORM_EPS = 1e-6
ATT_HEADS = 8
ATT_HEAD_DIM = 64
ATT_WIDTH = ATT_HEADS * ATT_HEAD_DIM
LEFT_CHUNKS = 8
BAND_CHUNKS = LEFT_CHUNKS + 1
BAND = BAND_CHUNKS * CHUNK
MAX_REL_DIST = 256
MLSTM_HEADS = 4
MLSTM_HEAD_DIM = 128
MLSTM_WIDTH = MLSTM_HEADS * MLSTM_HEAD_DIM
CONV_WIDTH = 4
D_FF = 4 * D_MODEL

LANES = 128
SUBLANES = 8

TOKENS = BATCH * SEQ
NUM_CHUNKS = SEQ // CHUNK
LEFT_ROWS = LEFT_CHUNKS * CHUNK
ROW_COLS = ATT_WIDTH + 2 * MLSTM_WIDTH + 2 * D_MODEL
COL_ROWS = 2 * ATT_WIDTH + 2 * MLSTM_WIDTH
GATE_ROWS = 2 * MLSTM_HEADS

Q_TILE = 2 * CHUNK
WINDOW = BAND + CHUNK
WINDOW_TILES = WINDOW // LANES
ATT_TILES_PER_STEP = 4
ATT_STEPS = SEQ // (Q_TILE * ATT_TILES_PER_STEP)
REL_ROW = 1024
ONES_ROWS = ATT_HEAD_DIM
LOG2E = 1.4426950408889634
MASKED = -1e30

ML_CHUNK = LANES
ML_STEPS = SEQ // ML_CHUNK
ML_UNITS = BATCH * MLSTM_HEADS
CONV_TAIL = 16
CONV_ROWS = 16
CONV_COLS = 128
SPLIT_ROWS = 16

TOKEN_TILE = 512
FF_TILE = 1024

F32 = jnp.float32
BF16 = jnp.bfloat16
MIB = 1024 * 1024


def _resident(shape, layer=None):
    nd = len(shape)
    if layer is None:
        return pl.BlockSpec(shape, lambda *_: (0,) * nd, pipeline_mode=pl.Buffered(1))
    return pl.BlockSpec((None,) + tuple(shape), lambda *_: (layer,) + (0,) * nd,
                        pipeline_mode=pl.Buffered(1))


def _rms_norm(x, g):
    ms = jnp.mean(x * x, axis=-1, keepdims=True)
    return x * lax.rsqrt(ms + NORM_EPS) * g


def _dot(a, b):
    return jnp.dot(a, b, preferred_element_type=F32)


def _dot_nt(a, b):
    return lax.dot_general(a, b, (((1,), (1,)), ((), ())), preferred_element_type=F32)


def _dot_tn(a, b):
    return lax.dot_general(a, b, (((0,), (0,)), ((), ())), preferred_element_type=F32)


def _in_proj_kernel(x_ref, g_ref, w_ref, wt_ref, wg_ref, cw_ref, cb_ref,
                    qt_ref, k_ref, vt_ref, mqk_ref, mvt_ref, mot_ref, gate_ref, gt_ref,
                    conv_scr):
    a = ATT_WIDTH
    m = MLSTM_WIDTH

    @pl.when(pl.program_id(0) % (SEQ // TOKEN_TILE) == 0)
    def _():
        conv_scr[:CONV_TAIL, :] = jnp.zeros((CONV_TAIL, 2 * m), F32)

    xn = _rms_norm(x_ref[...], g_ref[...]).astype(BF16)

    def seg(lo, width):
        return _dot(xn, w_ref[:, lo:lo + width])

    def seg_t(lo, width):
        return _dot_nt(wt_ref[lo:lo + width, :], xn)

    conv_scr[CONV_TAIL:, :m] = seg(a, m)
    conv_scr[CONV_TAIL:, m:] = seg(a + m, m)
    for c0 in range(0, 2 * m, CONV_COLS):
        cols = slice(c0, c0 + CONV_COLS)
        post_scale = 1.0 / math.sqrt(MLSTM_HEAD_DIM) if c0 >= m else None
        for r0 in range(0, TOKEN_TILE, CONV_ROWS):
            lo = CONV_TAIL - SUBLANES + r0
            ext = conv_scr[lo:lo + SUBLANES + CONV_ROWS, cols]
            acc = cb_ref[:, cols] + ext[SUBLANES:] * cw_ref[CONV_WIDTH - 1:CONV_WIDTH, cols]
            for d in range(1, CONV_WIDTH):
                tap = CONV_WIDTH - 1 - d
                acc = acc + pltpu.roll(ext, d, axis=0)[SUBLANES:] * cw_ref[tap:tap + 1, cols]
            half = 0.5 * acc
            act = half + half * jnp.tanh(half)
            if post_scale is not None:
                act = act * post_scale
            mqk_ref[r0:r0 + CONV_ROWS, cols] = act.astype(BF16)
    conv_scr[:CONV_TAIL, :] = conv_scr[TOKEN_TILE:, :]

    qt_ref[...] = (seg_t(0, a) * (LOG2E / math.sqrt(ATT_HEAD_DIM))).astype(BF16)
    vt = seg_t(a, a).astype(BF16)
    for j in range(TOKEN_TILE // LANES):
        vt_ref[0, j] = vt[:, j * LANES:(j + 1) * LANES]
    mvt_ref[0] = seg_t(2 * a, m).astype(BF16)
    mot_ref[0] = seg_t(2 * a + m, m).astype(BF16)
    gt_ref[0] = _dot_nt(wg_ref[...], xn)
    k_ref[0] = seg(0, a).astype(BF16)
    base = a + 2 * m
    for j in range(2 * D_MODEL // 512):
        gate_ref[:, j * 512:(j + 1) * 512] = seg(base + j * 512, 512).astype(BF16)


def _in_proj(layer, x2d, g, w_rows, w_cols, w_gate, conv_w, conv_b):
    tm = TOKEN_TILE
    tiles_per_batch = SEQ // tm
    lane_tiles = tm // LANES
    row = lambda width: pl.BlockSpec((tm, width), lambda i: (i, 0))
    col = lambda height: pl.BlockSpec((1, height, tm),
                                      lambda i: (i // tiles_per_batch, 0, i % tiles_per_batch))
    k_spec = pl.BlockSpec((1, tm, ATT_WIDTH),
                          lambda i: (i // tiles_per_batch, i % tiles_per_batch, 0))
    vt_spec = pl.BlockSpec((1, lane_tiles, ATT_WIDTH, LANES),
                           lambda i: (i // tiles_per_batch, i % tiles_per_batch, 0, 0))
    out_shape = (
        jax.ShapeDtypeStruct((ATT_WIDTH, TOKENS), BF16),
        jax.ShapeDtypeStruct((BATCH, SEQ, ATT_WIDTH), BF16),
        jax.ShapeDtypeStruct((BATCH, SEQ // LANES, ATT_WIDTH, LANES), BF16),
        jax.ShapeDtypeStruct((TOKENS, 2 * MLSTM_WIDTH), BF16),
        jax.ShapeDtypeStruct((BATCH, MLSTM_WIDTH, SEQ), BF16),
        jax.ShapeDtypeStruct((BATCH, MLSTM_WIDTH, SEQ), BF16),
        jax.ShapeDtypeStruct((TOKENS, 2 * D_MODEL), BF16),
        jax.ShapeDtypeStruct((BATCH, GATE_ROWS, SEQ), F32),
    )
    return pl.pallas_call(
        _in_proj_kernel,
        grid=(TOKENS // tm,),
        in_specs=[
            row(D_MODEL),
            _resident((1, D_MODEL), layer),
            _resident((D_MODEL, ROW_COLS), layer),
            _resident((COL_ROWS, D_MODEL), layer),
            _resident((GATE_ROWS, D_MODEL), layer),
            _resident((CONV_WIDTH, 2 * MLSTM_WIDTH), layer),
            _resident((1, 2 * MLSTM_WIDTH), layer),
        ],
        out_specs=(pl.BlockSpec((ATT_WIDTH, tm), lambda i: (0, i)), k_spec, vt_spec,
                   row(2 * MLSTM_WIDTH), col(MLSTM_WIDTH), col(MLSTM_WIDTH), row(2 * D_MODEL),
                   col(GATE_ROWS)),
        out_shape=out_shape,
        scratch_shapes=[pltpu.VMEM((CONV_TAIL + tm, 2 * MLSTM_WIDTH), F32)],
        compiler_params=pltpu.CompilerParams(
            dimension_semantics=("arbitrary",), vmem_limit_bytes=48 * MIB),
        name="in_proj",
    )(x2d, g, w_rows, w_cols, w_gate, conv_w, conv_b)


def _build_bias_table(rel_ref, bias_scr):
    kb = lax.broadcasted_iota(jnp.int32, (WINDOW, LANES), 0)
    lane = lax.broadcasted_iota(jnp.int32, (WINDOW, LANES), 1)
    band_lo = jnp.where(lane >= CHUNK, CHUNK, 0)
    in_band = (kb >= band_lo) & (kb < band_lo + BAND)
    for h in range(ATT_HEADS):
        rows = jnp.broadcast_to(rel_ref[h:h + 1, :], (LANES, REL_ROW))
        shifted = pltpu.roll(rows, 0, axis=1, stride=1, stride_axis=0)
        table = jnp.concatenate(
            [jnp.transpose(shifted[:, j * LANES:(j + 1) * LANES]) for j in range(WINDOW_TILES)],
            axis=0)
        bias_scr[h, :WINDOW, :] = jnp.where(in_band, table, MASKED)
        bias_scr[h, WINDOW:, :] = jnp.full((LEFT_ROWS, LANES), MASKED, F32)


def _band_attn_kernel(qt_ref, k_ref, vt_ref, rel_ref, o_ref, bias_scr):
    step = pl.program_id(1)

    @pl.when((pl.program_id(0) == 0) & (step == 0))
    def _():
        _build_bias_table(rel_ref, bias_scr)

    d = ATT_HEAD_DIM
    zeros_half = jnp.zeros((d, Q_TILE), BF16)
    ones_rows = jnp.ones((ONES_ROWS, LANES), BF16)
    for sub in range(ATT_TILES_PER_STEP):
        tile = step * ATT_TILES_PER_STEP + sub
        q_lanes = slice(sub * Q_TILE, (sub + 1) * Q_TILE)
        first_tile = jnp.maximum(tile - LEFT_ROWS // Q_TILE, 0)
        start = pl.multiple_of(first_tile * Q_TILE, Q_TILE)
        shift = pl.multiple_of(LEFT_ROWS - (tile - first_tile) * Q_TILE, Q_TILE)
        outs = []
        for h in range(ATT_HEADS):
            pair_dims = slice((h // 2) * 2 * d, (h // 2 + 1) * 2 * d)
            qh = qt_ref[h * d:(h + 1) * d, q_lanes]
            qm = jnp.concatenate([qh, zeros_half] if h % 2 == 0 else [zeros_half, qh], axis=0)
            tile_max, tile_out = [], []
            for j in range(WINDOW_TILES):
                keys = k_ref[0, pl.ds(start + j * LANES, LANES), pair_dims]
                st = _dot(keys, qm) + bias_scr[h, pl.ds(shift + j * LANES, LANES), :]
                m_j = jnp.max(st, axis=0, keepdims=True)
                p_j = jnp.exp2(st - m_j).astype(BF16)
                v_ones = jnp.concatenate(
                    [vt_ref[0, first_tile + j, h * d:(h + 1) * d, :], ones_rows], axis=0)
                tile_max.append(m_j)
                tile_out.append(_dot(v_ones, p_j))
            m = functools.reduce(jnp.maximum, tile_max)
            acc = functools.reduce(jnp.add, [jnp.exp2(m_j - m) * o
                                             for m_j, o in zip(tile_max, tile_out)])
            outs.append(acc[:d] / acc[d:d + 1])
        o_ref[0, q_lanes, :] = jnp.transpose(jnp.concatenate(outs, axis=0)).astype(BF16)


def _band_attn(layer, qt, k_tok, vt_tiles, rel_rows):
    return pl.pallas_call(
        _band_attn_kernel,
        grid=(BATCH, ATT_STEPS),
        in_specs=[
            pl.BlockSpec((ATT_WIDTH, ATT_TILES_PER_STEP * Q_TILE),
                         lambda b, s: (0, b * ATT_STEPS + s)),
            pl.BlockSpec((1, SEQ, ATT_WIDTH), lambda b, s: (b, 0, 0)),
            pl.BlockSpec((1, SEQ // LANES, ATT_WIDTH, LANES), lambda b, s: (b, 0, 0, 0)),
            _resident((ATT_HEADS, REL_ROW), layer),
        ],
        out_specs=pl.BlockSpec((1, ATT_TILES_PER_STEP * Q_TILE, ATT_WIDTH),
                               lambda b, s: (b, s, 0)),
        out_shape=jax.ShapeDtypeStruct((BATCH, SEQ, ATT_WIDTH), BF16),
        scratch_shapes=[pltpu.VMEM((ATT_HEADS, WINDOW + LEFT_ROWS, LANES), F32)],
        compiler_params=pltpu.CompilerParams(
            dimension_semantics=("arbitrary", "arbitrary"), vmem_limit_bytes=40 * MIB),
        name="band_attn",
    )(qt, k_tok, vt_tiles, rel_rows)


def _scan_lanes(x, op, identity, segment):
    pos = lax.broadcasted_iota(jnp.int32, x.shape, 1) % segment
    d = 1
    while d < segment:
        x = op(x, jnp.where(pos >= d, pltpu.roll(x, d, axis=1), identity))
        d *= 2
    return x


def _split3(x):
    hi = x.astype(BF16).astype(F32)
    r1 = x - hi
    mid = r1.astype(BF16).astype(F32)
    lo = (r1 - mid).astype(BF16).astype(F32)
    return hi, mid, lo


GP_BCUM, GP_STAB, GP_W, GP_BLAST, GP_MLOC, GP_R_HI, GP_R_MID, GP_R_LO = range(8)
GP_PLANES = 8


def _gate_planes(gt_ref, gb_ref, planes_ref):
    rows_all = BATCH * GATE_ROWS
    gates = jnp.concatenate([gt_ref[b] for b in range(BATCH)], axis=0) + gb_ref[...]
    logf = jnp.minimum(gates, 0.0) - jnp.log1p(jnp.exp(-jnp.abs(gates)))
    bcum = pltpu.roll(_scan_lanes(logf, jnp.add, 0.0, ML_CHUNK), rows_all - MLSTM_HEADS, axis=0)
    b_last = jnp.broadcast_to(bcum[:, ML_CHUNK - 1:ML_CHUNK], (rows_all, ML_CHUNK))
    a_row = b_last - bcum + gates
    m_loc = jnp.broadcast_to(jnp.max(a_row, axis=1, keepdims=True), (rows_all, ML_CHUNK))
    r_row = gates - bcum
    planes_ref[GP_BCUM] = bcum
    planes_ref[GP_STAB] = bcum + _scan_lanes(r_row, jnp.maximum, -jnp.inf, ML_CHUNK)
    planes_ref[GP_W] = jnp.exp(a_row - m_loc)
    planes_ref[GP_BLAST] = b_last
    planes_ref[GP_MLOC] = m_loc
    for plane, part in zip((GP_R_HI, GP_R_MID, GP_R_LO), _split3(r_row)):
        planes_ref[plane] = part


def _mlstm_kernel(mqk_ref, vt_ref, mot_ref, gt_ref, gt_next_ref, gb_ref, ng_ref, o_ref,
                  c_scr, n_scr, m_scr, gp_scr):
    step = pl.program_id(0)
    hd = MLSTM_HEAD_DIM
    gp_ref = gp_scr

    @pl.when(step == 0)
    def _():
        c_scr[...] = jnp.zeros_like(c_scr)
        n_scr[...] = jnp.zeros_like(n_scr)
        m_scr[...] = jnp.zeros_like(m_scr)
        _gate_planes(gt_ref, gb_ref, gp_scr)

    bcum = gp_ref[GP_BCUM]
    b_last = gp_ref[GP_BLAST]
    m_loc = gp_ref[GP_MLOC]
    w_row = gp_ref[GP_W]
    m_prev = m_scr[...]
    m_new = jnp.maximum(b_last + m_prev, m_loc)
    s_prev = jnp.exp(b_last + m_prev - m_new)
    s_loc = jnp.exp(m_loc - m_new)
    m_scr[...] = m_new
    g_row = bcum + m_prev
    m_t = jnp.maximum(g_row, gp_ref[GP_STAB])
    inter = jnp.exp(g_row - m_t)
    inv_floor = jnp.exp(-m_t)
    n_prev = n_scr[...]
    w_bf = w_row.astype(BF16)
    n_prev_bf = n_prev.astype(BF16)
    r_parts = (gp_ref[GP_R_HI], gp_ref[GP_R_MID], gp_ref[GP_R_LO])
    e_parts = _split3(bcum - m_t)
    split_row = lax.broadcasted_iota(jnp.int32, (SPLIT_ROWS, ML_CHUNK), 0)

    def outer_sum_operand(parts, one, first):
        other = 3 - first
        out = jnp.where((split_row >= other) & (split_row < other + 3), 1.0, 0.0)
        for i, part in enumerate(parts):
            out = jnp.where(split_row == first + i, part[one], out)
        return out.astype(BF16)

    src = lax.broadcasted_iota(jnp.int32, (ML_CHUNK, ML_CHUNK), 0)
    qry = lax.broadcasted_iota(jnp.int32, (ML_CHUNK, ML_CHUNK), 1)
    causal = src <= qry
    new_n = []

    for b in range(BATCH):
        grp = slice(b * GATE_ROWS, (b + 1) * GATE_ROWS)
        outs = []
        for h in range(MLSTM_HEADS):
            u = b * MLSTM_HEADS + h
            r = b * GATE_ROWS + h
            one = slice(r, r + 1)
            sl = slice(h * hd, (h + 1) * hd)
            qh = mqk_ref[b, :, sl]
            kh = mqk_ref[b, :, MLSTM_WIDTH + h * hd:MLSTM_WIDTH + (h + 1) * hd]
            vt = vt_ref[b, sl, :]

            c_prev = c_scr[u]
            c_loc = _dot((vt.astype(F32) * w_row[one]).astype(BF16), kh)
            n_loc = _dot(w_bf[grp], kh)[h:h + 1]
            c_scr[u] = s_prev[one] * c_prev + s_loc[one] * c_loc
            new_n.append(s_prev[one] * n_prev[one] + s_loc[one] * n_loc)
            if h == MLSTM_HEADS - 1:
                new_n.append(jnp.zeros((GATE_ROWS - MLSTM_HEADS, hd), F32))

            lhs = outer_sum_operand(r_parts, one, 0)
            rhs = outer_sum_operand(e_parts, one, 3)
            decay = jnp.exp(jnp.where(causal, _dot_tn(lhs, rhs), -jnp.inf))
            pt = _dot_nt(kh, qh) * decay
            nq = _dot_nt(n_prev_bf[grp], qh)[h:h + 1]
            den = inter[one] * nq + jnp.sum(pt, axis=0, keepdims=True)
            num = inter[one] * _dot_nt(c_prev.astype(BF16), qh) + _dot(vt, pt.astype(BF16))
            hout = num * (1.0 / jnp.maximum(jnp.abs(den), inv_floor[one]))
            ms = jnp.mean(hout * hout, axis=0, keepdims=True)
            y = hout * lax.rsqrt(ms + NORM_EPS) * ng_ref[sl, :]
            outs.append(jax.nn.sigmoid(mot_ref[b, sl, :].astype(F32)) * y)
        o_ref[b] = jnp.transpose(jnp.concatenate(outs, axis=0)).astype(BF16)

    n_scr[...] = jnp.concatenate(new_n, axis=0)
    _gate_planes(gt_next_ref, gb_ref, gp_scr)


def _mlstm(layer, mqk, mvt, mot, gt, gate_bias, norm_g):
    tok = lambda width: pl.BlockSpec((BATCH, ML_CHUNK, width), lambda c: (0, c, 0))
    feat = lambda height: pl.BlockSpec((BATCH, height, ML_CHUNK), lambda c: (0, 0, c))
    rows_all = BATCH * GATE_ROWS
    return pl.pallas_call(
        _mlstm_kernel,
        grid=(ML_STEPS,),
        in_specs=[
            tok(2 * MLSTM_WIDTH), feat(MLSTM_WIDTH), feat(MLSTM_WIDTH), feat(GATE_ROWS),
            pl.BlockSpec((BATCH, GATE_ROWS, ML_CHUNK),
                         lambda c: (0, 0, jnp.minimum(c + 1, ML_STEPS - 1))),
            _resident((rows_all, ML_CHUNK), layer),
            _resident((MLSTM_WIDTH, ML_CHUNK), layer),
        ],
        out_specs=tok(MLSTM_WIDTH),
        out_shape=jax.ShapeDtypeStruct((BATCH, SEQ, MLSTM_WIDTH), BF16),
        scratch_shapes=[
            pltpu.VMEM((ML_UNITS, MLSTM_HEAD_DIM, MLSTM_HEAD_DIM), F32),
            pltpu.VMEM((rows_all, MLSTM_HEAD_DIM), F32),
            pltpu.VMEM((rows_all, ML_CHUNK), F32),
            pltpu.VMEM((GP_PLANES, rows_all, ML_CHUNK), F32),
        ],
        compiler_params=pltpu.CompilerParams(
            dimension_semantics=("arbitrary",), vmem_limit_bytes=32 * MIB),
        name="mlstm",
    )(mqk, mvt, mot, gt, gt, gate_bias, norm_g)


def _merge_ffn_kernel(x_ref, att_ref, ml_ref, gate_ref, wa_ref, wm_ref, wo_ref,
                      g_ref, wu_ref, wd_ref, fg_ref, o_ref, *, final_norm):
    ga = gate_ref[:, :D_MODEL].astype(F32)
    gm = gate_ref[:, D_MODEL:].astype(F32)
    y = (jax.nn.sigmoid(ga) * _dot(att_ref[...], wa_ref[...])
         + jax.nn.sigmoid(gm) * _dot(ml_ref[...], wm_ref[...]))
    x = x_ref[...] + _dot(y.astype(BF16), wo_ref[...])
    hn = _rms_norm(x, g_ref[...]).astype(BF16)
    acc = x
    for j in range(D_FF // FF_TILE):
        cols = slice(j * FF_TILE, (j + 1) * FF_TILE)
        h = jnp.maximum(_dot(hn, wu_ref[:, cols]), 0.0)
        acc = acc + _dot((h * h).astype(BF16), wd_ref[cols, :])
    if final_norm:
        acc = _rms_norm(acc, fg_ref[...])
    o_ref[...] = acc


def _merge_ffn(layer, x2d, att, ml, gates, wa, wm, wo, g, wu, wd, final_g, final_norm):
    tm = TOKEN_TILE
    row = lambda width: pl.BlockSpec((tm, width), lambda i: (i, 0))
    return pl.pallas_call(
        functools.partial(_merge_ffn_kernel, final_norm=final_norm),
        grid=(TOKENS // tm,),
        in_specs=[row(D_MODEL), row(ATT_WIDTH), row(MLSTM_WIDTH), row(2 * D_MODEL),
                  _resident((ATT_WIDTH, D_MODEL), layer), _resident((MLSTM_WIDTH, D_MODEL), layer),
                  _resident((D_MODEL, D_MODEL), layer), _resident((1, D_MODEL), layer),
                  _resident((D_MODEL, D_FF), layer), _resident((D_FF, D_MODEL), layer),
                  _resident((1, D_MODEL))],
        out_specs=row(D_MODEL),
        out_shape=jax.ShapeDtypeStruct((TOKENS, D_MODEL), F32),
        compiler_params=pltpu.CompilerParams(
            dimension_semantics=("arbitrary",), vmem_limit_bytes=56 * MIB),
        name="merge_ffn",
    )(x2d, att, ml, gates, wa, wm, wo, g, wu, wd, final_g)


def _rel_bias_rows(rel_bias):
    lead = rel_bias.shape[:-1]
    far = jnp.broadcast_to(rel_bias[..., 2 * MAX_REL_DIST:], lead + (MAX_REL_DIST,))
    near = rel_bias[..., MAX_REL_DIST + 1 - Q_TILE:][..., ::-1]
    rest = jnp.broadcast_to(rel_bias[..., 2 * MAX_REL_DIST:],
                            lead + (REL_ROW - MAX_REL_DIST - near.shape[-1],))
    return jnp.concatenate([far, near, rest], axis=-1).astype(F32) * LOG2E


def kernel(x, mix_norm_g, w_in, conv_w, conv_b, b_igate, b_fgate, rel_bias, mh_norm_g,
           w_att_proj, w_mlstm_proj, w_out, ffn_norm_g, w_up, w_down, final_norm_g):
    a = ATT_WIDTH
    m = MLSTM_WIDTH
    gate_lo = 3 * a + 4 * m
    gate_hi = gate_lo + GATE_ROWS
    w_rows = jnp.concatenate(
        [w_in[:, :, a:2 * a], w_in[:, :, 3 * a:3 * a + 2 * m], w_in[:, :, gate_hi:]],
        axis=2).astype(BF16)
    w_cols = jnp.swapaxes(jnp.concatenate(
        [w_in[:, :, :a], w_in[:, :, 2 * a:3 * a], w_in[:, :, 3 * a + 2 * m:gate_lo]],
        axis=2), 1, 2).astype(BF16)
    w_gate = jnp.swapaxes(w_in[:, :, gate_lo:gate_hi], 1, 2).astype(BF16)
    gate_bias = jnp.tile(
        jnp.broadcast_to(jnp.concatenate([b_igate, b_fgate], axis=1)[:, :, None],
                         (DEPTH, GATE_ROWS, ML_CHUNK)), (1, BATCH, 1))
    norm_g = jnp.broadcast_to(mh_norm_g[:, :, None], (DEPTH, MLSTM_WIDTH, ML_CHUNK))
    rel_rows = _rel_bias_rows(rel_bias)
    mix_g = mix_norm_g.reshape(DEPTH, 1, D_MODEL)
    ffn_g = ffn_norm_g.reshape(DEPTH, 1, D_MODEL)
    conv_bias = conv_b.reshape(DEPTH, 1, 2 * MLSTM_WIDTH)
    final_g = final_norm_g.reshape(1, D_MODEL)
    wa, wm, wo, wu, wd = (w.astype(BF16) for w in (w_att_proj, w_mlstm_proj, w_out, w_up, w_down))

    h = x.reshape(TOKENS, D_MODEL)
    for l in range(DEPTH):
        qt, k_tok, vt_tiles, mqk, mvt, mot, gates, gt = _in_proj(
            l, h, mix_g, w_rows, w_cols, w_gate, conv_w, conv_bias)
        att = _band_attn(l, qt, k_tok, vt_tiles, rel_rows)
        ml = _mlstm(l, mqk.reshape(BATCH, SEQ, 2 * MLSTM_WIDTH), mvt, mot, gt, gate_bias, norm_g)
        h = _merge_ffn(l, h, att.reshape(TOKENS, ATT_WIDTH), ml.reshape(TOKENS, MLSTM_WIDTH),
                       gates, wa, wm, wo, ffn_g, wu, wd, final_g, final_norm=(l == DEPTH - 1))
    return h.reshape(BATCH, SEQ, D_MODEL)
```

```python
import functools
import math

import jax
import jax.numpy as jnp
from jax import lax
from jax.experimental import pallas as pl
from jax.experimental.pallas import tpu as pltpu

D_MODEL = 1024
BATCH = 4
SEQ = 4096
DEPTH = 2
CHUNK = 64
NORM_EPS = 1e-6
ATT_HEADS = 8
ATT_HEAD_DIM = 64
ATT_WIDTH = ATT_HEADS * ATT_HEAD_DIM
LEFT_CHUNKS = 8
BAND_CHUNKS = LEFT_CHUNKS + 1
BAND = BAND_CHUNKS * CHUNK
MAX_REL_DIST = 256
MLSTM_HEADS = 4
MLSTM_HEAD_DIM = 128
MLSTM_WIDTH = MLSTM_HEADS * MLSTM_HEAD_DIM
CONV_WIDTH = 4
D_FF = 4 * D_MODEL

LANES = 128
SUBLANES = 8

TOKENS = BATCH * SEQ
NUM_CHUNKS = SEQ // CHUNK
LEFT_ROWS = LEFT_CHUNKS * CHUNK
IN_Q = 0
IN_K = IN_Q + ATT_WIDTH
IN_V = IN_K + ATT_WIDTH
IN_MQ = IN_V + ATT_WIDTH
IN_MK = IN_MQ + MLSTM_WIDTH
IN_MV = IN_MK + MLSTM_WIDTH
IN_MO = IN_MV + MLSTM_WIDTH
IN_MAIN_COLS = IN_MO + MLSTM_WIDTH
GATE_ROWS = 2 * MLSTM_HEADS
IN_GATES = IN_MAIN_COLS + GATE_ROWS
FEATURE_MAJOR_ROWS = 2 * ATT_WIDTH + 2 * MLSTM_WIDTH

Q_TILE = 2 * CHUNK
WINDOW = BAND + CHUNK
WINDOW_TILES = WINDOW // LANES
ATT_TILES_PER_STEP = 4
ATT_STEPS = SEQ // (Q_TILE * ATT_TILES_PER_STEP)
REL_ROW = 1024
ONES_ROWS = ATT_HEAD_DIM
LOG2E = 1.4426950408889634
MASKED = -1e30

ML_CHUNK = LANES
ML_STEPS = SEQ // ML_CHUNK
ML_UNITS = BATCH * MLSTM_HEADS
CONV_TAIL = 16
CONV_ROWS = 16
CONV_COLS = 128
SPLIT_ROWS = 16

TOKEN_TILE = 512
FF_TILE = 1024

F32 = jnp.float32
BF16 = jnp.bfloat16
MIB = 1024 * 1024


def _resident(shape, layer=None):
    nd = len(shape)
    if layer is None:
        return pl.BlockSpec(shape, lambda *_: (0,) * nd, pipeline_mode=pl.Buffered(1))
    return pl.BlockSpec((None,) + tuple(shape), lambda *_: (layer,) + (0,) * nd,
                        pipeline_mode=pl.Buffered(1))


def _rms_norm(x, g):
    ms = jnp.mean(x * x, axis=-1, keepdims=True)
    return x * lax.rsqrt(ms + NORM_EPS) * g


def _dot(a, b):
    return jnp.dot(a, b, preferred_element_type=F32)


def _dot_nt(a, b):
    return lax.dot_general(a, b, (((1,), (1,)), ((), ())), preferred_element_type=F32)


def _dot_tn(a, b):
    return lax.dot_general(a, b, (((0,), (0,)), ((), ())), preferred_element_type=F32)


def _in_proj_kernel(x_ref, g_ref, w_ref, wgate_ref, wg_ref, cw_ref, cb_ref,
                    wa_ref, wm_ref, wo_ref, wu_ref, wd_ref,
                    qt_ref, k_ref, vt_ref, mqk_ref, mvt_ref, mot_ref, gate_ref, gt_ref,
                    wa_out, wm_out, wo_out, wu_out, wd_out,
                    conv_scr, wt_scr):
    a = ATT_WIDTH
    m = MLSTM_WIDTH

    @pl.when(pl.program_id(0) == 0)
    def _():
        for dst, src in ((0, IN_Q), (a, IN_V), (2 * a, IN_MV), (2 * a + m, IN_MO)):
            for c0 in range(0, a, LANES):
                cols = w_ref[:, src + c0:src + c0 + LANES].astype(F32)
                wt_scr[dst + c0:dst + c0 + LANES, :] = jnp.transpose(cols).astype(BF16)

    @pl.when(pl.program_id(0) % (SEQ // TOKEN_TILE) == 0)
    def _():
        conv_scr[:CONV_TAIL, :] = jnp.zeros((CONV_TAIL, 2 * m), F32)

    for src_ref, dst_ref in ((wa_ref, wa_out), (wm_ref, wm_out), (wo_ref, wo_out),
                             (wu_ref, wu_out), (wd_ref, wd_out)):
        dst_ref[...] = src_ref[...].astype(BF16)

    xn = _rms_norm(x_ref[...], g_ref[...]).astype(BF16)

    def seg(lo, width):
        return _dot(xn, w_ref[:, lo:lo + width])

    def seg_t(lo, width):
        return _dot_nt(wt_scr[lo:lo + width, :], xn)

    conv_scr[CONV_TAIL:, :m] = seg(IN_MQ, m)
    conv_scr[CONV_TAIL:, m:] = seg(IN_MK, m)
    for c0 in range(0, 2 * m, CONV_COLS):
        cols = slice(c0, c0 + CONV_COLS)
        post_scale = 1.0 / math.sqrt(MLSTM_HEAD_DIM) if c0 >= m else None
        for r0 in range(0, TOKEN_TILE, CONV_ROWS):
            lo = CONV_TAIL - SUBLANES + r0
            ext = conv_scr[lo:lo + SUBLANES + CONV_ROWS, cols]
            acc = cb_ref[:, cols] + ext[SUBLANES:] * cw_ref[CONV_WIDTH - 1:CONV_WIDTH, cols]
            for d in range(1, CONV_WIDTH):
                tap = CONV_WIDTH - 1 - d
                acc = acc + pltpu.roll(ext, d, axis=0)[SUBLANES:] * cw_ref[tap:tap + 1, cols]
            half = 0.5 * acc
            act = half + half * jnp.tanh(half)
            if post_scale is not None:
                act = act * post_scale
            mqk_ref[r0:r0 + CONV_ROWS, cols] = act.astype(BF16)
    conv_scr[:CONV_TAIL, :] = conv_scr[TOKEN_TILE:, :]

    qt_ref[...] = (seg_t(0, a) * (LOG2E / math.sqrt(ATT_HEAD_DIM))).astype(BF16)
    vt = seg_t(a, a).astype(BF16)
    for j in range(TOKEN_TILE // LANES):
        vt_ref[0, j] = vt[:, j * LANES:(j + 1) * LANES]
    mvt_ref[0] = seg_t(2 * a, m).astype(BF16)
    mot_ref[0] = seg_t(2 * a + m, m).astype(BF16)
    gt_ref[0] = _dot_nt(wg_ref[...], xn)
    k_ref[0] = seg(IN_K, a).astype(BF16)
    for j in range(2 * D_MODEL // 512):
        cols = slice(j * 512, (j + 1) * 512)
        gate_ref[:, cols] = _dot(xn, wgate_ref[:, cols]).astype(BF16)


def _in_proj(layer, x2d, g, w_main, w_gates, w_gate_t, conv_w, conv_b, wa, wm, wo, wu, wd):
    steps = TOKENS // TOKEN_TILE

    def slab(rows, width):
        return (pl.BlockSpec((None, rows // steps, width), lambda i: (layer, i, 0)),
                pl.BlockSpec((rows // steps, width), lambda i: (i, 0)),
                jax.ShapeDtypeStruct((rows, width), BF16))

    slabs = [slab(ATT_WIDTH, D_MODEL), slab(MLSTM_WIDTH, D_MODEL), slab(D_MODEL, D_MODEL),
             slab(D_MODEL, D_FF), slab(D_FF, D_MODEL)]
    tm = TOKEN_TILE
    tiles_per_batch = SEQ // tm
    lane_tiles = tm // LANES
    row = lambda width: pl.BlockSpec((tm, width), lambda i: (i, 0))
    col = lambda height: pl.BlockSpec((1, height, tm),
                                      lambda i: (i // tiles_per_batch, 0, i % tiles_per_batch))
    k_spec = pl.BlockSpec((1, tm, ATT_WIDTH),
                          lambda i: (i // tiles_per_batch, i % tiles_per_batch, 0))
    vt_spec = pl.BlockSpec((1, lane_tiles, ATT_WIDTH, LANES),
                           lambda i: (i // tiles_per_batch, i % tiles_per_batch, 0, 0))
    out_shape = (
        jax.ShapeDtypeStruct((ATT_WIDTH, TOKENS), BF16),
        jax.ShapeDtypeStruct((BATCH, SEQ, ATT_WIDTH), BF16),
        jax.ShapeDtypeStruct((BATCH, SEQ // LANES, ATT_WIDTH, LANES), BF16),
        jax.ShapeDtypeStruct((TOKENS, 2 * MLSTM_WIDTH), BF16),
        jax.ShapeDtypeStruct((BATCH, MLSTM_WIDTH, SEQ), BF16),
        jax.ShapeDtypeStruct((BATCH, MLSTM_WIDTH, SEQ), BF16),
        jax.ShapeDtypeStruct((TOKENS, 2 * D_MODEL), BF16),
        jax.ShapeDtypeStruct((BATCH, GATE_ROWS, SEQ), F32),
    ) + tuple(s[2] for s in slabs)
    return pl.pallas_call(
        _in_proj_kernel,
        grid=(steps,),
        in_specs=[
            row(D_MODEL),
            _resident((1, D_MODEL), layer),
            _resident((D_MODEL, IN_MAIN_COLS)),
            _resident((D_MODEL, 2 * D_MODEL), layer),
            _resident((GATE_ROWS, D_MODEL), layer),
            _resident((CONV_WIDTH, 2 * MLSTM_WIDTH), layer),
            _resident((1, 2 * MLSTM_WIDTH), layer),
        ] + [s[0] for s in slabs],
        out_specs=(pl.BlockSpec((ATT_WIDTH, tm), lambda i: (0, i)), k_spec, vt_spec,
                   row(2 * MLSTM_WIDTH), col(MLSTM_WIDTH), col(MLSTM_WIDTH), row(2 * D_MODEL),
                   col(GATE_ROWS)) + tuple(s[1] for s in slabs),
        out_shape=out_shape,
        scratch_shapes=[pltpu.VMEM((CONV_TAIL + tm, 2 * MLSTM_WIDTH), F32),
                        pltpu.VMEM((FEATURE_MAJOR_ROWS, D_MODEL), BF16)],
        compiler_params=pltpu.CompilerParams(
            dimension_semantics=("arbitrary",), vmem_limit_bytes=48 * MIB),
        name="in_proj",
    )(x2d, g, w_main, w_gates, w_gate_t, conv_w, conv_b, wa, wm, wo, wu, wd)


def _build_bias_table(rel_ref, bias_scr):
    kb = lax.broadcasted_iota(jnp.int32, (WINDOW, LANES), 0)
    lane = lax.broadcasted_iota(jnp.int32, (WINDOW, LANES), 1)
    band_lo = jnp.where(lane >= CHUNK, CHUNK, 0)
    in_band = (kb >= band_lo) & (kb < band_lo + BAND)
    for h in range(ATT_HEADS):
        rows = jnp.broadcast_to(rel_ref[h:h + 1, :], (LANES, REL_ROW))
        shifted = pltpu.roll(rows, 0, axis=1, stride=1, stride_axis=0)
        table = jnp.concatenate(
            [jnp.transpose(shifted[:, j * LANES:(j + 1) * LANES]) for j in range(WINDOW_TILES)],
            axis=0)
        bias_scr[h, :WINDOW, :] = jnp.where(in_band, table, MASKED)
        bias_scr[h, WINDOW:, :] = jnp.full((LEFT_ROWS, LANES), MASKED, F32)


def _band_attn_kernel(qt_ref, k_ref, vt_ref, rel_ref, o_ref, bias_scr):
    step = pl.program_id(1)

    @pl.when((pl.program_id(0) == 0) & (step == 0))
    def _():
        _build_bias_table(rel_ref, bias_scr)

    d = ATT_HEAD_DIM
    zeros_half = jnp.zeros((d, Q_TILE), BF16)
    ones_rows = jnp.ones((ONES_ROWS, LANES), BF16)
    for sub in range(ATT_TILES_PER_STEP):
        tile = step * ATT_TILES_PER_STEP + sub
        q_lanes = slice(sub * Q_TILE, (sub + 1) * Q_TILE)
        first_tile = jnp.maximum(tile - LEFT_ROWS // Q_TILE, 0)
        start = pl.multiple_of(first_tile * Q_TILE, Q_TILE)
        shift = pl.multiple_of(LEFT_ROWS - (tile - first_tile) * Q_TILE, Q_TILE)
        outs = []
        for h in range(ATT_HEADS):
            pair_dims = slice((h // 2) * 2 * d, (h // 2 + 1) * 2 * d)
            qh = qt_ref[h * d:(h + 1) * d, q_lanes]
            qm = jnp.concatenate([qh, zeros_half] if h % 2 == 0 else [zeros_half, qh], axis=0)
            tile_max, tile_out = [], []
            for j in range(WINDOW_TILES):
                keys = k_ref[0, pl.ds(start + j * LANES, LANES), pair_dims]
                st = _dot(keys, qm) + bias_scr[h, pl.ds(shift + j * LANES, LANES), :]
                m_j = jnp.max(st, axis=0, keepdims=True)
                p_j = jnp.exp2(st - m_j).astype(BF16)
                v_ones = jnp.concatenate(
                    [vt_ref[0, first_tile + j, h * d:(h + 1) * d, :], ones_rows], axis=0)
                tile_max.append(m_j)
                tile_out.append(_dot(v_ones, p_j))
            m = functools.reduce(jnp.maximum, tile_max)
            acc = functools.reduce(jnp.add, [jnp.exp2(m_j - m) * o
                                             for m_j, o in zip(tile_max, tile_out)])
            outs.append(acc[:d] / acc[d:d + 1])
        o_ref[0, q_lanes, :] = jnp.transpose(jnp.concatenate(outs, axis=0)).astype(BF16)


def _band_attn(layer, qt, k_tok, vt_tiles, rel_rows):
    return pl.pallas_call(
        _band_attn_kernel,
        grid=(BATCH, ATT_STEPS),
        in_specs=[
            pl.BlockSpec((ATT_WIDTH, ATT_TILES_PER_STEP * Q_TILE),
                         lambda b, s: (0, b * ATT_STEPS + s)),
            pl.BlockSpec((1, SEQ, ATT_WIDTH), lambda b, s: (b, 0, 0)),
            pl.BlockSpec((1, SEQ // LANES, ATT_WIDTH, LANES), lambda b, s: (b, 0, 0, 0)),
            _resident((ATT_HEADS, REL_ROW), layer),
        ],
        out_specs=pl.BlockSpec((1, ATT_TILES_PER_STEP * Q_TILE, ATT_WIDTH),
                               lambda b, s: (b, s, 0)),
        out_shape=jax.ShapeDtypeStruct((BATCH, SEQ, ATT_WIDTH), BF16),
        scratch_shapes=[pltpu.VMEM((ATT_HEADS, WINDOW + LEFT_ROWS, LANES), F32)],
        compiler_params=pltpu.CompilerParams(
            dimension_semantics=("arbitrary", "arbitrary"), vmem_limit_bytes=40 * MIB),
        name="band_attn",
    )(qt, k_tok, vt_tiles, rel_rows)


def _scan_lanes(x, op, identity, segment):
    pos = lax.broadcasted_iota(jnp.int32, x.shape, 1) % segment
    d = 1
    while d < segment:
        x = op(x, jnp.where(pos >= d, pltpu.roll(x, d, axis=1), identity))
        d *= 2
    return x


def _split3(x):
    hi = x.astype(BF16).astype(F32)
    r1 = x - hi
    mid = r1.astype(BF16).astype(F32)
    lo = (r1 - mid).astype(BF16).astype(F32)
    return hi, mid, lo


GP_BCUM, GP_STAB, GP_W, GP_BLAST, GP_MLOC, GP_R_HI, GP_R_MID, GP_R_LO = range(8)
GP_PLANES = 8


def _gate_planes(gt_ref, gb_ref, planes_ref):
    rows_all = BATCH * GATE_ROWS
    gates = jnp.concatenate([gt_ref[b] for b in range(BATCH)], axis=0) + gb_ref[...]
    logf = jnp.minimum(gates, 0.0) - jnp.log1p(jnp.exp(-jnp.abs(gates)))
    bcum = pltpu.roll(_scan_lanes(logf, jnp.add, 0.0, ML_CHUNK), rows_all - MLSTM_HEADS, axis=0)
    b_last = jnp.broadcast_to(bcum[:, ML_CHUNK - 1:ML_CHUNK], (rows_all, ML_CHUNK))
    a_row = b_last - bcum + gates
    m_loc = jnp.broadcast_to(jnp.max(a_row, axis=1, keepdims=True), (rows_all, ML_CHUNK))
    r_row = gates - bcum
    planes_ref[GP_BCUM] = bcum
    planes_ref[GP_STAB] = bcum + _scan_lanes(r_row, jnp.maximum, -jnp.inf, ML_CHUNK)
    planes_ref[GP_W] = jnp.exp(a_row - m_loc)
    planes_ref[GP_BLAST] = b_last
    planes_ref[GP_MLOC] = m_loc
    for plane, part in zip((GP_R_HI, GP_R_MID, GP_R_LO), _split3(r_row)):
        planes_ref[plane] = part


def _mlstm_kernel(mqk_ref, vt_ref, mot_ref, gt_ref, gt_next_ref, gb_ref, ng_ref, o_ref,
                  c_scr, n_scr, m_scr, gp_scr):
    step = pl.program_id(0)
    hd = MLSTM_HEAD_DIM
    gp_ref = gp_scr

    @pl.when(step == 0)
    def _():
        c_scr[...] = jnp.zeros_like(c_scr)
        n_scr[...] = jnp.zeros_like(n_scr)
        m_scr[...] = jnp.zeros_like(m_scr)
        _gate_planes(gt_ref, gb_ref, gp_scr)

    bcum = gp_ref[GP_BCUM]
    b_last = gp_ref[GP_BLAST]
    m_loc = gp_ref[GP_MLOC]
    w_row = gp_ref[GP_W]
    m_prev = m_scr[...]
    m_new = jnp.maximum(b_last + m_prev, m_loc)
    s_prev = jnp.exp(b_last + m_prev - m_new)
    s_loc = jnp.exp(m_loc - m_new)
    m_scr[...] = m_new
    g_row = bcum + m_prev
    m_t = jnp.maximum(g_row, gp_ref[GP_STAB])
    inter = jnp.exp(g_row - m_t)
    inv_floor = jnp.exp(-m_t)
    n_prev = n_scr[...]
    w_bf = w_row.astype(BF16)
    n_prev_bf = n_prev.astype(BF16)
    r_parts = (gp_ref[GP_R_HI], gp_ref[GP_R_MID], gp_ref[GP_R_LO])
    e_parts = _split3(bcum - m_t)
    split_row = lax.broadcasted_iota(jnp.int32, (SPLIT_ROWS, ML_CHUNK), 0)

    def outer_sum_operand(parts, one, first):
        other = 3 - first
        out = jnp.where((split_row >= other) & (split_row < other + 3), 1.0, 0.0)
        for i, part in enumerate(parts):
            out = jnp.where(split_row == first + i, part[one], out)
        return out.astype(BF16)

    src = lax.broadcasted_iota(jnp.int32, (ML_CHUNK, ML_CHUNK), 0)
    qry = lax.broadcasted_iota(jnp.int32, (ML_CHUNK, ML_CHUNK), 1)
    causal = src <= qry
    new_n = []

    for b in range(BATCH):
        grp = slice(b * GATE_ROWS, (b + 1) * GATE_ROWS)
        outs = []
        for h in range(MLSTM_HEADS):
            u = b * MLSTM_HEADS + h
            r = b * GATE_ROWS + h
            one = slice(r, r + 1)
            sl = slice(h * hd, (h + 1) * hd)
            qh = mqk_ref[b, :, sl]
            kh = mqk_ref[b, :, MLSTM_WIDTH + h * hd:MLSTM_WIDTH + (h + 1) * hd]
            vt = vt_ref[b, sl, :]

            c_prev = c_scr[u]
            c_loc = _dot((vt.astype(F32) * w_row[one]).astype(BF16), kh)
            n_loc = _dot(w_bf[grp], kh)[h:h + 1]
            c_scr[u] = s_prev[one] * c_prev + s_loc[one] * c_loc
            new_n.append(s_prev[one] * n_prev[one] + s_loc[one] * n_loc)
            if h == MLSTM_HEADS - 1:
                new_n.append(jnp.zeros((GATE_ROWS - MLSTM_HEADS, hd), F32))

            lhs = outer_sum_operand(r_parts, one, 0)
            rhs = outer_sum_operand(e_parts, one, 3)
            decay = jnp.exp(jnp.where(causal, _dot_tn(lhs, rhs), -jnp.inf))
            pt = _dot_nt(kh, qh) * decay
            nq = _dot_nt(n_prev_bf[grp], qh)[h:h + 1]
            den = inter[one] * nq + jnp.sum(pt, axis=0, keepdims=True)
            num = inter[one] * _dot_nt(c_prev.astype(BF16), qh) + _dot(vt, pt.astype(BF16))
            hout = num * (1.0 / jnp.maximum(jnp.abs(den), inv_floor[one]))
            ms = jnp.mean(hout * hout, axis=0, keepdims=True)
            y = hout * lax.rsqrt(ms + NORM_EPS) * ng_ref[sl, :]
            outs.append(jax.nn.sigmoid(mot_ref[b, sl, :].astype(F32)) * y)
        o_ref[b] = jnp.transpose(jnp.concatenate(outs, axis=0)).astype(BF16)

    n_scr[...] = jnp.concatenate(new_n, axis=0)
    _gate_planes(gt_next_ref, gb_ref, gp_scr)


def _mlstm(layer, mqk, mvt, mot, gt, gate_bias, norm_g):
    tok = lambda width: pl.BlockSpec((BATCH, ML_CHUNK, width), lambda c: (0, c, 0))
    feat = lambda height: pl.BlockSpec((BATCH, height, ML_CHUNK), lambda c: (0, 0, c))
    rows_all = BATCH * GATE_ROWS
    return pl.pallas_call(
        _mlstm_kernel,
        grid=(ML_STEPS,),
        in_specs=[
            tok(2 * MLSTM_WIDTH), feat(MLSTM_WIDTH), feat(MLSTM_WIDTH), feat(GATE_ROWS),
            pl.BlockSpec((BATCH, GATE_ROWS, ML_CHUNK),
                         lambda c: (0, 0, jnp.minimum(c + 1, ML_STEPS - 1))),
            _resident((rows_all, ML_CHUNK), layer),
            _resident((MLSTM_WIDTH, ML_CHUNK), layer),
        ],
        out_specs=tok(MLSTM_WIDTH),
        out_shape=jax.ShapeDtypeStruct((BATCH, SEQ, MLSTM_WIDTH), BF16),
        scratch_shapes=[
            pltpu.VMEM((ML_UNITS, MLSTM_HEAD_DIM, MLSTM_HEAD_DIM), F32),
            pltpu.VMEM((rows_all, MLSTM_HEAD_DIM), F32),
            pltpu.VMEM((rows_all, ML_CHUNK), F32),
            pltpu.VMEM((GP_PLANES, rows_all, ML_CHUNK), F32),
        ],
        compiler_params=pltpu.CompilerParams(
            dimension_semantics=("arbitrary",), vmem_limit_bytes=32 * MIB),
        name="mlstm",
    )(mqk, mvt, mot, gt, gt, gate_bias, norm_g)


def _merge_ffn_kernel(x_ref, att_ref, ml_ref, gate_ref, wa_ref, wm_ref, wo_ref,
                      g_ref, wu_ref, wd_ref, fg_ref, *rest, final_norm):
    if final_norm:
        (o_ref,) = rest
    else:
        w_next_ref, o_ref, w_next_out = rest
        w_next_out[...] = w_next_ref[...].astype(BF16)
    ga = gate_ref[:, :D_MODEL].astype(F32)
    gm = gate_ref[:, D_MODEL:].astype(F32)
    y = (jax.nn.sigmoid(ga) * _dot(att_ref[...], wa_ref[...])
         + jax.nn.sigmoid(gm) * _dot(ml_ref[...], wm_ref[...]))
    x = x_ref[...] + _dot(y.astype(BF16), wo_ref[...])
    hn = _rms_norm(x, g_ref[...]).astype(BF16)
    acc = x
    for j in range(D_FF // FF_TILE):
        cols = slice(j * FF_TILE, (j + 1) * FF_TILE)
        h = jnp.maximum(_dot(hn, wu_ref[:, cols]), 0.0)
        acc = acc + _dot((h * h).astype(BF16), wd_ref[cols, :])
    if final_norm:
        acc = _rms_norm(acc, fg_ref[...])
    o_ref[...] = acc


def _merge_ffn(layer, x2d, att, ml, gates, wa, wm, wo, g, wu, wd, final_g, w_in):
    final_norm = layer == DEPTH - 1
    tm = TOKEN_TILE
    steps = TOKENS // tm
    row = lambda width: pl.BlockSpec((tm, width), lambda i: (i, 0))
    in_specs = [row(D_MODEL), row(ATT_WIDTH), row(MLSTM_WIDTH), row(2 * D_MODEL),
                _resident((ATT_WIDTH, D_MODEL)), _resident((MLSTM_WIDTH, D_MODEL)),
                _resident((D_MODEL, D_MODEL)), _resident((1, D_MODEL), layer),
                _resident((D_MODEL, D_FF)), _resident((D_FF, D_MODEL)),
                _resident((1, D_MODEL))]
    args = [x2d, att, ml, gates, wa, wm, wo, g, wu, wd, final_g]
    out_specs = row(D_MODEL)
    out_shape = jax.ShapeDtypeStruct((TOKENS, D_MODEL), F32)
    if not final_norm:
        in_specs.append(pl.BlockSpec((None, D_MODEL // steps, IN_MAIN_COLS),
                                     lambda i: (layer + 1, i, 0)))
        args.append(w_in)
        out_specs = (out_specs, pl.BlockSpec((D_MODEL // steps, IN_MAIN_COLS), lambda i: (i, 0)))
        out_shape = (out_shape, jax.ShapeDtypeStruct((D_MODEL, IN_MAIN_COLS), BF16))
    return pl.pallas_call(
        functools.partial(_merge_ffn_kernel, final_norm=final_norm),
        grid=(steps,),
        in_specs=in_specs,
        out_specs=out_specs,
        out_shape=out_shape,
        compiler_params=pltpu.CompilerParams(
            dimension_semantics=("arbitrary",), vmem_limit_bytes=56 * MIB),
        name="merge_ffn",
    )(*args)


def _rel_bias_rows(rel_bias):
    lead = rel_bias.shape[:-1]
    far = jnp.broadcast_to(rel_bias[..., 2 * MAX_REL_DIST:], lead + (MAX_REL_DIST,))
    near = rel_bias[..., MAX_REL_DIST + 1 - Q_TILE:][..., ::-1]
    rest = jnp.broadcast_to(rel_bias[..., 2 * MAX_REL_DIST:],
                            lead + (REL_ROW - MAX_REL_DIST - near.shape[-1],))
    return jnp.concatenate([far, near, rest], axis=-1).astype(F32) * LOG2E


def kernel(x, mix_norm_g, w_in, conv_w, conv_b, b_igate, b_fgate, rel_bias, mh_norm_g,
           w_att_proj, w_mlstm_proj, w_out, ffn_norm_g, w_up, w_down, final_norm_g):
    w_main = w_in[0, :, :IN_MAIN_COLS].astype(BF16)
    w_gates = w_in[:, :, IN_GATES:].astype(BF16)
    w_gate_t = jnp.swapaxes(w_in[:, :, IN_MAIN_COLS:IN_GATES], 1, 2).astype(BF16)
    gate_bias = jnp.tile(
        jnp.broadcast_to(jnp.concatenate([b_igate, b_fgate], axis=1)[:, :, None],
                         (DEPTH, GATE_ROWS, ML_CHUNK)), (1, BATCH, 1))
    norm_g = jnp.broadcast_to(mh_norm_g[:, :, None], (DEPTH, MLSTM_WIDTH, ML_CHUNK))
    rel_rows = _rel_bias_rows(rel_bias)
    mix_g = mix_norm_g.reshape(DEPTH, 1, D_MODEL)
    ffn_g = ffn_norm_g.reshape(DEPTH, 1, D_MODEL)
    conv_bias = conv_b.reshape(DEPTH, 1, 2 * MLSTM_WIDTH)
    final_g = final_norm_g.reshape(1, D_MODEL)

    h = x.reshape(TOKENS, D_MODEL)
    for l in range(DEPTH):
        qt, k_tok, vt_tiles, mqk, mvt, mot, gates, gt, wa, wm, wo, wu, wd = _in_proj(
            l, h, mix_g, w_main, w_gates, w_gate_t, conv_w, conv_bias,
            w_att_proj, w_mlstm_proj, w_out, w_up, w_down)
        att = _band_attn(l, qt, k_tok, vt_tiles, rel_rows)
        ml = _mlstm(l, mqk.reshape(BATCH, SEQ, 2 * MLSTM_WIDTH), mvt, mot, gt, gate_bias, norm_g)
        out = _merge_ffn(l, h, att.reshape(TOKENS, ATT_WIDTH), ml.reshape(TOKENS, MLSTM_WIDTH),
                         gates, wa, wm, wo, ffn_g, wu, wd, final_g, w_in)
        h, w_main = out if l < DEPTH - 1 else (out, None)
    return h.reshape(BATCH, SEQ, D_MODEL)
```

```python
import functools
import math

import jax
import jax.numpy as jnp
from jax import lax
from jax.experimental import pallas as pl
from jax.experimental.pallas import tpu as pltpu

D_MODEL = 1024
BATCH = 4
SEQ = 4096
DEPTH = 2
CHUNK = 64
NORM_EPS = 1e-6
ATT_HEADS = 8
ATT_HEAD_DIM = 64
ATT_WIDTH = ATT_HEADS * ATT_HEAD_DIM
LEFT_CHUNKS = 8
BAND_CHUNKS = LEFT_CHUNKS + 1
BAND = BAND_CHUNKS * CHUNK
MAX_REL_DIST = 256
MLSTM_HEADS = 4
MLSTM_HEAD_DIM = 128
MLSTM_WIDTH = MLSTM_HEADS * MLSTM_HEAD_DIM
CONV_WIDTH = 4
D_FF = 4 * D_MODEL

LANES = 128
SUBLANES = 8

TOKENS = BATCH * SEQ
NUM_CHUNKS = SEQ // CHUNK
LEFT_ROWS = LEFT_CHUNKS * CHUNK
IN_Q = 0
IN_K = IN_Q + ATT_WIDTH
IN_V = IN_K + ATT_WIDTH
IN_MQ = IN_V + ATT_WIDTH
IN_MK = IN_MQ + MLSTM_WIDTH
IN_MV = IN_MK + MLSTM_WIDTH
IN_MO = IN_MV + MLSTM_WIDTH
IN_MAIN_COLS = IN_MO + MLSTM_WIDTH
GATE_ROWS = 2 * MLSTM_HEADS
IN_GATES = IN_MAIN_COLS + GATE_ROWS
FEATURE_MAJOR_ROWS = 2 * ATT_WIDTH + 2 * MLSTM_WIDTH

Q_TILE = 2 * CHUNK
WINDOW = BAND + CHUNK
WINDOW_TILES = WINDOW // LANES
ATT_TILES_PER_STEP = 4
ATT_STEPS = SEQ // (Q_TILE * ATT_TILES_PER_STEP)
REL_ROW = 1024
ONES_ROWS = ATT_HEAD_DIM
LOG2E = 1.4426950408889634
MASKED = -1e30

ML_CHUNK = LANES
ML_STEPS = SEQ // ML_CHUNK
ML_UNITS = BATCH * MLSTM_HEADS
CONV_TAIL = 16
CONV_ROWS = 16
CONV_COLS = 128
SPLIT_ROWS = 16

TOKEN_TILE = 512
FF_TILE = 1024

F32 = jnp.float32
BF16 = jnp.bfloat16
MIB = 1024 * 1024


def _resident(shape, layer=None):
    nd = len(shape)
    if layer is None:
        return pl.BlockSpec(shape, lambda *_: (0,) * nd, pipeline_mode=pl.Buffered(1))
    return pl.BlockSpec((None,) + tuple(shape), lambda *_: (layer,) + (0,) * nd,
                        pipeline_mode=pl.Buffered(1))


def _rms_norm(x, g):
    ms = jnp.mean(x * x, axis=-1, keepdims=True)
    return x * lax.rsqrt(ms + NORM_EPS) * g


def _dot(a, b):
    return jnp.dot(a, b, preferred_element_type=F32)


def _dot_nt(a, b):
    return lax.dot_general(a, b, (((1,), (1,)), ((), ())), preferred_element_type=F32)


def _dot_tn(a, b):
    return lax.dot_general(a, b, (((0,), (0,)), ((), ())), preferred_element_type=F32)


def _in_proj_kernel(x_ref, g_ref, w_ref, wgate_ref, wg_ref, cw_ref, cb_ref,
                    wa_ref, wm_ref, wo_ref, wu_ref, wd_ref,
                    qt_ref, k_ref, vt_ref, mqk_ref, mvt_ref, mot_ref, gate_ref, gt_ref,
                    wa_out, wm_out, wo_out, wu_out, wd_out,
                    conv_scr):
    a = ATT_WIDTH
    m = MLSTM_WIDTH

    @pl.when(pl.program_id(0) % (SEQ // TOKEN_TILE) == 0)
    def _():
        conv_scr[:CONV_TAIL, :] = jnp.zeros((CONV_TAIL, 2 * m), F32)

    for src_ref, dst_ref in ((wa_ref, wa_out), (wm_ref, wm_out), (wo_ref, wo_out),
                             (wu_ref, wu_out), (wd_ref, wd_out)):
        dst_ref[...] = src_ref[...].astype(BF16)

    xn = _rms_norm(x_ref[...], g_ref[...]).astype(BF16)

    def seg(lo, width):
        return _dot_nt(xn, w_ref[lo:lo + width, :])

    def seg_t(lo, width):
        return _dot_nt(w_ref[lo:lo + width, :], xn)

    conv_scr[CONV_TAIL:, :m] = seg(IN_MQ, m)
    conv_scr[CONV_TAIL:, m:] = seg(IN_MK, m)
    for c0 in range(0, 2 * m, CONV_COLS):
        cols = slice(c0, c0 + CONV_COLS)
        post_scale = 1.0 / math.sqrt(MLSTM_HEAD_DIM) if c0 >= m else None
        for r0 in range(0, TOKEN_TILE, CONV_ROWS):
            lo = CONV_TAIL - SUBLANES + r0
            ext = conv_scr[lo:lo + SUBLANES + CONV_ROWS, cols]
            acc = cb_ref[:, cols] + ext[SUBLANES:] * cw_ref[CONV_WIDTH - 1:CONV_WIDTH, cols]
            for d in range(1, CONV_WIDTH):
                tap = CONV_WIDTH - 1 - d
                acc = acc + pltpu.roll(ext, d, axis=0)[SUBLANES:] * cw_ref[tap:tap + 1, cols]
            half = 0.5 * acc
            act = half + half * jnp.tanh(half)
            if post_scale is not None:
                act = act * post_scale
            mqk_ref[r0:r0 + CONV_ROWS, cols] = act.astype(BF16)
    conv_scr[:CONV_TAIL, :] = conv_scr[TOKEN_TILE:, :]

    qt_ref[...] = (seg_t(IN_Q, a) * (LOG2E / math.sqrt(ATT_HEAD_DIM))).astype(BF16)
    vt = seg_t(IN_V, a).astype(BF16)
    for j in range(TOKEN_TILE // LANES):
        vt_ref[0, j] = vt[:, j * LANES:(j + 1) * LANES]
    mvt_ref[0] = seg_t(IN_MV, m).astype(BF16)
    mot_ref[0] = seg_t(IN_MO, m).astype(BF16)
    gt_ref[0] = _dot_nt(wg_ref[...], xn)
    k_ref[0] = seg(IN_K, a).astype(BF16)
    for j in range(2 * D_MODEL // 512):
        cols = slice(j * 512, (j + 1) * 512)
        gate_ref[:, cols] = _dot_nt(xn, wgate_ref[cols, :]).astype(BF16)


def _in_proj(layer, x2d, g, w_main, w_gates, w_gate_t, conv_w, conv_b, wa, wm, wo, wu, wd):
    steps = TOKENS // TOKEN_TILE

    def slab(rows, width):
        return (pl.BlockSpec((None, rows // steps, width), lambda i: (layer, i, 0)),
                pl.BlockSpec((rows // steps, width), lambda i: (i, 0)),
                jax.ShapeDtypeStruct((rows, width), BF16))

    slabs = [slab(ATT_WIDTH, D_MODEL), slab(MLSTM_WIDTH, D_MODEL), slab(D_MODEL, D_MODEL),
             slab(D_MODEL, D_FF), slab(D_FF, D_MODEL)]
    tm = TOKEN_TILE
    tiles_per_batch = SEQ // tm
    lane_tiles = tm // LANES
    row = lambda width: pl.BlockSpec((tm, width), lambda i: (i, 0))
    col = lambda height: pl.BlockSpec((1, height, tm),
                                      lambda i: (i // tiles_per_batch, 0, i % tiles_per_batch))
    k_spec = pl.BlockSpec((1, tm, ATT_WIDTH),
                          lambda i: (i // tiles_per_batch, i % tiles_per_batch, 0))
    vt_spec = pl.BlockSpec((1, lane_tiles, ATT_WIDTH, LANES),
                           lambda i: (i // tiles_per_batch, i % tiles_per_batch, 0, 0))
    out_shape = (
        jax.ShapeDtypeStruct((ATT_WIDTH, TOKENS), BF16),
        jax.ShapeDtypeStruct((BATCH, SEQ, ATT_WIDTH), BF16),
        jax.ShapeDtypeStruct((BATCH, SEQ // LANES, ATT_WIDTH, LANES), BF16),
        jax.ShapeDtypeStruct((TOKENS, 2 * MLSTM_WIDTH), BF16),
        jax.ShapeDtypeStruct((BATCH, MLSTM_WIDTH, SEQ), BF16),
        jax.ShapeDtypeStruct((BATCH, MLSTM_WIDTH, SEQ), BF16),
        jax.ShapeDtypeStruct((TOKENS, 2 * D_MODEL), BF16),
        jax.ShapeDtypeStruct((BATCH, GATE_ROWS, SEQ), F32),
    ) + tuple(s[2] for s in slabs)
    return pl.pallas_call(
        _in_proj_kernel,
        grid=(steps,),
        in_specs=[
            row(D_MODEL),
            _resident((1, D_MODEL), layer),
            _resident((IN_MAIN_COLS, D_MODEL), layer),
            _resident((2 * D_MODEL, D_MODEL), layer),
            _resident((GATE_ROWS, D_MODEL), layer),
            _resident((CONV_WIDTH, 2 * MLSTM_WIDTH), layer),
            _resident((1, 2 * MLSTM_WIDTH), layer),
        ] + [s[0] for s in slabs],
        out_specs=(pl.BlockSpec((ATT_WIDTH, tm), lambda i: (0, i)), k_spec, vt_spec,
                   row(2 * MLSTM_WIDTH), col(MLSTM_WIDTH), col(MLSTM_WIDTH), row(2 * D_MODEL),
                   col(GATE_ROWS)) + tuple(s[1] for s in slabs),
        out_shape=out_shape,
        scratch_shapes=[pltpu.VMEM((CONV_TAIL + tm, 2 * MLSTM_WIDTH), F32)],
        compiler_params=pltpu.CompilerParams(
            dimension_semantics=("arbitrary",), vmem_limit_bytes=48 * MIB),
        name="in_proj",
    )(x2d, g, w_main, w_gates, w_gate_t, conv_w, conv_b, wa, wm, wo, wu, wd)


def _build_bias_table(rel_ref, bias_scr):
    kb = lax.broadcasted_iota(jnp.int32, (WINDOW, LANES), 0)
    lane = lax.broadcasted_iota(jnp.int32, (WINDOW, LANES), 1)
    band_lo = jnp.where(lane >= CHUNK, CHUNK, 0)
    in_band = (kb >= band_lo) & (kb < band_lo + BAND)
    for h in range(ATT_HEADS):
        rows = jnp.broadcast_to(rel_ref[h:h + 1, :], (LANES, REL_ROW))
        shifted = pltpu.roll(rows, 0, axis=1, stride=1, stride_axis=0)
        table = jnp.concatenate(
            [jnp.transpose(shifted[:, j * LANES:(j + 1) * LANES]) for j in range(WINDOW_TILES)],
            axis=0)
        bias_scr[h, :WINDOW, :] = jnp.where(in_band, table, MASKED)
        bias_scr[h, WINDOW:, :] = jnp.full((LEFT_ROWS, LANES), MASKED, F32)


def _band_attn_kernel(qt_ref, k_ref, vt_ref, rel_ref, o_ref, bias_scr):
    step = pl.program_id(1)

    @pl.when((pl.program_id(0) == 0) & (step == 0))
    def _():
        _build_bias_table(rel_ref, bias_scr)

    d = ATT_HEAD_DIM
    zeros_half = jnp.zeros((d, Q_TILE), BF16)
    ones_rows = jnp.ones((ONES_ROWS, LANES), BF16)
    for sub in range(ATT_TILES_PER_STEP):
        tile = step * ATT_TILES_PER_STEP + sub
        q_lanes = slice(sub * Q_TILE, (sub + 1) * Q_TILE)
        first_tile = jnp.maximum(tile - LEFT_ROWS // Q_TILE, 0)
        start = pl.multiple_of(first_tile * Q_TILE, Q_TILE)
        shift = pl.multiple_of(LEFT_ROWS - (tile - first_tile) * Q_TILE, Q_TILE)
        outs = []
        for h in range(ATT_HEADS):
            pair_dims = slice((h // 2) * 2 * d, (h // 2 + 1) * 2 * d)
            qh = qt_ref[h * d:(h + 1) * d, q_lanes]
            qm = jnp.concatenate([qh, zeros_half] if h % 2 == 0 else [zeros_half, qh], axis=0)
            tile_max, tile_out = [], []
            for j in range(WINDOW_TILES):
                keys = k_ref[0, pl.ds(start + j * LANES, LANES), pair_dims]
                st = _dot(keys, qm) + bias_scr[h, pl.ds(shift + j * LANES, LANES), :]
                m_j = jnp.max(st, axis=0, keepdims=True)
                p_j = jnp.exp2(st - m_j).astype(BF16)
                v_ones = jnp.concatenate(
                    [vt_ref[0, first_tile + j, h * d:(h + 1) * d, :], ones_rows], axis=0)
                tile_max.append(m_j)
                tile_out.append(_dot(v_ones, p_j))
            m = functools.reduce(jnp.maximum, tile_max)
            acc = functools.reduce(jnp.add, [jnp.exp2(m_j - m) * o
                                             for m_j, o in zip(tile_max, tile_out)])
            outs.append(acc[:d] / acc[d:d + 1])
        o_ref[0, q_lanes, :] = jnp.transpose(jnp.concatenate(outs, axis=0)).astype(BF16)


def _band_attn(layer, qt, k_tok, vt_tiles, rel_rows):
    return pl.pallas_call(
        _band_attn_kernel,
        grid=(BATCH, ATT_STEPS),
        in_specs=[
            pl.BlockSpec((ATT_WIDTH, ATT_TILES_PER_STEP * Q_TILE),
                         lambda b, s: (0, b * ATT_STEPS + s)),
            pl.BlockSpec((1, SEQ, ATT_WIDTH), lambda b, s: (b, 0, 0)),
            pl.BlockSpec((1, SEQ // LANES, ATT_WIDTH, LANES), lambda b, s: (b, 0, 0, 0)),
            _resident((ATT_HEADS, REL_ROW), layer),
        ],
        out_specs=pl.BlockSpec((1, ATT_TILES_PER_STEP * Q_TILE, ATT_WIDTH),
                               lambda b, s: (b, s, 0)),
        out_shape=jax.ShapeDtypeStruct((BATCH, SEQ, ATT_WIDTH), BF16),
        scratch_shapes=[pltpu.VMEM((ATT_HEADS, WINDOW + LEFT_ROWS, LANES), F32)],
        compiler_params=pltpu.CompilerParams(
            dimension_semantics=("arbitrary", "arbitrary"), vmem_limit_bytes=40 * MIB),
        name="band_attn",
    )(qt, k_tok, vt_tiles, rel_rows)


def _scan_lanes(x, op, identity, segment):
    pos = lax.broadcasted_iota(jnp.int32, x.shape, 1) % segment
    d = 1
    while d < segment:
        x = op(x, jnp.where(pos >= d, pltpu.roll(x, d, axis=1), identity))
        d *= 2
    return x


def _split3(x):
    hi = x.astype(BF16).astype(F32)
    r1 = x - hi
    mid = r1.astype(BF16).astype(F32)
    lo = (r1 - mid).astype(BF16).astype(F32)
    return hi, mid, lo


GP_BCUM, GP_STAB, GP_W, GP_BLAST, GP_MLOC, GP_R_HI, GP_R_MID, GP_R_LO = range(8)
GP_PLANES = 8


def _gate_planes(gt_ref, gb_ref, planes_ref):
    rows_all = BATCH * GATE_ROWS
    gates = jnp.concatenate([gt_ref[b] for b in range(BATCH)], axis=0) + gb_ref[...]
    logf = jnp.minimum(gates, 0.0) - jnp.log1p(jnp.exp(-jnp.abs(gates)))
    bcum = pltpu.roll(_scan_lanes(logf, jnp.add, 0.0, ML_CHUNK), rows_all - MLSTM_HEADS, axis=0)
    b_last = jnp.broadcast_to(bcum[:, ML_CHUNK - 1:ML_CHUNK], (rows_all, ML_CHUNK))
    a_row = b_last - bcum + gates
    m_loc = jnp.broadcast_to(jnp.max(a_row, axis=1, keepdims=True), (rows_all, ML_CHUNK))
    r_row = gates - bcum
    planes_ref[GP_BCUM] = bcum
    planes_ref[GP_STAB] = bcum + _scan_lanes(r_row, jnp.maximum, -jnp.inf, ML_CHUNK)
    planes_ref[GP_W] = jnp.exp(a_row - m_loc)
    planes_ref[GP_BLAST] = b_last
    planes_ref[GP_MLOC] = m_loc
    for plane, part in zip((GP_R_HI, GP_R_MID, GP_R_LO), _split3(r_row)):
        planes_ref[plane] = part


def _mlstm_kernel(mqk_ref, vt_ref, mot_ref, gt_ref, gt_next_ref, gb_ref, ng_ref, o_ref,
                  c_scr, n_scr, m_scr, gp_scr):
    step = pl.program_id(0)
    hd = MLSTM_HEAD_DIM
    gp_ref = gp_scr

    @pl.when(step == 0)
    def _():
        c_scr[...] = jnp.zeros_like(c_scr)
        n_scr[...] = jnp.zeros_like(n_scr)
        m_scr[...] = jnp.zeros_like(m_scr)
        _gate_planes(gt_ref, gb_ref, gp_scr)

    bcum = gp_ref[GP_BCUM]
    b_last = gp_ref[GP_BLAST]
    m_loc = gp_ref[GP_MLOC]
    w_row = gp_ref[GP_W]
    m_prev = m_scr[...]
    m_new = jnp.maximum(b_last + m_prev, m_loc)
    s_prev = jnp.exp(b_last + m_prev - m_new)
    s_loc = jnp.exp(m_loc - m_new)
    m_scr[...] = m_new
    g_row = bcum + m_prev
    m_t = jnp.maximum(g_row, gp_ref[GP_STAB])
    inter = jnp.exp(g_row - m_t)
    inv_floor = jnp.exp(-m_t)
    n_prev = n_scr[...]
    w_bf = w_row.astype(BF16)
    n_prev_bf = n_prev.astype(BF16)
    r_parts = (gp_ref[GP_R_HI], gp_ref[GP_R_MID], gp_ref[GP_R_LO])
    e_parts = _split3(bcum - m_t)
    split_row = lax.broadcasted_iota(jnp.int32, (SPLIT_ROWS, ML_CHUNK), 0)

    def outer_sum_operand(parts, one, first):
        other = 3 - first
        out = jnp.where((split_row >= other) & (split_row < other + 3), 1.0, 0.0)
        for i, part in enumerate(parts):
            out = jnp.where(split_row == first + i, part[one], out)
        return out.astype(BF16)

    src = lax.broadcasted_iota(jnp.int32, (ML_CHUNK, ML_CHUNK), 0)
    qry = lax.broadcasted_iota(jnp.int32, (ML_CHUNK, ML_CHUNK), 1)
    causal = src <= qry
    new_n = []

    for b in range(BATCH):
        grp = slice(b * GATE_ROWS, (b + 1) * GATE_ROWS)
        outs = []
        for h in range(MLSTM_HEADS):
            u = b * MLSTM_HEADS + h
            r = b * GATE_ROWS + h
            one = slice(r, r + 1)
            sl = slice(h * hd, (h + 1) * hd)
            qh = mqk_ref[b, :, sl]
            kh = mqk_ref[b, :, MLSTM_WIDTH + h * hd:MLSTM_WIDTH + (h + 1) * hd]
            vt = vt_ref[b, sl, :]

            c_prev = c_scr[u]
            c_loc = _dot((vt.astype(F32) * w_row[one]).astype(BF16), kh)
            n_loc = _dot(w_bf[grp], kh)[h:h + 1]
            c_scr[u] = s_prev[one] * c_prev + s_loc[one] * c_loc
            new_n.append(s_prev[one] * n_prev[one] + s_loc[one] * n_loc)
            if h == MLSTM_HEADS - 1:
                new_n.append(jnp.zeros((GATE_ROWS - MLSTM_HEADS, hd), F32))

            lhs = outer_sum_operand(r_parts, one, 0)
            rhs = outer_sum_operand(e_parts, one, 3)
            decay = jnp.exp(jnp.where(causal, _dot_tn(lhs, rhs), -jnp.inf))
            pt = _dot_nt(kh, qh) * decay
            nq = _dot_nt(n_prev_bf[grp], qh)[h:h + 1]
            den = inter[one] * nq + jnp.sum(pt, axis=0, keepdims=True)
            num = inter[one] * _dot_nt(c_prev.astype(BF16), qh) + _dot(vt, pt.astype(BF16))
            hout = num * (1.0 / jnp.maximum(jnp.abs(den), inv_floor[one]))
            ms = jnp.mean(hout * hout, axis=0, keepdims=True)
            y = hout * lax.rsqrt(ms + NORM_EPS) * ng_ref[sl, :]
            outs.append(jax.nn.sigmoid(mot_ref[b, sl, :].astype(F32)) * y)
        o_ref[b] = jnp.transpose(jnp.concatenate(outs, axis=0)).astype(BF16)

    n_scr[...] = jnp.concatenate(new_n, axis=0)
    _gate_planes(gt_next_ref, gb_ref, gp_scr)


def _mlstm(layer, mqk, mvt, mot, gt, gate_bias, norm_g):
    tok = lambda width: pl.BlockSpec((BATCH, ML_CHUNK, width), lambda c: (0, c, 0))
    feat = lambda height: pl.BlockSpec((BATCH, height, ML_CHUNK), lambda c: (0, 0, c))
    rows_all = BATCH * GATE_ROWS
    return pl.pallas_call(
        _mlstm_kernel,
        grid=(ML_STEPS,),
        in_specs=[
            tok(2 * MLSTM_WIDTH), feat(MLSTM_WIDTH), feat(MLSTM_WIDTH), feat(GATE_ROWS),
            pl.BlockSpec((BATCH, GATE_ROWS, ML_CHUNK),
                         lambda c: (0, 0, jnp.minimum(c + 1, ML_STEPS - 1))),
            _resident((rows_all, ML_CHUNK), layer),
            _resident((MLSTM_WIDTH, ML_CHUNK), layer),
        ],
        out_specs=tok(MLSTM_WIDTH),
        out_shape=jax.ShapeDtypeStruct((BATCH, SEQ, MLSTM_WIDTH), BF16),
        scratch_shapes=[
            pltpu.VMEM((ML_UNITS, MLSTM_HEAD_DIM, MLSTM_HEAD_DIM), F32),
            pltpu.VMEM((rows_all, MLSTM_HEAD_DIM), F32),
            pltpu.VMEM((rows_all, ML_CHUNK), F32),
            pltpu.VMEM((GP_PLANES, rows_all, ML_CHUNK), F32),
        ],
        compiler_params=pltpu.CompilerParams(
            dimension_semantics=("arbitrary",), vmem_limit_bytes=32 * MIB),
        name="mlstm",
    )(mqk, mvt, mot, gt, gt, gate_bias, norm_g)


def _merge_ffn_kernel(x_ref, att_ref, ml_ref, gate_ref, wa_ref, wm_ref, wo_ref,
                      g_ref, wu_ref, wd_ref, fg_ref, o_ref, *, final_norm):
    ga = gate_ref[:, :D_MODEL].astype(F32)
    gm = gate_ref[:, D_MODEL:].astype(F32)
    y = (jax.nn.sigmoid(ga) * _dot(att_ref[...], wa_ref[...])
         + jax.nn.sigmoid(gm) * _dot(ml_ref[...], wm_ref[...]))
    x = x_ref[...] + _dot(y.astype(BF16), wo_ref[...])
    hn = _rms_norm(x, g_ref[...]).astype(BF16)
    acc = x
    for j in range(D_FF // FF_TILE):
        cols = slice(j * FF_TILE, (j + 1) * FF_TILE)
        h = jnp.maximum(_dot(hn, wu_ref[:, cols]), 0.0)
        acc = acc + _dot((h * h).astype(BF16), wd_ref[cols, :])
    if final_norm:
        acc = _rms_norm(acc, fg_ref[...])
    o_ref[...] = acc


def _merge_ffn(layer, x2d, att, ml, gates, wa, wm, wo, g, wu, wd, final_g):
    tm = TOKEN_TILE
    row = lambda width: pl.BlockSpec((tm, width), lambda i: (i, 0))
    return pl.pallas_call(
        functools.partial(_merge_ffn_kernel, final_norm=(layer == DEPTH - 1)),
        grid=(TOKENS // tm,),
        in_specs=[row(D_MODEL), row(ATT_WIDTH), row(MLSTM_WIDTH), row(2 * D_MODEL),
                  _resident((ATT_WIDTH, D_MODEL)), _resident((MLSTM_WIDTH, D_MODEL)),
                  _resident((D_MODEL, D_MODEL)), _resident((1, D_MODEL), layer),
                  _resident((D_MODEL, D_FF)), _resident((D_FF, D_MODEL)),
                  _resident((1, D_MODEL))],
        out_specs=row(D_MODEL),
        out_shape=jax.ShapeDtypeStruct((TOKENS, D_MODEL), F32),
        compiler_params=pltpu.CompilerParams(
            dimension_semantics=("arbitrary",), vmem_limit_bytes=56 * MIB),
        name="merge_ffn",
    )(x2d, att, ml, gates, wa, wm, wo, g, wu, wd, final_g)


def _rel_bias_rows(rel_bias):
    lead = rel_bias.shape[:-1]
    far = jnp.broadcast_to(rel_bias[..., 2 * MAX_REL_DIST:], lead + (MAX_REL_DIST,))
    near = rel_bias[..., MAX_REL_DIST + 1 - Q_TILE:][..., ::-1]
    rest = jnp.broadcast_to(rel_bias[..., 2 * MAX_REL_DIST:],
                            lead + (REL_ROW - MAX_REL_DIST - near.shape[-1],))
    return jnp.concatenate([far, near, rest], axis=-1).astype(F32) * LOG2E


def kernel(x, mix_norm_g, w_in, conv_w, conv_b, b_igate, b_fgate, rel_bias, mh_norm_g,
           w_att_proj, w_mlstm_proj, w_out, ffn_norm_g, w_up, w_down, final_norm_g):
    w_t = jnp.swapaxes(w_in, 1, 2).astype(BF16)
    w_gates = w_t[:, IN_GATES:]
    w_gate_t = w_t[:, IN_MAIN_COLS:IN_GATES]
    gate_bias = jnp.tile(
        jnp.broadcast_to(jnp.concatenate([b_igate, b_fgate], axis=1)[:, :, None],
                         (DEPTH, GATE_ROWS, ML_CHUNK)), (1, BATCH, 1))
    norm_g = jnp.broadcast_to(mh_norm_g[:, :, None], (DEPTH, MLSTM_WIDTH, ML_CHUNK))
    rel_rows = _rel_bias_rows(rel_bias)
    mix_g = mix_norm_g.reshape(DEPTH, 1, D_MODEL)
    ffn_g = ffn_norm_g.reshape(DEPTH, 1, D_MODEL)
    conv_bias = conv_b.reshape(DEPTH, 1, 2 * MLSTM_WIDTH)
    final_g = final_norm_g.reshape(1, D_MODEL)

    h = x.reshape(TOKENS, D_MODEL)
    for l in range(DEPTH):
        qt, k_tok, vt_tiles, mqk, mvt, mot, gates, gt, wa, wm, wo, wu, wd = _in_proj(
            l, h, mix_g, w_t, w_gates, w_gate_t, conv_w, conv_bias,
            w_att_proj, w_mlstm_proj, w_out, w_up, w_down)
        att = _band_attn(l, qt, k_tok, vt_tiles, rel_rows)
        ml = _mlstm(l, mqk.reshape(BATCH, SEQ, 2 * MLSTM_WIDTH), mvt, mot, gt, gate_bias, norm_g)
        h = _merge_ffn(l, h, att.reshape(TOKENS, ATT_WIDTH), ml.reshape(TOKENS, MLSTM_WIDTH),
                       gates, wa, wm, wo, ffn_g, wu, wd, final_g)
    return h.reshape(BATCH, SEQ, D_MODEL)
```

```python
import functools
import math

import jax
import jax.numpy as jnp
from jax import lax
from jax.experimental import pallas as pl
from jax.experimental.pallas import tpu as pltpu

D_MODEL = 1024
BATCH = 4
SEQ = 4096
DEPTH = 2
CHUNK = 64
NORM_EPS = 1e-6
ATT_HEADS = 8
ATT_HEAD_DIM = 64
ATT_WIDTH = ATT_HEADS * ATT_HEAD_DIM
LEFT_CHUNKS = 8
BAND_CHUNKS = LEFT_CHUNKS + 1
BAND = BAND_CHUNKS * CHUNK
MAX_REL_DIST = 256
MLSTM_HEADS = 4
MLSTM_HEAD_DIM = 128
MLSTM_WIDTH = MLSTM_HEADS * MLSTM_HEAD_DIM
CONV_WIDTH = 4
D_FF = 4 * D_MODEL

LANES = 128
SUBLANES = 8

TOKENS = BATCH * SEQ
NUM_CHUNKS = SEQ // CHUNK
LEFT_ROWS = LEFT_CHUNKS * CHUNK
IN_Q = 0
IN_K = IN_Q + ATT_WIDTH
IN_V = IN_K + ATT_WIDTH
IN_MQ = IN_V + ATT_WIDTH
IN_MK = IN_MQ + MLSTM_WIDTH
IN_MV = IN_MK + MLSTM_WIDTH
IN_MO = IN_MV + MLSTM_WIDTH
IN_MAIN_COLS = IN_MO + MLSTM_WIDTH
GATE_ROWS = 2 * MLSTM_HEADS
IN_GATES = IN_MAIN_COLS + GATE_ROWS
FEATURE_MAJOR_ROWS = 2 * ATT_WIDTH + 2 * MLSTM_WIDTH

Q_TILE = 2 * CHUNK
WINDOW = BAND + CHUNK
WINDOW_TILES = WINDOW // LANES
ATT_TILES_PER_STEP = 8
ATT_STEPS = SEQ // (Q_TILE * ATT_TILES_PER_STEP)
REL_ROW = 1024
ONES_ROWS = ATT_HEAD_DIM
LOG2E = 1.4426950408889634
MASKED = -1e30

ML_CHUNK = LANES
ML_CHUNKS_PER_STEP = 2
ML_STEPS = SEQ // (ML_CHUNK * ML_CHUNKS_PER_STEP)
ML_UNITS = BATCH * MLSTM_HEADS
CONV_TAIL = 16
CONV_ROWS = 16
CONV_COLS = 128
SPLIT_ROWS = 16

TOKEN_TILE = 512
FF_TILE = 1024

F32 = jnp.float32
BF16 = jnp.bfloat16
MIB = 1024 * 1024
VMEM_LIMIT = {"in_proj": 56 * MIB, "band_attn": 56 * MIB, "mlstm": 56 * MIB, "merge_ffn": 56 * MIB}


def _resident(shape, layer=None):
    nd = len(shape)
    if layer is None:
        return pl.BlockSpec(shape, lambda *_: (0,) * nd, pipeline_mode=pl.Buffered(1))
    return pl.BlockSpec((None,) + tuple(shape), lambda *_: (layer,) + (0,) * nd,
                        pipeline_mode=pl.Buffered(1))


def _rms_norm(x, g):
    ms = jnp.mean(x * x, axis=-1, keepdims=True)
    return x * lax.rsqrt(ms + NORM_EPS) * g


def _dot(a, b):
    return jnp.dot(a, b, preferred_element_type=F32)


def _dot_nt(a, b):
    return lax.dot_general(a, b, (((1,), (1,)), ((), ())), preferred_element_type=F32)


def _dot_tn(a, b):
    return lax.dot_general(a, b, (((0,), (0,)), ((), ())), preferred_element_type=F32)


def _in_proj_kernel(x_ref, g_ref, w_ref, wgate_ref, wg_ref, cw_ref, cb_ref,
                    wa_ref, wm_ref, wo_ref, wu_ref, wd_ref,
                    qt_ref, k_ref, vt_ref, mqk_ref, mvt_ref, mot_ref, gate_ref, gt_ref,
                    wa_out, wm_out, wo_out, wu_out, wd_out,
                    conv_scr):
    a = ATT_WIDTH
    m = MLSTM_WIDTH

    @pl.when(pl.program_id(0) % (SEQ // TOKEN_TILE) == 0)
    def _():
        conv_scr[:CONV_TAIL, :] = jnp.zeros((CONV_TAIL, 2 * m), F32)

    for src_ref, dst_ref in ((wa_ref, wa_out), (wm_ref, wm_out), (wo_ref, wo_out),
                             (wu_ref, wu_out), (wd_ref, wd_out)):
        dst_ref[...] = src_ref[...].astype(BF16)

    xn = _rms_norm(x_ref[...], g_ref[...]).astype(BF16)

    def seg(lo, width):
        return _dot_nt(xn, w_ref[lo:lo + width, :])

    def seg_t(lo, width):
        return _dot_nt(w_ref[lo:lo + width, :], xn)

    conv_scr[CONV_TAIL:, :m] = seg(IN_MQ, m)
    conv_scr[CONV_TAIL:, m:] = seg(IN_MK, m)
    for c0 in range(0, 2 * m, CONV_COLS):
        cols = slice(c0, c0 + CONV_COLS)
        post_scale = 1.0 / math.sqrt(MLSTM_HEAD_DIM) if c0 >= m else None
        for r0 in range(0, TOKEN_TILE, CONV_ROWS):
            lo = CONV_TAIL - SUBLANES + r0
            ext = conv_scr[lo:lo + SUBLANES + CONV_ROWS, cols]
            acc = cb_ref[:, cols] + ext[SUBLANES:] * cw_ref[CONV_WIDTH - 1:CONV_WIDTH, cols]
            for d in range(1, CONV_WIDTH):
                tap = CONV_WIDTH - 1 - d
                acc = acc + pltpu.roll(ext, d, axis=0)[SUBLANES:] * cw_ref[tap:tap + 1, cols]
            half = 0.5 * acc
            act = half + half * jnp.tanh(half)
            if post_scale is not None:
                act = act * post_scale
            mqk_ref[r0:r0 + CONV_ROWS, cols] = act.astype(BF16)
    conv_scr[:CONV_TAIL, :] = conv_scr[TOKEN_TILE:, :]

    qt_ref[...] = (seg_t(IN_Q, a) * (LOG2E / math.sqrt(ATT_HEAD_DIM))).astype(BF16)
    vt = seg_t(IN_V, a).astype(BF16)
    for j in range(TOKEN_TILE // LANES):
        vt_ref[0, j] = vt[:, j * LANES:(j + 1) * LANES]
    mvt_ref[0] = seg_t(IN_MV, m).astype(BF16)
    mot_ref[0] = seg_t(IN_MO, m).astype(BF16)
    gt_ref[0] = _dot_nt(wg_ref[...], xn)
    k_ref[0] = seg(IN_K, a).astype(BF16)
    for j in range(2 * D_MODEL // 512):
        cols = slice(j * 512, (j + 1) * 512)
        gate_ref[:, cols] = _dot_nt(xn, wgate_ref[cols, :]).astype(BF16)


def _in_proj(layer, x2d, g, w_main, w_gates, w_gate_t, conv_w, conv_b, wa, wm, wo, wu, wd):
    steps = TOKENS // TOKEN_TILE

    def slab(rows, width):
        return (pl.BlockSpec((None, rows // steps, width), lambda i: (layer, i, 0)),
                pl.BlockSpec((rows // steps, width), lambda i: (i, 0)),
                jax.ShapeDtypeStruct((rows, width), BF16))

    slabs = [slab(ATT_WIDTH, D_MODEL), slab(MLSTM_WIDTH, D_MODEL), slab(D_MODEL, D_MODEL),
             slab(D_MODEL, D_FF), slab(D_FF, D_MODEL)]
    tm = TOKEN_TILE
    tiles_per_batch = SEQ // tm
    lane_tiles = tm // LANES
    row = lambda width: pl.BlockSpec((tm, width), lambda i: (i, 0))
    col = lambda height: pl.BlockSpec((1, height, tm),
                                      lambda i: (i // tiles_per_batch, 0, i % tiles_per_batch))
    k_spec = pl.BlockSpec((1, tm, ATT_WIDTH),
                          lambda i: (i // tiles_per_batch, i % tiles_per_batch, 0))
    vt_spec = pl.BlockSpec((1, lane_tiles, ATT_WIDTH, LANES),
                           lambda i: (i // tiles_per_batch, i % tiles_per_batch, 0, 0))
    out_shape = (
        jax.ShapeDtypeStruct((ATT_WIDTH, TOKENS), BF16),
        jax.ShapeDtypeStruct((BATCH, SEQ, ATT_WIDTH), BF16),
        jax.ShapeDtypeStruct((BATCH, SEQ // LANES, ATT_WIDTH, LANES), BF16),
        jax.ShapeDtypeStruct((TOKENS, 2 * MLSTM_WIDTH), BF16),
        jax.ShapeDtypeStruct((BATCH, MLSTM_WIDTH, SEQ), BF16),
        jax.ShapeDtypeStruct((BATCH, MLSTM_WIDTH, SEQ), BF16),
        jax.ShapeDtypeStruct((TOKENS, 2 * D_MODEL), BF16),
        jax.ShapeDtypeStruct((BATCH, GATE_ROWS, SEQ), F32),
    ) + tuple(s[2] for s in slabs)
    return pl.pallas_call(
        _in_proj_kernel,
        grid=(steps,),
        in_specs=[
            row(D_MODEL),
            _resident((1, D_MODEL), layer),
            _resident((IN_MAIN_COLS, D_MODEL), layer),
            _resident((2 * D_MODEL, D_MODEL), layer),
            _resident((GATE_ROWS, D_MODEL), layer),
            _resident((CONV_WIDTH, 2 * MLSTM_WIDTH), layer),
            _resident((1, 2 * MLSTM_WIDTH), layer),
        ] + [s[0] for s in slabs],
        out_specs=(pl.BlockSpec((ATT_WIDTH, tm), lambda i: (0, i)), k_spec, vt_spec,
                   row(2 * MLSTM_WIDTH), col(MLSTM_WIDTH), col(MLSTM_WIDTH), row(2 * D_MODEL),
                   col(GATE_ROWS)) + tuple(s[1] for s in slabs),
        out_shape=out_shape,
        scratch_shapes=[pltpu.VMEM((CONV_TAIL + tm, 2 * MLSTM_WIDTH), F32)],
        compiler_params=pltpu.CompilerParams(
            dimension_semantics=("arbitrary",), vmem_limit_bytes=VMEM_LIMIT["in_proj"]),
        name="in_proj",
    )(x2d, g, w_main, w_gates, w_gate_t, conv_w, conv_b, wa, wm, wo, wu, wd)


def _build_bias_table(rel_ref, bias_scr):
    kb = lax.broadcasted_iota(jnp.int32, (WINDOW, LANES), 0)
    lane = lax.broadcasted_iota(jnp.int32, (WINDOW, LANES), 1)
    band_lo = jnp.where(lane >= CHUNK, CHUNK, 0)
    in_band = (kb >= band_lo) & (kb < band_lo + BAND)
    for h in range(ATT_HEADS):
        rows = jnp.broadcast_to(rel_ref[h:h + 1, :], (LANES, REL_ROW))
        shifted = pltpu.roll(rows, 0, axis=1, stride=1, stride_axis=0)
        table = jnp.concatenate(
            [jnp.transpose(shifted[:, j * LANES:(j + 1) * LANES]) for j in range(WINDOW_TILES)],
            axis=0)
        bias_scr[h, :WINDOW, :] = jnp.where(in_band, table, MASKED)
        bias_scr[h, WINDOW:, :] = jnp.full((LEFT_ROWS, LANES), MASKED, F32)


def _band_attn_kernel(qt_ref, k_ref, vt_ref, rel_ref, o_ref, bias_scr):
    step = pl.program_id(1)

    @pl.when((pl.program_id(0) == 0) & (step == 0))
    def _():
        _build_bias_table(rel_ref, bias_scr)

    d = ATT_HEAD_DIM
    zeros_half = jnp.zeros((d, Q_TILE), BF16)
    ones_rows = jnp.ones((ONES_ROWS, LANES), BF16)
    for sub in range(ATT_TILES_PER_STEP):
        tile = step * ATT_TILES_PER_STEP + sub
        q_lanes = slice(sub * Q_TILE, (sub + 1) * Q_TILE)
        first_tile = jnp.maximum(tile - LEFT_ROWS // Q_TILE, 0)
        start = pl.multiple_of(first_tile * Q_TILE, Q_TILE)
        shift = pl.multiple_of(LEFT_ROWS - (tile - first_tile) * Q_TILE, Q_TILE)
        outs = []
        for h in range(ATT_HEADS):
            pair_dims = slice((h // 2) * 2 * d, (h // 2 + 1) * 2 * d)
            qh = qt_ref[h * d:(h + 1) * d, q_lanes]
            qm = jnp.concatenate([qh, zeros_half] if h % 2 == 0 else [zeros_half, qh], axis=0)
            tile_max, tile_out = [], []
            for j in range(WINDOW_TILES):
                keys = k_ref[0, pl.ds(start + j * LANES, LANES), pair_dims]
                st = _dot(keys, qm) + bias_scr[h, pl.ds(shift + j * LANES, LANES), :]
                m_j = jnp.max(st, axis=0, keepdims=True)
                p_j = jnp.exp2(st - m_j).astype(BF16)
                v_ones = jnp.concatenate(
                    [vt_ref[0, first_tile + j, h * d:(h + 1) * d, :], ones_rows], axis=0)
                tile_max.append(m_j)
                tile_out.append(_dot(v_ones, p_j))
            m = functools.reduce(jnp.maximum, tile_max)
            acc = functools.reduce(jnp.add, [jnp.exp2(m_j - m) * o
                                             for m_j, o in zip(tile_max, tile_out)])
            outs.append(acc[:d] / acc[d:d + 1])
        o_ref[0, q_lanes, :] = jnp.transpose(jnp.concatenate(outs, axis=0)).astype(BF16)


def _band_attn(layer, qt, k_tok, vt_tiles, rel_rows):
    return pl.pallas_call(
        _band_attn_kernel,
        grid=(BATCH, ATT_STEPS),
        in_specs=[
            pl.BlockSpec((ATT_WIDTH, ATT_TILES_PER_STEP * Q_TILE),
                         lambda b, s: (0, b * ATT_STEPS + s)),
            pl.BlockSpec((1, SEQ, ATT_WIDTH), lambda b, s: (b, 0, 0)),
            pl.BlockSpec((1, SEQ // LANES, ATT_WIDTH, LANES), lambda b, s: (b, 0, 0, 0)),
            _resident((ATT_HEADS, REL_ROW), layer),
        ],
        out_specs=pl.BlockSpec((1, ATT_TILES_PER_STEP * Q_TILE, ATT_WIDTH),
                               lambda b, s: (b, s, 0)),
        out_shape=jax.ShapeDtypeStruct((BATCH, SEQ, ATT_WIDTH), BF16),
        scratch_shapes=[pltpu.VMEM((ATT_HEADS, WINDOW + LEFT_ROWS, LANES), F32)],
        compiler_params=pltpu.CompilerParams(
            dimension_semantics=("arbitrary", "arbitrary"),
            vmem_limit_bytes=VMEM_LIMIT["band_attn"]),
        name="band_attn",
    )(qt, k_tok, vt_tiles, rel_rows)


def _scan_lanes(x, op, identity, segment):
    pos = lax.broadcasted_iota(jnp.int32, x.shape, 1) % segment
    d = 1
    while d < segment:
        x = op(x, jnp.where(pos >= d, pltpu.roll(x, d, axis=1), identity))
        d *= 2
    return x


def _split3(x):
    hi = x.astype(BF16).astype(F32)
    r1 = x - hi
    mid = r1.astype(BF16).astype(F32)
    lo = (r1 - mid).astype(BF16).astype(F32)
    return hi, mid, lo


GP_BCUM, GP_STAB, GP_W, GP_BLAST, GP_MLOC, GP_R_HI, GP_R_MID, GP_R_LO = range(8)
GP_PLANES = 8


def _gate_planes(gt_ref, lanes, gb_ref, planes_ref):
    rows_all = BATCH * GATE_ROWS
    gates = jnp.concatenate([gt_ref[b, :, lanes] for b in range(BATCH)], axis=0) + gb_ref[...]
    logf = jnp.minimum(gates, 0.0) - jnp.log1p(jnp.exp(-jnp.abs(gates)))
    bcum = pltpu.roll(_scan_lanes(logf, jnp.add, 0.0, ML_CHUNK), rows_all - MLSTM_HEADS, axis=0)
    b_last = jnp.broadcast_to(bcum[:, ML_CHUNK - 1:ML_CHUNK], (rows_all, ML_CHUNK))
    a_row = b_last - bcum + gates
    m_loc = jnp.broadcast_to(jnp.max(a_row, axis=1, keepdims=True), (rows_all, ML_CHUNK))
    r_row = gates - bcum
    planes_ref[GP_BCUM] = bcum
    planes_ref[GP_STAB] = bcum + _scan_lanes(r_row, jnp.maximum, -jnp.inf, ML_CHUNK)
    planes_ref[GP_W] = jnp.exp(a_row - m_loc)
    planes_ref[GP_BLAST] = b_last
    planes_ref[GP_MLOC] = m_loc
    for plane, part in zip((GP_R_HI, GP_R_MID, GP_R_LO), _split3(r_row)):
        planes_ref[plane] = part


def _mlstm_kernel(mqk_ref, vt_ref, mot_ref, gt_ref, gt_next_ref, gb_ref, ng_ref, o_ref,
                  c_scr, n_scr, m_scr, gp_scr):
    step = pl.program_id(0)

    @pl.when(step == 0)
    def _():
        c_scr[...] = jnp.zeros_like(c_scr)
        n_scr[...] = jnp.zeros_like(n_scr)
        m_scr[...] = jnp.zeros_like(m_scr)
        for sub in range(ML_CHUNKS_PER_STEP):
            _gate_planes(gt_ref, slice(sub * ML_CHUNK, (sub + 1) * ML_CHUNK), gb_ref,
                         gp_scr.at[sub])

    for sub in range(ML_CHUNKS_PER_STEP):
        _mlstm_chunk(slice(sub * ML_CHUNK, (sub + 1) * ML_CHUNK), mqk_ref, vt_ref, mot_ref,
                     gp_scr.at[sub], ng_ref, o_ref, c_scr, n_scr, m_scr)
    for sub in range(ML_CHUNKS_PER_STEP):
        _gate_planes(gt_next_ref, slice(sub * ML_CHUNK, (sub + 1) * ML_CHUNK), gb_ref,
                     gp_scr.at[sub])


def _mlstm_chunk(time, mqk_ref, vt_ref, mot_ref, gp_ref, ng_ref, o_ref, c_scr, n_scr, m_scr):
    hd = MLSTM_HEAD_DIM

    bcum = gp_ref[GP_BCUM]
    b_last = gp_ref[GP_BLAST]
    m_loc = gp_ref[GP_MLOC]
    w_row = gp_ref[GP_W]
    m_prev = m_scr[...]
    m_new = jnp.maximum(b_last + m_prev, m_loc)
    s_prev = jnp.exp(b_last + m_prev - m_new)
    s_loc = jnp.exp(m_loc - m_new)
    m_scr[...] = m_new
    g_row = bcum + m_prev
    m_t = jnp.maximum(g_row, gp_ref[GP_STAB])
    inter = jnp.exp(g_row - m_t)
    inv_floor = jnp.exp(-m_t)
    n_prev = n_scr[...]
    w_bf = w_row.astype(BF16)
    n_prev_bf = n_prev.astype(BF16)
    r_parts = (gp_ref[GP_R_HI], gp_ref[GP_R_MID], gp_ref[GP_R_LO])
    e_parts = _split3(bcum - m_t)
    split_row = lax.broadcasted_iota(jnp.int32, (SPLIT_ROWS, ML_CHUNK), 0)

    def outer_sum_operand(parts, one, first):
        other = 3 - first
        out = jnp.where((split_row >= other) & (split_row < other + 3), 1.0, 0.0)
        for i, part in enumerate(parts):
            out = jnp.where(split_row == first + i, part[one], out)
        return out.astype(BF16)

    src = lax.broadcasted_iota(jnp.int32, (ML_CHUNK, ML_CHUNK), 0)
    qry = lax.broadcasted_iota(jnp.int32, (ML_CHUNK, ML_CHUNK), 1)
    causal = src <= qry
    new_n = []

    for b in range(BATCH):
        grp = slice(b * GATE_ROWS, (b + 1) * GATE_ROWS)
        outs = []
        for h in range(MLSTM_HEADS):
            u = b * MLSTM_HEADS + h
            r = b * GATE_ROWS + h
            one = slice(r, r + 1)
            sl = slice(h * hd, (h + 1) * hd)
            qh = mqk_ref[b, time, sl]
            kh = mqk_ref[b, time, MLSTM_WIDTH + h * hd:MLSTM_WIDTH + (h + 1) * hd]
            vt = vt_ref[b, sl, time]

            c_prev = c_scr[u]
            c_loc = _dot((vt.astype(F32) * w_row[one]).astype(BF16), kh)
            n_loc = _dot(w_bf[grp], kh)[h:h + 1]
            c_scr[u] = s_prev[one] * c_prev + s_loc[one] * c_loc
            new_n.append(s_prev[one] * n_prev[one] + s_loc[one] * n_loc)
            if h == MLSTM_HEADS - 1:
                new_n.append(jnp.zeros((GATE_ROWS - MLSTM_HEADS, hd), F32))

            lhs = outer_sum_operand(r_parts, one, 0)
            rhs = outer_sum_operand(e_parts, one, 3)
            decay = jnp.exp(jnp.where(causal, _dot_tn(lhs, rhs), -jnp.inf))
            pt = _dot_nt(kh, qh) * decay
            nq = _dot_nt(n_prev_bf[grp], qh)[h:h + 1]
            den = inter[one] * nq + jnp.sum(pt, axis=0, keepdims=True)
            num = inter[one] * _dot_nt(c_prev.astype(BF16), qh) + _dot(vt, pt.astype(BF16))
            hout = num * (1.0 / jnp.maximum(jnp.abs(den), inv_floor[one]))
            ms = jnp.mean(hout * hout, axis=0, keepdims=True)
            y = hout * lax.rsqrt(ms + NORM_EPS) * ng_ref[sl, :]
            outs.append(jax.nn.sigmoid(mot_ref[b, sl, time].astype(F32)) * y)
        o_ref[b, time, :] = jnp.transpose(jnp.concatenate(outs, axis=0)).astype(BF16)

    n_scr[...] = jnp.concatenate(new_n, axis=0)


def _mlstm(layer, mqk, mvt, mot, gt, gate_bias, norm_g):
    span = ML_CHUNKS_PER_STEP * ML_CHUNK
    tok = lambda width: pl.BlockSpec((BATCH, span, width), lambda c: (0, c, 0))
    feat = lambda height: pl.BlockSpec((BATCH, height, span), lambda c: (0, 0, c))
    rows_all = BATCH * GATE_ROWS
    return pl.pallas_call(
        _mlstm_kernel,
        grid=(ML_STEPS,),
        in_specs=[
            tok(2 * MLSTM_WIDTH), feat(MLSTM_WIDTH), feat(MLSTM_WIDTH), feat(GATE_ROWS),
            pl.BlockSpec((BATCH, GATE_ROWS, span),
                         lambda c: (0, 0, jnp.minimum(c + 1, ML_STEPS - 1))),
            _resident((rows_all, ML_CHUNK), layer),
            _resident((MLSTM_WIDTH, ML_CHUNK), layer),
        ],
        out_specs=tok(MLSTM_WIDTH),
        out_shape=jax.ShapeDtypeStruct((BATCH, SEQ, MLSTM_WIDTH), BF16),
        scratch_shapes=[
            pltpu.VMEM((ML_UNITS, MLSTM_HEAD_DIM, MLSTM_HEAD_DIM), F32),
            pltpu.VMEM((rows_all, MLSTM_HEAD_DIM), F32),
            pltpu.VMEM((rows_all, ML_CHUNK), F32),
            pltpu.VMEM((ML_CHUNKS_PER_STEP, GP_PLANES, rows_all, ML_CHUNK), F32),
        ],
        compiler_params=pltpu.CompilerParams(
            dimension_semantics=("arbitrary",), vmem_limit_bytes=VMEM_LIMIT["mlstm"]),
        name="mlstm",
    )(mqk, mvt, mot, gt, gt, gate_bias, norm_g)


def _merge_ffn_kernel(x_ref, att_ref, ml_ref, gate_ref, wa_ref, wm_ref, wo_ref,
                      g_ref, wu_ref, wd_ref, fg_ref, o_ref, *, final_norm):
    ga = gate_ref[:, :D_MODEL].astype(F32)
    gm = gate_ref[:, D_MODEL:].astype(F32)
    y = (jax.nn.sigmoid(ga) * _dot(att_ref[...], wa_ref[...])
         + jax.nn.sigmoid(gm) * _dot(ml_ref[...], wm_ref[...]))
    x = x_ref[...] + _dot(y.astype(BF16), wo_ref[...])
    hn = _rms_norm(x, g_ref[...]).astype(BF16)
    acc = x
    for j in range(D_FF // FF_TILE):
        cols = slice(j * FF_TILE, (j + 1) * FF_TILE)
        h = jnp.maximum(_dot(hn, wu_ref[:, cols]), 0.0)
        acc = acc + _dot((h * h).astype(BF16), wd_ref[cols, :])
    if final_norm:
        acc = _rms_norm(acc, fg_ref[...])
    o_ref[...] = acc


def _merge_ffn(layer, x2d, att, ml, gates, wa, wm, wo, g, wu, wd, final_g):
    tm = TOKEN_TILE
    row = lambda width: pl.BlockSpec((tm, width), lambda i: (i, 0))
    return pl.pallas_call(
        functools.partial(_merge_ffn_kernel, final_norm=(layer == DEPTH - 1)),
        grid=(TOKENS // tm,),
        in_specs=[row(D_MODEL), row(ATT_WIDTH), row(MLSTM_WIDTH), row(2 * D_MODEL),
                  _resident((ATT_WIDTH, D_MODEL)), _resident((MLSTM_WIDTH, D_MODEL)),
                  _resident((D_MODEL, D_MODEL)), _resident((1, D_MODEL), layer),
                  _resident((D_MODEL, D_FF)), _resident((D_FF, D_MODEL)),
                  _resident((1, D_MODEL))],
        out_specs=row(D_MODEL),
        out_shape=jax.ShapeDtypeStruct((TOKENS, D_MODEL), F32),
        compiler_params=pltpu.CompilerParams(
            dimension_semantics=("arbitrary",), vmem_limit_bytes=VMEM_LIMIT["merge_ffn"]),
        name="merge_ffn",
    )(x2d, att, ml, gates, wa, wm, wo, g, wu, wd, final_g)


def _rel_bias_rows(rel_bias):
    lead = rel_bias.shape[:-1]
    far = jnp.broadcast_to(rel_bias[..., 2 * MAX_REL_DIST:], lead + (MAX_REL_DIST,))
    near = rel_bias[..., MAX_REL_DIST + 1 - Q_TILE:][..., ::-1]
    rest = jnp.broadcast_to(rel_bias[..., 2 * MAX_REL_DIST:],
                            lead + (REL_ROW - MAX_REL_DIST - near.shape[-1],))
    return jnp.concatenate([far, near, rest], axis=-1).astype(F32) * LOG2E


def kernel(x, mix_norm_g, w_in, conv_w, conv_b, b_igate, b_fgate, rel_bias, mh_norm_g,
           w_att_proj, w_mlstm_proj, w_out, ffn_norm_g, w_up, w_down, final_norm_g):
    w_t = jnp.swapaxes(w_in, 1, 2).astype(BF16)
    w_gates = w_t[:, IN_GATES:]
    w_gate_t = w_t[:, IN_MAIN_COLS:IN_GATES]
    gate_bias = jnp.tile(
        jnp.broadcast_to(jnp.concatenate([b_igate, b_fgate], axis=1)[:, :, None],
                         (DEPTH, GATE_ROWS, ML_CHUNK)), (1, BATCH, 1))
    norm_g = jnp.broadcast_to(mh_norm_g[:, :, None], (DEPTH, MLSTM_WIDTH, ML_CHUNK))
    rel_rows = _rel_bias_rows(rel_bias)
    mix_g = mix_norm_g.reshape(DEPTH, 1, D_MODEL)
    ffn_g = ffn_norm_g.reshape(DEPTH, 1, D_MODEL)
    conv_bias = conv_b.reshape(DEPTH, 1, 2 * MLSTM_WIDTH)
    final_g = final_norm_g.reshape(1, D_MODEL)

    h = x.reshape(TOKENS, D_MODEL)
    for l in range(DEPTH):
        qt, k_tok, vt_tiles, mqk, mvt, mot, gates, gt, wa, wm, wo, wu, wd = _in_proj(
            l, h, mix_g, w_t, w_gates, w_gate_t, conv_w, conv_bias,
            w_att_proj, w_mlstm_proj, w_out, w_up, w_down)
        att = _band_attn(l, qt, k_tok, vt_tiles, rel_rows)
        ml = _mlstm(l, mqk.reshape(BATCH, SEQ, 2 * MLSTM_WIDTH), mvt, mot, gt, gate_bias, norm_g)
        h = _merge_ffn(l, h, att.reshape(TOKENS, ATT_WIDTH), ml.reshape(TOKENS, MLSTM_WIDTH),
                       gates, wa, wm, wo, ffn_g, wu, wd, final_g)
    return h.reshape(BATCH, SEQ, D_MODEL)
```

```python
import functools
import math

import jax
import jax.numpy as jnp
from jax import lax
from jax.experimental import pallas as pl
from jax.experimental.pallas import tpu as pltpu

D_MODEL = 1024
BATCH = 4
SEQ = 4096
DEPTH = 2
CHUNK = 64
NORM_EPS = 1e-6
ATT_HEADS = 8
ATT_HEAD_DIM = 64
ATT_WIDTH = ATT_HEADS * ATT_HEAD_DIM
LEFT_CHUNKS = 8
BAND_CHUNKS = LEFT_CHUNKS + 1
BAND = BAND_CHUNKS * CHUNK
MAX_REL_DIST = 256
MLSTM_HEADS = 4
MLSTM_HEAD_DIM = 128
MLSTM_WIDTH = MLSTM_HEADS * MLSTM_HEAD_DIM
CONV_WIDTH = 4
D_FF = 4 * D_MODEL

LANES = 128
SUBLANES = 8

TOKENS = BATCH * SEQ
LEFT_ROWS = LEFT_CHUNKS * CHUNK
IN_Q = 0
IN_K = IN_Q + ATT_WIDTH
IN_V = IN_K + ATT_WIDTH
IN_MQ = IN_V + ATT_WIDTH
IN_MK = IN_MQ + MLSTM_WIDTH
IN_MV = IN_MK + MLSTM_WIDTH
IN_MO = IN_MV + MLSTM_WIDTH
IN_MAIN_COLS = IN_MO + MLSTM_WIDTH
GATE_ROWS = 2 * MLSTM_HEADS
IN_GATES = IN_MAIN_COLS + GATE_ROWS

Q_TILE = 2 * CHUNK
WINDOW = BAND + CHUNK
WINDOW_TILES = WINDOW // LANES
ATT_TILES_PER_STEP = 8
ATT_STEPS = SEQ // (Q_TILE * ATT_TILES_PER_STEP)
REL_ROW = 1024
ONES_ROWS = ATT_HEAD_DIM
LOG2E = 1.4426950408889634
MASKED = -1e30

ML_CHUNK = LANES
ML_CHUNKS_PER_STEP = 2
ML_STEPS = SEQ // (ML_CHUNK * ML_CHUNKS_PER_STEP)
ML_UNITS = BATCH * MLSTM_HEADS
CONV_TAIL = 16
CONV_ROWS = 16
CONV_COLS = 128
SPLIT_ROWS = 16

TOKEN_TILE = 512
FF_TILE = 1024

F32 = jnp.float32
BF16 = jnp.bfloat16
MIB = 1024 * 1024
VMEM_LIMIT = {"in_proj": 56 * MIB, "band_attn": 56 * MIB, "mlstm": 56 * MIB, "merge_ffn": 56 * MIB}


def _resident(shape, layer=None):
    nd = len(shape)
    if layer is None:
        return pl.BlockSpec(shape, lambda *_: (0,) * nd, pipeline_mode=pl.Buffered(1))
    return pl.BlockSpec((None,) + tuple(shape), lambda *_: (layer,) + (0,) * nd,
                        pipeline_mode=pl.Buffered(1))


def _rms_norm(x, g):
    ms = jnp.mean(x * x, axis=-1, keepdims=True)
    return x * lax.rsqrt(ms + NORM_EPS) * g


def _dot(a, b):
    return jnp.dot(a, b, preferred_element_type=F32)


def _dot_nt(a, b):
    return lax.dot_general(a, b, (((1,), (1,)), ((), ())), preferred_element_type=F32)


def _dot_tn(a, b):
    return lax.dot_general(a, b, (((0,), (0,)), ((), ())), preferred_element_type=F32)


def _in_proj_kernel(x_ref, g_ref, w_ref, wgate_ref, wg_ref, cw_ref, cb_ref,
                    wa_ref, wm_ref, wo_ref, wu_ref, wd_ref,
                    qt_ref, k_ref, vt_ref, mqk_ref, mvt_ref, mot_ref, gate_ref, gt_ref,
                    wa_out, wm_out, wo_out, wu_out, wd_out,
                    conv_scr):
    a = ATT_WIDTH
    m = MLSTM_WIDTH

    @pl.when(pl.program_id(0) % (SEQ // TOKEN_TILE) == 0)
    def _():
        conv_scr[:CONV_TAIL, :] = jnp.zeros((CONV_TAIL, 2 * m), F32)

    for src_ref, dst_ref in ((wa_ref, wa_out), (wm_ref, wm_out), (wo_ref, wo_out),
                             (wu_ref, wu_out), (wd_ref, wd_out)):
        dst_ref[...] = src_ref[...].astype(BF16)

    xn = _rms_norm(x_ref[...], g_ref[...]).astype(BF16)

    def seg(lo, width):
        return _dot_nt(xn, w_ref[lo:lo + width, :])

    def seg_t(lo, width):
        return _dot_nt(w_ref[lo:lo + width, :], xn)

    conv_scr[CONV_TAIL:, :m] = seg(IN_MQ, m)
    conv_scr[CONV_TAIL:, m:] = seg(IN_MK, m)
    for c0 in range(0, 2 * m, CONV_COLS):
        cols = slice(c0, c0 + CONV_COLS)
        post_scale = 1.0 / math.sqrt(MLSTM_HEAD_DIM) if c0 >= m else None
        for r0 in range(0, TOKEN_TILE, CONV_ROWS):
            lo = CONV_TAIL - SUBLANES + r0
            ext = conv_scr[lo:lo + SUBLANES + CONV_ROWS, cols]
            acc = cb_ref[:, cols] + ext[SUBLANES:] * cw_ref[CONV_WIDTH - 1:CONV_WIDTH, cols]
            for d in range(1, CONV_WIDTH):
                tap = CONV_WIDTH - 1 - d
                acc = acc + pltpu.roll(ext, d, axis=0)[SUBLANES:] * cw_ref[tap:tap + 1, cols]
            half = 0.5 * acc
            act = half + half * jnp.tanh(half)
            if post_scale is not None:
                act = act * post_scale
            mqk_ref[r0:r0 + CONV_ROWS, cols] = act.astype(BF16)
    conv_scr[:CONV_TAIL, :] = conv_scr[TOKEN_TILE:, :]

    qt_ref[...] = (seg_t(IN_Q, a) * (LOG2E / math.sqrt(ATT_HEAD_DIM))).astype(BF16)
    vt = seg_t(IN_V, a).astype(BF16)
    for j in range(TOKEN_TILE // LANES):
        vt_ref[0, j] = vt[:, j * LANES:(j + 1) * LANES]
    mvt_ref[0] = seg_t(IN_MV, m).astype(BF16)
    mot_ref[0] = seg_t(IN_MO, m).astype(BF16)
    gt_ref[0] = _dot_nt(wg_ref[...], xn)
    k_ref[0] = seg(IN_K, a).astype(BF16)
    for j in range(2 * D_MODEL // 512):
        cols = slice(j * 512, (j + 1) * 512)
        gate_ref[:, cols] = _dot_nt(xn, wgate_ref[cols, :]).astype(BF16)


def _in_proj(layer, x2d, g, w_main, w_gates, w_gate_t, conv_w, conv_b, wa, wm, wo, wu, wd):
    steps = TOKENS // TOKEN_TILE

    def slab(rows, width):
        return (pl.BlockSpec((None, rows // steps, width), lambda i: (layer, i, 0)),
                pl.BlockSpec((rows // steps, width), lambda i: (i, 0)),
                jax.ShapeDtypeStruct((rows, width), BF16))

    slabs = [slab(ATT_WIDTH, D_MODEL), slab(MLSTM_WIDTH, D_MODEL), slab(D_MODEL, D_MODEL),
             slab(D_MODEL, D_FF), slab(D_FF, D_MODEL)]
    tm = TOKEN_TILE
    tiles_per_batch = SEQ // tm
    lane_tiles = tm // LANES
    row = lambda width: pl.BlockSpec((tm, width), lambda i: (i, 0))
    col = lambda height: pl.BlockSpec((1, height, tm),
                                      lambda i: (i // tiles_per_batch, 0, i % tiles_per_batch))
    k_spec = pl.BlockSpec((1, tm, ATT_WIDTH),
                          lambda i: (i // tiles_per_batch, i % tiles_per_batch, 0))
    vt_spec = pl.BlockSpec((1, lane_tiles, ATT_WIDTH, LANES),
                           lambda i: (i // tiles_per_batch, i % tiles_per_batch, 0, 0))
    out_shape = (
        jax.ShapeDtypeStruct((ATT_WIDTH, TOKENS), BF16),
        jax.ShapeDtypeStruct((BATCH, SEQ, ATT_WIDTH), BF16),
        jax.ShapeDtypeStruct((BATCH, SEQ // LANES, ATT_WIDTH, LANES), BF16),
        jax.ShapeDtypeStruct((TOKENS, 2 * MLSTM_WIDTH), BF16),
        jax.ShapeDtypeStruct((BATCH, MLSTM_WIDTH, SEQ), BF16),
        jax.ShapeDtypeStruct((BATCH, MLSTM_WIDTH, SEQ), BF16),
        jax.ShapeDtypeStruct((TOKENS, 2 * D_MODEL), BF16),
        jax.ShapeDtypeStruct((BATCH, GATE_ROWS, SEQ), F32),
    ) + tuple(s[2] for s in slabs)
    return pl.pallas_call(
        _in_proj_kernel,
        grid=(steps,),
        in_specs=[
            row(D_MODEL),
            _resident((1, D_MODEL), layer),
            _resident((IN_MAIN_COLS, D_MODEL), layer),
            _resident((2 * D_MODEL, D_MODEL), layer),
            _resident((GATE_ROWS, D_MODEL), layer),
            _resident((CONV_WIDTH, 2 * MLSTM_WIDTH), layer),
            _resident((1, 2 * MLSTM_WIDTH), layer),
        ] + [s[0] for s in slabs],
        out_specs=(pl.BlockSpec((ATT_WIDTH, tm), lambda i: (0, i)), k_spec, vt_spec,
                   row(2 * MLSTM_WIDTH), col(MLSTM_WIDTH), col(MLSTM_WIDTH), row(2 * D_MODEL),
                   col(GATE_ROWS)) + tuple(s[1] for s in slabs),
        out_shape=out_shape,
        scratch_shapes=[pltpu.VMEM((CONV_TAIL + tm, 2 * MLSTM_WIDTH), F32)],
        compiler_params=pltpu.CompilerParams(
            dimension_semantics=("arbitrary",), vmem_limit_bytes=VMEM_LIMIT["in_proj"]),
        name="in_proj",
    )(x2d, g, w_main, w_gates, w_gate_t, conv_w, conv_b, wa, wm, wo, wu, wd)


def _build_bias_table(rel_ref, bias_scr):
    kb = lax.broadcasted_iota(jnp.int32, (WINDOW, LANES), 0)
    lane = lax.broadcasted_iota(jnp.int32, (WINDOW, LANES), 1)
    band_lo = jnp.where(lane >= CHUNK, CHUNK, 0)
    in_band = (kb >= band_lo) & (kb < band_lo + BAND)
    for h in range(ATT_HEADS):
        rows = jnp.broadcast_to(rel_ref[h:h + 1, :], (LANES, REL_ROW))
        shifted = pltpu.roll(rows, 0, axis=1, stride=1, stride_axis=0)
        table = jnp.concatenate(
            [jnp.transpose(shifted[:, j * LANES:(j + 1) * LANES]) for j in range(WINDOW_TILES)],
            axis=0)
        bias_scr[h, :WINDOW, :] = jnp.where(in_band, table, MASKED)
        bias_scr[h, WINDOW:, :] = jnp.full((LEFT_ROWS, LANES), MASKED, F32)


def _band_attn_kernel(qt_ref, k_ref, vt_ref, rel_ref, o_ref, bias_scr):
    step = pl.program_id(1)

    @pl.when((pl.program_id(0) == 0) & (step == 0))
    def _():
        _build_bias_table(rel_ref, bias_scr)

    d = ATT_HEAD_DIM
    zeros_half = jnp.zeros((d, Q_TILE), BF16)
    ones_rows = jnp.ones((ONES_ROWS, LANES), BF16)
    for sub in range(ATT_TILES_PER_STEP):
        tile = step * ATT_TILES_PER_STEP + sub
        q_lanes = slice(sub * Q_TILE, (sub + 1) * Q_TILE)
        first_tile = jnp.maximum(tile - LEFT_ROWS // Q_TILE, 0)
        start = pl.multiple_of(first_tile * Q_TILE, Q_TILE)
        shift = pl.multiple_of(LEFT_ROWS - (tile - first_tile) * Q_TILE, Q_TILE)
        outs = []
        for h in range(ATT_HEADS):
            pair_dims = slice((h // 2) * 2 * d, (h // 2 + 1) * 2 * d)
            qh = qt_ref[h * d:(h + 1) * d, q_lanes]
            qm = jnp.concatenate([qh, zeros_half] if h % 2 == 0 else [zeros_half, qh], axis=0)
            tile_max, tile_out = [], []
            for j in range(WINDOW_TILES):
                keys = k_ref[0, pl.ds(start + j * LANES, LANES), pair_dims]
                st = _dot(keys, qm) + bias_scr[h, pl.ds(shift + j * LANES, LANES), :]
                m_j = jnp.max(st, axis=0, keepdims=True)
                p_j = jnp.exp2(st - m_j).astype(BF16)
                v_ones = jnp.concatenate(
                    [vt_ref[0, first_tile + j, h * d:(h + 1) * d, :], ones_rows], axis=0)
                tile_max.append(m_j)
                tile_out.append(_dot(v_ones, p_j))
            m = functools.reduce(jnp.maximum, tile_max)
            acc = functools.reduce(jnp.add, [jnp.exp2(m_j - m) * o
                                             for m_j, o in zip(tile_max, tile_out)])
            outs.append(acc[:d] / acc[d:d + 1])
        o_ref[0, q_lanes, :] = jnp.transpose(jnp.concatenate(outs, axis=0)).astype(BF16)


def _band_attn(layer, qt, k_tok, vt_tiles, rel_rows):
    return pl.pallas_call(
        _band_attn_kernel,
        grid=(BATCH, ATT_STEPS),
        in_specs=[
            pl.BlockSpec((ATT_WIDTH, ATT_TILES_PER_STEP * Q_TILE),
                         lambda b, s: (0, b * ATT_STEPS + s)),
            pl.BlockSpec((1, SEQ, ATT_WIDTH), lambda b, s: (b, 0, 0)),
            pl.BlockSpec((1, SEQ // LANES, ATT_WIDTH, LANES), lambda b, s: (b, 0, 0, 0)),
            _resident((ATT_HEADS, REL_ROW), layer),
        ],
        out_specs=pl.BlockSpec((1, ATT_TILES_PER_STEP * Q_TILE, ATT_WIDTH),
                               lambda b, s: (b, s, 0)),
        out_shape=jax.ShapeDtypeStruct((BATCH, SEQ, ATT_WIDTH), BF16),
        scratch_shapes=[pltpu.VMEM((ATT_HEADS, WINDOW + LEFT_ROWS, LANES), F32)],
        compiler_params=pltpu.CompilerParams(
            dimension_semantics=("arbitrary", "arbitrary"),
            vmem_limit_bytes=VMEM_LIMIT["band_attn"]),
        name="band_attn",
    )(qt, k_tok, vt_tiles, rel_rows)


def _scan_lanes(x, op, identity, segment):
    pos = lax.broadcasted_iota(jnp.int32, x.shape, 1) % segment
    d = 1
    while d < segment:
        x = op(x, jnp.where(pos >= d, pltpu.roll(x, d, axis=1), identity))
        d *= 2
    return x


def _split3(x):
    hi = x.astype(BF16).astype(F32)
    r1 = x - hi
    mid = r1.astype(BF16).astype(F32)
    lo = (r1 - mid).astype(BF16).astype(F32)
    return hi, mid, lo


GP_BCUM, GP_STAB, GP_W, GP_BLAST, GP_MLOC, GP_R_HI, GP_R_MID, GP_R_LO = range(8)
GP_PLANES = 8


def _gate_planes(gt_ref, lanes, gb_ref, planes_ref):
    rows_all = BATCH * GATE_ROWS
    gates = jnp.concatenate([gt_ref[b, :, lanes] for b in range(BATCH)], axis=0) + gb_ref[...]
    logf = jnp.minimum(gates, 0.0) - jnp.log1p(jnp.exp(-jnp.abs(gates)))
    bcum = pltpu.roll(_scan_lanes(logf, jnp.add, 0.0, ML_CHUNK), rows_all - MLSTM_HEADS, axis=0)
    b_last = jnp.broadcast_to(bcum[:, ML_CHUNK - 1:ML_CHUNK], (rows_all, ML_CHUNK))
    a_row = b_last - bcum + gates
    m_loc = jnp.broadcast_to(jnp.max(a_row, axis=1, keepdims=True), (rows_all, ML_CHUNK))
    r_row = gates - bcum
    planes_ref[GP_BCUM] = bcum
    planes_ref[GP_STAB] = bcum + _scan_lanes(r_row, jnp.maximum, -jnp.inf, ML_CHUNK)
    planes_ref[GP_W] = jnp.exp(a_row - m_loc)
    planes_ref[GP_BLAST] = b_last
    planes_ref[GP_MLOC] = m_loc
    for plane, part in zip((GP_R_HI, GP_R_MID, GP_R_LO), _split3(r_row)):
        planes_ref[plane] = part


def _mlstm_kernel(mqk_ref, vt_ref, mot_ref, gt_ref, gt_next_ref, gb_ref, ng_ref, o_ref,
                  c_scr, n_scr, m_scr, gp_scr):
    step = pl.program_id(0)

    @pl.when(step == 0)
    def _():
        c_scr[...] = jnp.zeros_like(c_scr)
        n_scr[...] = jnp.zeros_like(n_scr)
        m_scr[...] = jnp.zeros_like(m_scr)
        for sub in range(ML_CHUNKS_PER_STEP):
            _gate_planes(gt_ref, slice(sub * ML_CHUNK, (sub + 1) * ML_CHUNK), gb_ref,
                         gp_scr.at[sub])

    for sub in range(ML_CHUNKS_PER_STEP):
        _mlstm_chunk(slice(sub * ML_CHUNK, (sub + 1) * ML_CHUNK), mqk_ref, vt_ref, mot_ref,
                     gp_scr.at[sub], ng_ref, o_ref, c_scr, n_scr, m_scr)
    for sub in range(ML_CHUNKS_PER_STEP):
        _gate_planes(gt_next_ref, slice(sub * ML_CHUNK, (sub + 1) * ML_CHUNK), gb_ref,
                     gp_scr.at[sub])


def _mlstm_chunk(time, mqk_ref, vt_ref, mot_ref, gp_ref, ng_ref, o_ref, c_scr, n_scr, m_scr):
    hd = MLSTM_HEAD_DIM

    bcum = gp_ref[GP_BCUM]
    b_last = gp_ref[GP_BLAST]
    m_loc = gp_ref[GP_MLOC]
    w_row = gp_ref[GP_W]
    m_prev = m_scr[...]
    m_new = jnp.maximum(b_last + m_prev, m_loc)
    s_prev = jnp.exp(b_last + m_prev - m_new)
    s_loc = jnp.exp(m_loc - m_new)
    m_scr[...] = m_new
    g_row = bcum + m_prev
    m_t = jnp.maximum(g_row, gp_ref[GP_STAB])
    inter = jnp.exp(g_row - m_t)
    inv_floor = jnp.exp(-m_t)
    n_prev = n_scr[...]
    w_bf = w_row.astype(BF16)
    n_prev_bf = n_prev.astype(BF16)
    r_parts = (gp_ref[GP_R_HI], gp_ref[GP_R_MID], gp_ref[GP_R_LO])
    e_parts = _split3(bcum - m_t)
    split_row = lax.broadcasted_iota(jnp.int32, (SPLIT_ROWS, ML_CHUNK), 0)

    def outer_sum_operand(parts, one, first):
        other = 3 - first
        out = jnp.where((split_row >= other) & (split_row < other + 3), 1.0, 0.0)
        for i, part in enumerate(parts):
            out = jnp.where(split_row == first + i, part[one], out)
        return out.astype(BF16)

    src = lax.broadcasted_iota(jnp.int32, (ML_CHUNK, ML_CHUNK), 0)
    qry = lax.broadcasted_iota(jnp.int32, (ML_CHUNK, ML_CHUNK), 1)
    causal = src <= qry
    new_n = {}
    outs = {}

    def stage_matmuls(u):
        b, h = divmod(u, MLSTM_HEADS)
        one = slice(b * GATE_ROWS + h, b * GATE_ROWS + h + 1)
        grp = slice(b * GATE_ROWS, (b + 1) * GATE_ROWS)
        sl = slice(h * hd, (h + 1) * hd)
        qh = mqk_ref[b, time, sl]
        kh = mqk_ref[b, time, MLSTM_WIDTH + h * hd:MLSTM_WIDTH + (h + 1) * hd]
        vt = vt_ref[b, sl, time]
        c_prev = c_scr[u]
        lhs = outer_sum_operand(r_parts, one, 0)
        rhs = outer_sum_operand(e_parts, one, 3)
        return dict(
            one=one, sl=sl, b=b, h=h, vt=vt, c_prev=c_prev,
            c_loc=_dot((vt.astype(F32) * w_row[one]).astype(BF16), kh),
            n_loc=_dot(w_bf[grp], kh)[h:h + 1],
            exponent=_dot_tn(lhs, rhs),
            scores=_dot_nt(kh, qh),
            nq=_dot_nt(n_prev_bf[grp], qh)[h:h + 1],
            cq=_dot_nt(c_prev.astype(BF16), qh))

    def stage_decay(u, st):
        one = st["one"]
        c_scr[u] = s_prev[one] * st["c_prev"] + s_loc[one] * st["c_loc"]
        new_n[u] = s_prev[one] * n_prev[one] + s_loc[one] * st["n_loc"]
        pt = st["scores"] * jnp.exp(jnp.where(causal, st["exponent"], -jnp.inf))
        st["den"] = inter[one] * st["nq"] + jnp.sum(pt, axis=0, keepdims=True)
        st["pt"] = pt.astype(BF16)

    def stage_intra(u, st):
        st["intra"] = _dot(st["vt"], st["pt"])

    def stage_out(u, st):
        one, sl, b = st["one"], st["sl"], st["b"]
        num = inter[one] * st["cq"] + st["intra"]
        hout = num * (1.0 / jnp.maximum(jnp.abs(st["den"]), inv_floor[one]))
        ms = jnp.mean(hout * hout, axis=0, keepdims=True)
        y = hout * lax.rsqrt(ms + NORM_EPS) * ng_ref[sl, :]
        outs[u] = jax.nn.sigmoid(mot_ref[b, sl, time].astype(F32)) * y
        if st["h"] == MLSTM_HEADS - 1:
            heads = [outs.pop(b * MLSTM_HEADS + i) for i in range(MLSTM_HEADS)]
            o_ref[b, time, :] = jnp.transpose(jnp.concatenate(heads, axis=0)).astype(BF16)

    stages = ((stage_decay, 2), (stage_intra, 3), (stage_out, 5))
    state = {}
    for k in range(ML_UNITS + stages[-1][1]):
        if k < ML_UNITS:
            state[k] = stage_matmuls(k)
        for stage, lag in stages:
            if 0 <= k - lag < ML_UNITS:
                stage(k - lag, state[k - lag])

    pad = jnp.zeros((GATE_ROWS - MLSTM_HEADS, hd), F32)
    n_scr[...] = jnp.concatenate(
        [row for b in range(BATCH)
         for row in [new_n[b * MLSTM_HEADS + i] for i in range(MLSTM_HEADS)] + [pad]], axis=0)


def _mlstm(layer, mqk, mvt, mot, gt, gate_bias, norm_g):
    span = ML_CHUNKS_PER_STEP * ML_CHUNK
    tok = lambda width: pl.BlockSpec((BATCH, span, width), lambda c: (0, c, 0))
    feat = lambda height: pl.BlockSpec((BATCH, height, span), lambda c: (0, 0, c))
    rows_all = BATCH * GATE_ROWS
    return pl.pallas_call(
        _mlstm_kernel,
        grid=(ML_STEPS,),
        in_specs=[
            tok(2 * MLSTM_WIDTH), feat(MLSTM_WIDTH), feat(MLSTM_WIDTH), feat(GATE_ROWS),
            pl.BlockSpec((BATCH, GATE_ROWS, span),
                         lambda c: (0, 0, jnp.minimum(c + 1, ML_STEPS - 1))),
            _resident((rows_all, ML_CHUNK), layer),
            _resident((MLSTM_WIDTH, ML_CHUNK), layer),
        ],
        out_specs=tok(MLSTM_WIDTH),
        out_shape=jax.ShapeDtypeStruct((BATCH, SEQ, MLSTM_WIDTH), BF16),
        scratch_shapes=[
            pltpu.VMEM((ML_UNITS, MLSTM_HEAD_DIM, MLSTM_HEAD_DIM), F32),
            pltpu.VMEM((rows_all, MLSTM_HEAD_DIM), F32),
            pltpu.VMEM((rows_all, ML_CHUNK), F32),
            pltpu.VMEM((ML_CHUNKS_PER_STEP, GP_PLANES, rows_all, ML_CHUNK), F32),
        ],
        compiler_params=pltpu.CompilerParams(
            dimension_semantics=("arbitrary",), vmem_limit_bytes=VMEM_LIMIT["mlstm"]),
        name="mlstm",
    )(mqk, mvt, mot, gt, gt, gate_bias, norm_g)


def _merge_ffn_kernel(x_ref, att_ref, ml_ref, gate_ref, wa_ref, wm_ref, wo_ref,
                      g_ref, wu_ref, wd_ref, fg_ref, o_ref, *, final_norm):
    ga = gate_ref[:, :D_MODEL].astype(F32)
    gm = gate_ref[:, D_MODEL:].astype(F32)
    y = (jax.nn.sigmoid(ga) * _dot(att_ref[...], wa_ref[...])
         + jax.nn.sigmoid(gm) * _dot(ml_ref[...], wm_ref[...]))
    x = x_ref[...] + _dot(y.astype(BF16), wo_ref[...])
    hn = _rms_norm(x, g_ref[...]).astype(BF16)
    acc = x
    for j in range(D_FF // FF_TILE):
        cols = slice(j * FF_TILE, (j + 1) * FF_TILE)
        h = jnp.maximum(_dot(hn, wu_ref[:, cols]), 0.0)
        acc = acc + _dot((h * h).astype(BF16), wd_ref[cols, :])
    if final_norm:
        acc = _rms_norm(acc, fg_ref[...])
    o_ref[...] = acc


def _merge_ffn(layer, x2d, att, ml, gates, wa, wm, wo, g, wu, wd, final_g):
    tm = TOKEN_TILE
    row = lambda width: pl.BlockSpec((tm, width), lambda i: (i, 0))
    return pl.pallas_call(
        functools.partial(_merge_ffn_kernel, final_norm=(layer == DEPTH - 1)),
        grid=(TOKENS // tm,),
        in_specs=[row(D_MODEL), row(ATT_WIDTH), row(MLSTM_WIDTH), row(2 * D_MODEL),
                  _resident((ATT_WIDTH, D_MODEL)), _resident((MLSTM_WIDTH, D_MODEL)),
                  _resident((D_MODEL, D_MODEL)), _resident((1, D_MODEL), layer),
                  _resident((D_MODEL, D_FF)), _resident((D_FF, D_MODEL)),
                  _resident((1, D_MODEL))],
        out_specs=row(D_MODEL),
        out_shape=jax.ShapeDtypeStruct((TOKENS, D_MODEL), F32),
        compiler_params=pltpu.CompilerParams(
            dimension_semantics=("arbitrary",), vmem_limit_bytes=VMEM_LIMIT["merge_ffn"]),
        name="merge_ffn",
    )(x2d, att, ml, gates, wa, wm, wo, g, wu, wd, final_g)


def _rel_bias_rows(rel_bias):
    lead = rel_bias.shape[:-1]
    far = jnp.broadcast_to(rel_bias[..., 2 * MAX_REL_DIST:], lead + (MAX_REL_DIST,))
    near = rel_bias[..., MAX_REL_DIST + 1 - Q_TILE:][..., ::-1]
    rest = jnp.broadcast_to(rel_bias[..., 2 * MAX_REL_DIST:],
                            lead + (REL_ROW - MAX_REL_DIST - near.shape[-1],))
    return jnp.concatenate([far, near, rest], axis=-1).astype(F32) * LOG2E


def kernel(x, mix_norm_g, w_in, conv_w, conv_b, b_igate, b_fgate, rel_bias, mh_norm_g,
           w_att_proj, w_mlstm_proj, w_out, ffn_norm_g, w_up, w_down, final_norm_g):
    w_t = jnp.swapaxes(w_in, 1, 2).astype(BF16)
    w_gates = w_t[:, IN_GATES:]
    w_gate_t = w_t[:, IN_MAIN_COLS:IN_GATES]
    gate_bias = jnp.tile(
        jnp.broadcast_to(jnp.concatenate([b_igate, b_fgate], axis=1)[:, :, None],
                         (DEPTH, GATE_ROWS, ML_CHUNK)), (1, BATCH, 1))
    norm_g = jnp.broadcast_to(mh_norm_g[:, :, None], (DEPTH, MLSTM_WIDTH, ML_CHUNK))
    rel_rows = _rel_bias_rows(rel_bias)
    mix_g = mix_norm_g.reshape(DEPTH, 1, D_MODEL)
    ffn_g = ffn_norm_g.reshape(DEPTH, 1, D_MODEL)
    conv_bias = conv_b.reshape(DEPTH, 1, 2 * MLSTM_WIDTH)
    final_g = final_norm_g.reshape(1, D_MODEL)

    h = x.reshape(TOKENS, D_MODEL)
    for l in range(DEPTH):
        qt, k_tok, vt_tiles, mqk, mvt, mot, gates, gt, wa, wm, wo, wu, wd = _in_proj(
            l, h, mix_g, w_t, w_gates, w_gate_t, conv_w, conv_bias,
            w_att_proj, w_mlstm_proj, w_out, w_up, w_down)
        att = _band_attn(l, qt, k_tok, vt_tiles, rel_rows)
        ml = _mlstm(l, mqk.reshape(BATCH, SEQ, 2 * MLSTM_WIDTH), mvt, mot, gt, gate_bias, norm_g)
        h = _merge_ffn(l, h, att.reshape(TOKENS, ATT_WIDTH), ml.reshape(TOKENS, MLSTM_WIDTH),
                       gates, wa, wm, wo, ffn_g, wu, wd, final_g)
    return h.reshape(BATCH, SEQ, D_MODEL)
```

```python
import functools
import math

import jax
import jax.numpy as jnp
from jax import lax
from jax.experimental import pallas as pl
from jax.experimental.pallas import tpu as pltpu

D_MODEL = 1024
BATCH = 4
SEQ = 4096
DEPTH = 2
CHUNK = 64
NORM_EPS = 1e-6
ATT_HEADS = 8
ATT_HEAD_DIM = 64
ATT_WIDTH = ATT_HEADS * ATT_HEAD_DIM
LEFT_CHUNKS = 8
BAND_CHUNKS = LEFT_CHUNKS + 1
BAND = BAND_CHUNKS * CHUNK
MAX_REL_DIST = 256
MLSTM_HEADS = 4
MLSTM_HEAD_DIM = 128
MLSTM_WIDTH = MLSTM_HEADS * MLSTM_HEAD_DIM
CONV_WIDTH = 4
D_FF = 4 * D_MODEL

LANES = 128
SUBLANES = 8

TOKENS = BATCH * SEQ
LEFT_ROWS = LEFT_CHUNKS * CHUNK
IN_Q = 0
IN_K = IN_Q + ATT_WIDTH
IN_V = IN_K + ATT_WIDTH
IN_MQ = IN_V + ATT_WIDTH
IN_MK = IN_MQ + MLSTM_WIDTH
IN_MV = IN_MK + MLSTM_WIDTH
IN_MO = IN_MV + MLSTM_WIDTH
IN_MAIN_COLS = IN_MO + MLSTM_WIDTH
GATE_ROWS = 2 * MLSTM_HEADS
IN_GATES = IN_MAIN_COLS + GATE_ROWS

Q_TILE = 2 * CHUNK
WINDOW = BAND + CHUNK
WINDOW_TILES = WINDOW // LANES
ATT_TILES_PER_STEP = 8
ATT_STEPS = SEQ // (Q_TILE * ATT_TILES_PER_STEP)
REL_ROW = 1024
ONES_ROWS = ATT_HEAD_DIM
LOG2E = 1.4426950408889634
MASKED = -1e30

ML_CHUNK = LANES
ML_CHUNKS_PER_STEP = 2
ML_STEPS = SEQ // (ML_CHUNK * ML_CHUNKS_PER_STEP)
ML_UNITS = BATCH * MLSTM_HEADS
CONV_TAIL = 16
CONV_ROWS = 16
CONV_COLS = 128
SPLIT_ROWS = 16

PROJ_TILE = 1024
TOKEN_TILE = 512
FF_TILE = 1024

F32 = jnp.float32
BF16 = jnp.bfloat16
MIB = 1024 * 1024
VMEM_LIMIT = {"in_proj": 58 * MIB, "band_attn": 56 * MIB, "mlstm": 56 * MIB, "merge_ffn": 56 * MIB}


def _resident(shape, layer=None):
    nd = len(shape)
    if layer is None:
        return pl.BlockSpec(shape, lambda *_: (0,) * nd, pipeline_mode=pl.Buffered(1))
    return pl.BlockSpec((None,) + tuple(shape), lambda *_: (layer,) + (0,) * nd,
                        pipeline_mode=pl.Buffered(1))


def _rms_norm(x, g):
    ms = jnp.mean(x * x, axis=-1, keepdims=True)
    return x * lax.rsqrt(ms + NORM_EPS) * g


def _dot(a, b):
    return jnp.dot(a, b, preferred_element_type=F32)


def _dot_nt(a, b):
    return lax.dot_general(a, b, (((1,), (1,)), ((), ())), preferred_element_type=F32)


def _dot_tn(a, b):
    return lax.dot_general(a, b, (((0,), (0,)), ((), ())), preferred_element_type=F32)


def _in_proj_kernel(x_ref, g_ref, w_ref, wgate_ref, wg_ref, cw_ref, cb_ref,
                    wa_ref, wm_ref, wo_ref, wu_ref, wd_ref,
                    qt_ref, k_ref, vt_ref, mqk_ref, mvt_ref, mot_ref, gate_ref, gt_ref,
                    wa_out, wm_out, wo_out, wu_out, wd_out,
                    conv_scr):
    a = ATT_WIDTH
    m = MLSTM_WIDTH

    @pl.when(pl.program_id(0) % (SEQ // PROJ_TILE) == 0)
    def _():
        conv_scr[:CONV_TAIL, :] = jnp.zeros((CONV_TAIL, 2 * m), F32)

    for src_ref, dst_ref in ((wa_ref, wa_out), (wm_ref, wm_out), (wo_ref, wo_out),
                             (wu_ref, wu_out), (wd_ref, wd_out)):
        dst_ref[...] = src_ref[...].astype(BF16)

    xn = _rms_norm(x_ref[...], g_ref[...]).astype(BF16)

    def seg(lo, width):
        return _dot_nt(xn, w_ref[lo:lo + width, :])

    def seg_t(lo, width):
        return _dot_nt(w_ref[lo:lo + width, :], xn)

    conv_scr[CONV_TAIL:, :m] = seg(IN_MQ, m)
    conv_scr[CONV_TAIL:, m:] = seg(IN_MK, m)
    for c0 in range(0, 2 * m, CONV_COLS):
        cols = slice(c0, c0 + CONV_COLS)
        post_scale = 1.0 / math.sqrt(MLSTM_HEAD_DIM) if c0 >= m else None
        for r0 in range(0, PROJ_TILE, CONV_ROWS):
            lo = CONV_TAIL - SUBLANES + r0
            ext = conv_scr[lo:lo + SUBLANES + CONV_ROWS, cols]
            acc = cb_ref[:, cols] + ext[SUBLANES:] * cw_ref[CONV_WIDTH - 1:CONV_WIDTH, cols]
            for d in range(1, CONV_WIDTH):
                tap = CONV_WIDTH - 1 - d
                acc = acc + pltpu.roll(ext, d, axis=0)[SUBLANES:] * cw_ref[tap:tap + 1, cols]
            half = 0.5 * acc
            act = half + half * jnp.tanh(half)
            if post_scale is not None:
                act = act * post_scale
            mqk_ref[r0:r0 + CONV_ROWS, cols] = act.astype(BF16)
    conv_scr[:CONV_TAIL, :] = conv_scr[PROJ_TILE:, :]

    qt_ref[...] = (seg_t(IN_Q, a) * (LOG2E / math.sqrt(ATT_HEAD_DIM))).astype(BF16)
    vt = seg_t(IN_V, a).astype(BF16)
    for j in range(PROJ_TILE // LANES):
        vt_ref[0, j] = vt[:, j * LANES:(j + 1) * LANES]
    mv_mo = seg_t(IN_MV, 2 * m).astype(BF16)
    mvt_ref[0] = mv_mo[:m]
    mot_ref[0] = mv_mo[m:]
    gt_ref[0] = _dot_nt(wg_ref[...], xn)
    k_ref[0] = seg(IN_K, a).astype(BF16)
    for j in range(2 * D_MODEL // 512):
        cols = slice(j * 512, (j + 1) * 512)
        gate_ref[:, cols] = _dot_nt(xn, wgate_ref[cols, :]).astype(BF16)


def _in_proj(layer, x2d, g, w_main, w_gates, w_gate_t, conv_w, conv_b, wa, wm, wo, wu, wd):
    steps = TOKENS // PROJ_TILE

    def slab(rows, width):
        return (pl.BlockSpec((None, rows // steps, width), lambda i: (layer, i, 0)),
                pl.BlockSpec((rows // steps, width), lambda i: (i, 0)),
                jax.ShapeDtypeStruct((rows, width), BF16))

    slabs = [slab(ATT_WIDTH, D_MODEL), slab(MLSTM_WIDTH, D_MODEL), slab(D_MODEL, D_MODEL),
             slab(D_MODEL, D_FF), slab(D_FF, D_MODEL)]
    tm = PROJ_TILE
    tiles_per_batch = SEQ // tm
    lane_tiles = tm // LANES
    row = lambda width: pl.BlockSpec((tm, width), lambda i: (i, 0))
    col = lambda height: pl.BlockSpec((1, height, tm),
                                      lambda i: (i // tiles_per_batch, 0, i % tiles_per_batch))
    k_spec = pl.BlockSpec((1, tm, ATT_WIDTH),
                          lambda i: (i // tiles_per_batch, i % tiles_per_batch, 0))
    vt_spec = pl.BlockSpec((1, lane_tiles, ATT_WIDTH, LANES),
                           lambda i: (i // tiles_per_batch, i % tiles_per_batch, 0, 0))
    out_shape = (
        jax.ShapeDtypeStruct((ATT_WIDTH, TOKENS), BF16),
        jax.ShapeDtypeStruct((BATCH, SEQ, ATT_WIDTH), BF16),
        jax.ShapeDtypeStruct((BATCH, SEQ // LANES, ATT_WIDTH, LANES), BF16),
        jax.ShapeDtypeStruct((TOKENS, 2 * MLSTM_WIDTH), BF16),
        jax.ShapeDtypeStruct((BATCH, MLSTM_WIDTH, SEQ), BF16),
        jax.ShapeDtypeStruct((BATCH, MLSTM_WIDTH, SEQ), BF16),
        jax.ShapeDtypeStruct((TOKENS, 2 * D_MODEL), BF16),
        jax.ShapeDtypeStruct((BATCH, GATE_ROWS, SEQ), F32),
    ) + tuple(s[2] for s in slabs)
    return pl.pallas_call(
        _in_proj_kernel,
        grid=(steps,),
        in_specs=[
            row(D_MODEL),
            _resident((1, D_MODEL), layer),
            _resident((IN_MAIN_COLS, D_MODEL), layer),
            _resident((2 * D_MODEL, D_MODEL), layer),
            _resident((GATE_ROWS, D_MODEL), layer),
            _resident((CONV_WIDTH, 2 * MLSTM_WIDTH), layer),
            _resident((1, 2 * MLSTM_WIDTH), layer),
        ] + [s[0] for s in slabs],
        out_specs=(pl.BlockSpec((ATT_WIDTH, tm), lambda i: (0, i)), k_spec, vt_spec,
                   row(2 * MLSTM_WIDTH), col(MLSTM_WIDTH), col(MLSTM_WIDTH), row(2 * D_MODEL),
                   col(GATE_ROWS)) + tuple(s[1] for s in slabs),
        out_shape=out_shape,
        scratch_shapes=[pltpu.VMEM((CONV_TAIL + tm, 2 * MLSTM_WIDTH), F32)],
        compiler_params=pltpu.CompilerParams(
            dimension_semantics=("arbitrary",), vmem_limit_bytes=VMEM_LIMIT["in_proj"]),
        name="in_proj",
    )(x2d, g, w_main, w_gates, w_gate_t, conv_w, conv_b, wa, wm, wo, wu, wd)


def _build_bias_table(rel_ref, bias_scr):
    kb = lax.broadcasted_iota(jnp.int32, (WINDOW, LANES), 0)
    lane = lax.broadcasted_iota(jnp.int32, (WINDOW, LANES), 1)
    band_lo = jnp.where(lane >= CHUNK, CHUNK, 0)
    in_band = (kb >= band_lo) & (kb < band_lo + BAND)
    for h in range(ATT_HEADS):
        rows = jnp.broadcast_to(rel_ref[h:h + 1, :], (LANES, REL_ROW))
        shifted = pltpu.roll(rows, 0, axis=1, stride=1, stride_axis=0)
        table = jnp.concatenate(
            [jnp.transpose(shifted[:, j * LANES:(j + 1) * LANES]) for j in range(WINDOW_TILES)],
            axis=0)
        bias_scr[h, :WINDOW, :] = jnp.where(in_band, table, MASKED)
        bias_scr[h, WINDOW:, :] = jnp.full((LEFT_ROWS, LANES), MASKED, F32)


def _band_attn_kernel(qt_ref, k_ref, vt_ref, rel_ref, o_ref, bias_scr):
    step = pl.program_id(1)

    @pl.when((pl.program_id(0) == 0) & (step == 0))
    def _():
        _build_bias_table(rel_ref, bias_scr)

    d = ATT_HEAD_DIM
    zeros_half = jnp.zeros((d, Q_TILE), BF16)
    ones_rows = jnp.ones((ONES_ROWS, LANES), BF16)
    for sub in range(ATT_TILES_PER_STEP):
        tile = step * ATT_TILES_PER_STEP + sub
        q_lanes = slice(sub * Q_TILE, (sub + 1) * Q_TILE)
        first_tile = jnp.maximum(tile - LEFT_ROWS // Q_TILE, 0)
        start = pl.multiple_of(first_tile * Q_TILE, Q_TILE)
        shift = pl.multiple_of(LEFT_ROWS - (tile - first_tile) * Q_TILE, Q_TILE)
        outs = []
        for h in range(ATT_HEADS):
            pair_dims = slice((h // 2) * 2 * d, (h // 2 + 1) * 2 * d)
            qh = qt_ref[h * d:(h + 1) * d, q_lanes]
            qm = jnp.concatenate([qh, zeros_half] if h % 2 == 0 else [zeros_half, qh], axis=0)
            tile_max, tile_out = [], []
            for j in range(WINDOW_TILES):
                keys = k_ref[0, pl.ds(start + j * LANES, LANES), pair_dims]
                st = _dot(keys, qm) + bias_scr[h, pl.ds(shift + j * LANES, LANES), :]
                m_j = jnp.max(st, axis=0, keepdims=True)
                p_j = jnp.exp2(st - m_j).astype(BF16)
                v_ones = jnp.concatenate(
                    [vt_ref[0, first_tile + j, h * d:(h + 1) * d, :], ones_rows], axis=0)
                tile_max.append(m_j)
                tile_out.append(_dot(v_ones, p_j))
            m = functools.reduce(jnp.maximum, tile_max)
            acc = functools.reduce(jnp.add, [jnp.exp2(m_j - m) * o
                                             for m_j, o in zip(tile_max, tile_out)])
            outs.append(acc[:d] / acc[d:d + 1])
        o_ref[0, q_lanes, :] = jnp.transpose(jnp.concatenate(outs, axis=0)).astype(BF16)


def _band_attn(layer, qt, k_tok, vt_tiles, rel_rows):
    return pl.pallas_call(
        _band_attn_kernel,
        grid=(BATCH, ATT_STEPS),
        in_specs=[
            pl.BlockSpec((ATT_WIDTH, ATT_TILES_PER_STEP * Q_TILE),
                         lambda b, s: (0, b * ATT_STEPS + s)),
            pl.BlockSpec((1, SEQ, ATT_WIDTH), lambda b, s: (b, 0, 0)),
            pl.BlockSpec((1, SEQ // LANES, ATT_WIDTH, LANES), lambda b, s: (b, 0, 0, 0)),
            _resident((ATT_HEADS, REL_ROW), layer),
        ],
        out_specs=pl.BlockSpec((1, ATT_TILES_PER_STEP * Q_TILE, ATT_WIDTH),
                               lambda b, s: (b, s, 0)),
        out_shape=jax.ShapeDtypeStruct((BATCH, SEQ, ATT_WIDTH), BF16),
        scratch_shapes=[pltpu.VMEM((ATT_HEADS, WINDOW + LEFT_ROWS, LANES), F32)],
        compiler_params=pltpu.CompilerParams(
            dimension_semantics=("arbitrary", "arbitrary"),
            vmem_limit_bytes=VMEM_LIMIT["band_attn"]),
        name="band_attn",
    )(qt, k_tok, vt_tiles, rel_rows)


def _scan_lanes(x, op, identity, segment):
    pos = lax.broadcasted_iota(jnp.int32, x.shape, 1) % segment
    d = 1
    while d < segment:
        x = op(x, jnp.where(pos >= d, pltpu.roll(x, d, axis=1), identity))
        d *= 2
    return x


def _split3(x):
    hi = x.astype(BF16).astype(F32)
    r1 = x - hi
    mid = r1.astype(BF16).astype(F32)
    lo = (r1 - mid).astype(BF16).astype(F32)
    return hi, mid, lo


GP_BCUM, GP_STAB, GP_W, GP_BLAST, GP_MLOC, GP_R_HI, GP_R_MID, GP_R_LO = range(8)
GP_PLANES = 8


def _gate_planes(gt_ref, lanes, gb_ref, planes_ref):
    rows_all = BATCH * GATE_ROWS
    gates = jnp.concatenate([gt_ref[b, :, lanes] for b in range(BATCH)], axis=0) + gb_ref[...]
    logf = jnp.minimum(gates, 0.0) - jnp.log1p(jnp.exp(-jnp.abs(gates)))
    bcum = pltpu.roll(_scan_lanes(logf, jnp.add, 0.0, ML_CHUNK), rows_all - MLSTM_HEADS, axis=0)
    b_last = jnp.broadcast_to(bcum[:, ML_CHUNK - 1:ML_CHUNK], (rows_all, ML_CHUNK))
    a_row = b_last - bcum + gates
    m_loc = jnp.broadcast_to(jnp.max(a_row, axis=1, keepdims=True), (rows_all, ML_CHUNK))
    r_row = gates - bcum
    planes_ref[GP_BCUM] = bcum
    planes_ref[GP_STAB] = bcum + _scan_lanes(r_row, jnp.maximum, -jnp.inf, ML_CHUNK)
    planes_ref[GP_W] = jnp.exp(a_row - m_loc)
    planes_ref[GP_BLAST] = b_last
    planes_ref[GP_MLOC] = m_loc
    for plane, part in zip((GP_R_HI, GP_R_MID, GP_R_LO), _split3(r_row)):
        planes_ref[plane] = part


def _mlstm_kernel(mqk_ref, vt_ref, mot_ref, gt_ref, gt_next_ref, gb_ref, ng_ref, o_ref,
                  c_scr, n_scr, m_scr, gp_scr):
    step = pl.program_id(0)

    @pl.when(step == 0)
    def _():
        c_scr[...] = jnp.zeros_like(c_scr)
        n_scr[...] = jnp.zeros_like(n_scr)
        m_scr[...] = jnp.zeros_like(m_scr)
        for sub in range(ML_CHUNKS_PER_STEP):
            _gate_planes(gt_ref, slice(sub * ML_CHUNK, (sub + 1) * ML_CHUNK), gb_ref,
                         gp_scr.at[sub])

    for sub in range(ML_CHUNKS_PER_STEP):
        _mlstm_chunk(slice(sub * ML_CHUNK, (sub + 1) * ML_CHUNK), mqk_ref, vt_ref, mot_ref,
                     gp_scr.at[sub], ng_ref, o_ref, c_scr, n_scr, m_scr)
    for sub in range(ML_CHUNKS_PER_STEP):
        _gate_planes(gt_next_ref, slice(sub * ML_CHUNK, (sub + 1) * ML_CHUNK), gb_ref,
                     gp_scr.at[sub])


def _mlstm_chunk(time, mqk_ref, vt_ref, mot_ref, gp_ref, ng_ref, o_ref, c_scr, n_scr, m_scr):
    hd = MLSTM_HEAD_DIM

    bcum = gp_ref[GP_BCUM]
    b_last = gp_ref[GP_BLAST]
    m_loc = gp_ref[GP_MLOC]
    w_row = gp_ref[GP_W]
    m_prev = m_scr[...]
    m_new = jnp.maximum(b_last + m_prev, m_loc)
    s_prev = jnp.exp(b_last + m_prev - m_new)
    s_loc = jnp.exp(m_loc - m_new)
    m_scr[...] = m_new
    g_row = bcum + m_prev
    m_t = jnp.maximum(g_row, gp_ref[GP_STAB])
    inter = jnp.exp(g_row - m_t)
    inv_floor = jnp.exp(-m_t)
    n_prev = n_scr[...]
    w_bf = w_row.astype(BF16)
    n_prev_bf = n_prev.astype(BF16)
    r_parts = (gp_ref[GP_R_HI], gp_ref[GP_R_MID], gp_ref[GP_R_LO])
    e_parts = _split3(bcum - m_t)
    split_row = lax.broadcasted_iota(jnp.int32, (SPLIT_ROWS, ML_CHUNK), 0)

    def outer_sum_operand(parts, one, first):
        other = 3 - first
        out = jnp.where((split_row >= other) & (split_row < other + 3), 1.0, 0.0)
        for i, part in enumerate(parts):
            out = jnp.where(split_row == first + i, part[one], out)
        return out.astype(BF16)

    src = lax.broadcasted_iota(jnp.int32, (ML_CHUNK, ML_CHUNK), 0)
    qry = lax.broadcasted_iota(jnp.int32, (ML_CHUNK, ML_CHUNK), 1)
    causal = src <= qry
    new_n = {}
    outs = {}

    def stage_matmuls(u):
        b, h = divmod(u, MLSTM_HEADS)
        one = slice(b * GATE_ROWS + h, b * GATE_ROWS + h + 1)
        grp = slice(b * GATE_ROWS, (b + 1) * GATE_ROWS)
        sl = slice(h * hd, (h + 1) * hd)
        qh = mqk_ref[b, time, sl]
        kh = mqk_ref[b, time, MLSTM_WIDTH + h * hd:MLSTM_WIDTH + (h + 1) * hd]
        vt = vt_ref[b, sl, time]
        c_prev = c_scr[u]
        lhs = outer_sum_operand(r_parts, one, 0)
        rhs = outer_sum_operand(e_parts, one, 3)
        return dict(
            one=one, sl=sl, b=b, h=h, vt=vt, c_prev=c_prev,
            c_loc=_dot((vt.astype(F32) * w_row[one]).astype(BF16), kh),
            n_loc=_dot(w_bf[grp], kh)[h:h + 1],
            exponent=_dot_tn(lhs, rhs),
            scores=_dot_nt(kh, qh),
            nq=_dot_nt(n_prev_bf[grp], qh)[h:h + 1],
            cq=_dot_nt(c_prev.astype(BF16), qh))

    def stage_decay(u, st):
        one = st["one"]
        c_scr[u] = s_prev[one] * st["c_prev"] + s_loc[one] * st["c_loc"]
        new_n[u] = s_prev[one] * n_prev[one] + s_loc[one] * st["n_loc"]
        pt = st["scores"] * jnp.exp(jnp.where(causal, st["exponent"], -jnp.inf))
        st["den"] = inter[one] * st["nq"] + jnp.sum(pt, axis=0, keepdims=True)
        st["pt"] = pt.astype(BF16)

    def stage_intra(u, st):
        st["intra"] = _dot(st["vt"], st["pt"])

    def stage_out(u, st):
        one, sl, b = st["one"], st["sl"], st["b"]
        num = inter[one] * st["cq"] + st["intra"]
        hout = num * (1.0 / jnp.maximum(jnp.abs(st["den"]), inv_floor[one]))
        ms = jnp.mean(hout * hout, axis=0, keepdims=True)
        y = hout * lax.rsqrt(ms + NORM_EPS) * ng_ref[sl, :]
        outs[u] = jax.nn.sigmoid(mot_ref[b, sl, time].astype(F32)) * y
        if st["h"] == MLSTM_HEADS - 1:
            heads = [outs.pop(b * MLSTM_HEADS + i) for i in range(MLSTM_HEADS)]
            o_ref[b, time, :] = jnp.transpose(jnp.concatenate(heads, axis=0)).astype(BF16)

    stages = ((stage_decay, 2), (stage_intra, 3), (stage_out, 5))
    state = {}
    for k in range(ML_UNITS + stages[-1][1]):
        if k < ML_UNITS:
            state[k] = stage_matmuls(k)
        for stage, lag in stages:
            if 0 <= k - lag < ML_UNITS:
                stage(k - lag, state[k - lag])

    pad = jnp.zeros((GATE_ROWS - MLSTM_HEADS, hd), F32)
    n_scr[...] = jnp.concatenate(
        [row for b in range(BATCH)
         for row in [new_n[b * MLSTM_HEADS + i] for i in range(MLSTM_HEADS)] + [pad]], axis=0)


def _mlstm(layer, mqk, mvt, mot, gt, gate_bias, norm_g):
    span = ML_CHUNKS_PER_STEP * ML_CHUNK
    tok = lambda width: pl.BlockSpec((BATCH, span, width), lambda c: (0, c, 0))
    feat = lambda height: pl.BlockSpec((BATCH, height, span), lambda c: (0, 0, c))
    rows_all = BATCH * GATE_ROWS
    return pl.pallas_call(
        _mlstm_kernel,
        grid=(ML_STEPS,),
        in_specs=[
            tok(2 * MLSTM_WIDTH), feat(MLSTM_WIDTH), feat(MLSTM_WIDTH), feat(GATE_ROWS),
            pl.BlockSpec((BATCH, GATE_ROWS, span),
                         lambda c: (0, 0, jnp.minimum(c + 1, ML_STEPS - 1))),
            _resident((rows_all, ML_CHUNK), layer),
            _resident((MLSTM_WIDTH, ML_CHUNK), layer),
        ],
        out_specs=tok(MLSTM_WIDTH),
        out_shape=jax.ShapeDtypeStruct((BATCH, SEQ, MLSTM_WIDTH), BF16),
        scratch_shapes=[
            pltpu.VMEM((ML_UNITS, MLSTM_HEAD_DIM, MLSTM_HEAD_DIM), F32),
            pltpu.VMEM((rows_all, MLSTM_HEAD_DIM), F32),
            pltpu.VMEM((rows_all, ML_CHUNK), F32),
            pltpu.VMEM((ML_CHUNKS_PER_STEP, GP_PLANES, rows_all, ML_CHUNK), F32),
        ],
        compiler_params=pltpu.CompilerParams(
            dimension_semantics=("arbitrary",), vmem_limit_bytes=VMEM_LIMIT["mlstm"]),
        name="mlstm",
    )(mqk, mvt, mot, gt, gt, gate_bias, norm_g)


def _merge_ffn_kernel(x_ref, att_ref, ml_ref, gate_ref, wa_ref, wm_ref, wo_ref,
                      g_ref, wu_ref, wd_ref, fg_ref, o_ref, *, final_norm):
    ga = gate_ref[:, :D_MODEL].astype(F32)
    gm = gate_ref[:, D_MODEL:].astype(F32)
    y = (jax.nn.sigmoid(ga) * _dot(att_ref[...], wa_ref[...])
         + jax.nn.sigmoid(gm) * _dot(ml_ref[...], wm_ref[...]))
    x = x_ref[...] + _dot(y.astype(BF16), wo_ref[...])
    hn = _rms_norm(x, g_ref[...]).astype(BF16)
    acc = x
    for j in range(D_FF // FF_TILE):
        cols = slice(j * FF_TILE, (j + 1) * FF_TILE)
        h = jnp.maximum(_dot(hn, wu_ref[:, cols]), 0.0)
        acc = acc + _dot((h * h).astype(BF16), wd_ref[cols, :])
    if final_norm:
        acc = _rms_norm(acc, fg_ref[...])
    o_ref[...] = acc


def _merge_ffn(layer, x2d, att, ml, gates, wa, wm, wo, g, wu, wd, final_g):
    tm = TOKEN_TILE
    row = lambda width: pl.BlockSpec((tm, width), lambda i: (i, 0))
    return pl.pallas_call(
        functools.partial(_merge_ffn_kernel, final_norm=(layer == DEPTH - 1)),
        grid=(TOKENS // tm,),
        in_specs=[row(D_MODEL), row(ATT_WIDTH), row(MLSTM_WIDTH), row(2 * D_MODEL),
                  _resident((ATT_WIDTH, D_MODEL)), _resident((MLSTM_WIDTH, D_MODEL)),
                  _resident((D_MODEL, D_MODEL)), _resident((1, D_MODEL), layer),
                  _resident((D_MODEL, D_FF)), _resident((D_FF, D_MODEL)),
                  _resident((1, D_MODEL))],
        out_specs=row(D_MODEL),
        out_shape=jax.ShapeDtypeStruct((TOKENS, D_MODEL), F32),
        compiler_params=pltpu.CompilerParams(
            dimension_semantics=("arbitrary",), vmem_limit_bytes=VMEM_LIMIT["merge_ffn"]),
        name="merge_ffn",
    )(x2d, att, ml, gates, wa, wm, wo, g, wu, wd, final_g)


def _rel_bias_rows(rel_bias):
    lead = rel_bias.shape[:-1]
    far = jnp.broadcast_to(rel_bias[..., 2 * MAX_REL_DIST:], lead + (MAX_REL_DIST,))
    near = rel_bias[..., MAX_REL_DIST + 1 - Q_TILE:][..., ::-1]
    rest = jnp.broadcast_to(rel_bias[..., 2 * MAX_REL_DIST:],
                            lead + (REL_ROW - MAX_REL_DIST - near.shape[-1],))
    return jnp.concatenate([far, near, rest], axis=-1).astype(F32) * LOG2E


def kernel(x, mix_norm_g, w_in, conv_w, conv_b, b_igate, b_fgate, rel_bias, mh_norm_g,
           w_att_proj, w_mlstm_proj, w_out, ffn_norm_g, w_up, w_down, final_norm_g):
    w_t = jnp.swapaxes(w_in, 1, 2).astype(BF16)
    w_gates = w_t[:, IN_GATES:]
    w_gate_t = w_t[:, IN_MAIN_COLS:IN_GATES]
    gate_bias = jnp.tile(
        jnp.broadcast_to(jnp.concatenate([b_igate, b_fgate], axis=1)[:, :, None],
                         (DEPTH, GATE_ROWS, ML_CHUNK)), (1, BATCH, 1))
    norm_g = jnp.broadcast_to(mh_norm_g[:, :, None], (DEPTH, MLSTM_WIDTH, ML_CHUNK))
    rel_rows = _rel_bias_rows(rel_bias)
    mix_g = mix_norm_g.reshape(DEPTH, 1, D_MODEL)
    ffn_g = ffn_norm_g.reshape(DEPTH, 1, D_MODEL)
    conv_bias = conv_b.reshape(DEPTH, 1, 2 * MLSTM_WIDTH)
    final_g = final_norm_g.reshape(1, D_MODEL)

    h = x.reshape(TOKENS, D_MODEL)
    for l in range(DEPTH):
        qt, k_tok, vt_tiles, mqk, mvt, mot, gates, gt, wa, wm, wo, wu, wd = _in_proj(
            l, h, mix_g, w_t, w_gates, w_gate_t, conv_w, conv_bias,
            w_att_proj, w_mlstm_proj, w_out, w_up, w_down)
        att = _band_attn(l, qt, k_tok, vt_tiles, rel_rows)
        ml = _mlstm(l, mqk.reshape(BATCH, SEQ, 2 * MLSTM_WIDTH), mvt, mot, gt, gate_bias, norm_g)
        h = _merge_ffn(l, h, att.reshape(TOKENS, ATT_WIDTH), ml.reshape(TOKENS, MLSTM_WIDTH),
                       gates, wa, wm, wo, ffn_g, wu, wd, final_g)
    return h.reshape(BATCH, SEQ, D_MODEL)
```

```python
import functools
import math

import jax
import jax.numpy as jnp
from jax import lax
from jax.experimental import pallas as pl
from jax.experimental.pallas import tpu as pltpu

D_MODEL = 1024
BATCH = 4
SEQ = 4096
DEPTH = 2
CHUNK = 64
NORM_EPS = 1e-6
ATT_HEADS = 8
ATT_HEAD_DIM = 64
ATT_WIDTH = ATT_HEADS * ATT_HEAD_DIM
LEFT_CHUNKS = 8
BAND_CHUNKS = LEFT_CHUNKS + 1
BAND = BAND_CHUNKS * CHUNK
MAX_REL_DIST = 256
MLSTM_HEADS = 4
MLSTM_HEAD_DIM = 128
MLSTM_WIDTH = MLSTM_HEADS * MLSTM_HEAD_DIM
CONV_WIDTH = 4
D_FF = 4 * D_MODEL

LANES = 128
SUBLANES = 8

TOKENS = BATCH * SEQ
LEFT_ROWS = LEFT_CHUNKS * CHUNK
IN_Q = 0
IN_K = IN_Q + ATT_WIDTH
IN_V = IN_K + ATT_WIDTH
IN_MQ = IN_V + ATT_WIDTH
IN_MK = IN_MQ + MLSTM_WIDTH
IN_MV = IN_MK + MLSTM_WIDTH
IN_MO = IN_MV + MLSTM_WIDTH
IN_MAIN_COLS = IN_MO + MLSTM_WIDTH
GATE_ROWS = 2 * MLSTM_HEADS
IN_GATES = IN_MAIN_COLS + GATE_ROWS

Q_TILE = 2 * CHUNK
WINDOW = BAND + CHUNK
WINDOW_TILES = WINDOW // LANES
ATT_TILES_PER_STEP = 8
ATT_STEPS = SEQ // (Q_TILE * ATT_TILES_PER_STEP)
REL_ROW = 1024
ONES_ROWS = ATT_HEAD_DIM
LOG2E = 1.4426950408889634
MASKED = -1e30

ML_CHUNK = LANES
ML_CHUNKS_PER_STEP = 4
ML_STEPS = SEQ // (ML_CHUNK * ML_CHUNKS_PER_STEP)
ML_UNITS = BATCH * MLSTM_HEADS
CONV_TAIL = 16
CONV_ROWS = 16
CONV_COLS = 128
SPLIT_ROWS = 16

PROJ_TILE = 1024
TOKEN_TILE = 512
FF_TILE = 1024

F32 = jnp.float32
BF16 = jnp.bfloat16
MIB = 1024 * 1024
VMEM_LIMIT = {"in_proj": 58 * MIB, "band_attn": 56 * MIB, "mlstm": 56 * MIB, "merge_ffn": 56 * MIB}


def _resident(shape, layer=None):
    nd = len(shape)
    if layer is None:
        return pl.BlockSpec(shape, lambda *_: (0,) * nd, pipeline_mode=pl.Buffered(1))
    return pl.BlockSpec((None,) + tuple(shape), lambda *_: (layer,) + (0,) * nd,
                        pipeline_mode=pl.Buffered(1))


def _rms_norm(x, g):
    ms = jnp.mean(x * x, axis=-1, keepdims=True)
    return x * lax.rsqrt(ms + NORM_EPS) * g


def _dot(a, b):
    return jnp.dot(a, b, preferred_element_type=F32)


def _dot_nt(a, b):
    return lax.dot_general(a, b, (((1,), (1,)), ((), ())), preferred_element_type=F32)


def _dot_tn(a, b):
    return lax.dot_general(a, b, (((0,), (0,)), ((), ())), preferred_element_type=F32)


def _in_proj_kernel(x_ref, g_ref, w_ref, wgate_ref, wg_ref, cw_ref, cb_ref,
                    wa_ref, wm_ref, wo_ref, wu_ref, wd_ref,
                    qt_ref, k_ref, vt_ref, mqk_ref, mvt_ref, mot_ref, gate_ref, gt_ref,
                    wa_out, wm_out, wo_out, wu_out, wd_out,
                    conv_scr):
    a = ATT_WIDTH
    m = MLSTM_WIDTH

    @pl.when(pl.program_id(0) % (SEQ // PROJ_TILE) == 0)
    def _():
        conv_scr[:CONV_TAIL, :] = jnp.zeros((CONV_TAIL, 2 * m), F32)

    for src_ref, dst_ref in ((wa_ref, wa_out), (wm_ref, wm_out), (wo_ref, wo_out),
                             (wu_ref, wu_out), (wd_ref, wd_out)):
        dst_ref[...] = src_ref[...].astype(BF16)

    xn = _rms_norm(x_ref[...], g_ref[...]).astype(BF16)

    def seg(lo, width):
        return _dot_nt(xn, w_ref[lo:lo + width, :])

    def seg_t(lo, width):
        return _dot_nt(w_ref[lo:lo + width, :], xn)

    conv_scr[CONV_TAIL:, :m] = seg(IN_MQ, m)
    conv_scr[CONV_TAIL:, m:] = seg(IN_MK, m)
    for c0 in range(0, 2 * m, CONV_COLS):
        cols = slice(c0, c0 + CONV_COLS)
        post_scale = 1.0 / math.sqrt(MLSTM_HEAD_DIM) if c0 >= m else None
        for r0 in range(0, PROJ_TILE, CONV_ROWS):
            lo = CONV_TAIL - SUBLANES + r0
            ext = conv_scr[lo:lo + SUBLANES + CONV_ROWS, cols]
            acc = cb_ref[:, cols] + ext[SUBLANES:] * cw_ref[CONV_WIDTH - 1:CONV_WIDTH, cols]
            for d in range(1, CONV_WIDTH):
                tap = CONV_WIDTH - 1 - d
                acc = acc + pltpu.roll(ext, d, axis=0)[SUBLANES:] * cw_ref[tap:tap + 1, cols]
            half = 0.5 * acc
            act = half + half * jnp.tanh(half)
            if post_scale is not None:
                act = act * post_scale
            mqk_ref[r0:r0 + CONV_ROWS, cols] = act.astype(BF16)
    conv_scr[:CONV_TAIL, :] = conv_scr[PROJ_TILE:, :]

    qt_ref[...] = (seg_t(IN_Q, a) * (LOG2E / math.sqrt(ATT_HEAD_DIM))).astype(BF16)
    vt = seg_t(IN_V, a).astype(BF16)
    for j in range(PROJ_TILE // LANES):
        vt_ref[0, j] = vt[:, j * LANES:(j + 1) * LANES]
    mv_mo = seg_t(IN_MV, 2 * m).astype(BF16)
    mvt_ref[0] = mv_mo[:m]
    mot_ref[0] = mv_mo[m:]
    gt_ref[0] = _dot_nt(wg_ref[...], xn)
    k_ref[0] = seg(IN_K, a).astype(BF16)
    for j in range(2 * D_MODEL // 512):
        cols = slice(j * 512, (j + 1) * 512)
        gate_ref[:, cols] = _dot_nt(xn, wgate_ref[cols, :]).astype(BF16)


def _in_proj(layer, x2d, g, w_main, w_gates, w_gate_t, conv_w, conv_b, wa, wm, wo, wu, wd):
    steps = TOKENS // PROJ_TILE

    def slab(rows, width):
        return (pl.BlockSpec((None, rows // steps, width), lambda i: (layer, i, 0)),
                pl.BlockSpec((rows // steps, width), lambda i: (i, 0)),
                jax.ShapeDtypeStruct((rows, width), BF16))

    slabs = [slab(ATT_WIDTH, D_MODEL), slab(MLSTM_WIDTH, D_MODEL), slab(D_MODEL, D_MODEL),
             slab(D_MODEL, D_FF), slab(D_FF, D_MODEL)]
    tm = PROJ_TILE
    tiles_per_batch = SEQ // tm
    lane_tiles = tm // LANES
    row = lambda width: pl.BlockSpec((tm, width), lambda i: (i, 0))
    col = lambda height: pl.BlockSpec((1, height, tm),
                                      lambda i: (i // tiles_per_batch, 0, i % tiles_per_batch))
    k_spec = pl.BlockSpec((1, tm, ATT_WIDTH),
                          lambda i: (i // tiles_per_batch, i % tiles_per_batch, 0))
    vt_spec = pl.BlockSpec((1, lane_tiles, ATT_WIDTH, LANES),
                           lambda i: (i // tiles_per_batch, i % tiles_per_batch, 0, 0))
    out_shape = (
        jax.ShapeDtypeStruct((ATT_WIDTH, TOKENS), BF16),
        jax.ShapeDtypeStruct((BATCH, SEQ, ATT_WIDTH), BF16),
        jax.ShapeDtypeStruct((BATCH, SEQ // LANES, ATT_WIDTH, LANES), BF16),
        jax.ShapeDtypeStruct((TOKENS, 2 * MLSTM_WIDTH), BF16),
        jax.ShapeDtypeStruct((BATCH, MLSTM_WIDTH, SEQ), BF16),
        jax.ShapeDtypeStruct((BATCH, MLSTM_WIDTH, SEQ), BF16),
        jax.ShapeDtypeStruct((TOKENS, 2 * D_MODEL), BF16),
        jax.ShapeDtypeStruct((BATCH, GATE_ROWS, SEQ), F32),
    ) + tuple(s[2] for s in slabs)
    return pl.pallas_call(
        _in_proj_kernel,
        grid=(steps,),
        in_specs=[
            row(D_MODEL),
            _resident((1, D_MODEL), layer),
            _resident((IN_MAIN_COLS, D_MODEL), layer),
            _resident((2 * D_MODEL, D_MODEL), layer),
            _resident((GATE_ROWS, D_MODEL), layer),
            _resident((CONV_WIDTH, 2 * MLSTM_WIDTH), layer),
            _resident((1, 2 * MLSTM_WIDTH), layer),
        ] + [s[0] for s in slabs],
        out_specs=(pl.BlockSpec((ATT_WIDTH, tm), lambda i: (0, i)), k_spec, vt_spec,
                   row(2 * MLSTM_WIDTH), col(MLSTM_WIDTH), col(MLSTM_WIDTH), row(2 * D_MODEL),
                   col(GATE_ROWS)) + tuple(s[1] for s in slabs),
        out_shape=out_shape,
        scratch_shapes=[pltpu.VMEM((CONV_TAIL + tm, 2 * MLSTM_WIDTH), F32)],
        compiler_params=pltpu.CompilerParams(
            dimension_semantics=("arbitrary",), vmem_limit_bytes=VMEM_LIMIT["in_proj"]),
        name="in_proj",
    )(x2d, g, w_main, w_gates, w_gate_t, conv_w, conv_b, wa, wm, wo, wu, wd)


def _build_bias_table(rel_ref, bias_scr):
    kb = lax.broadcasted_iota(jnp.int32, (WINDOW, LANES), 0)
    lane = lax.broadcasted_iota(jnp.int32, (WINDOW, LANES), 1)
    band_lo = jnp.where(lane >= CHUNK, CHUNK, 0)
    in_band = (kb >= band_lo) & (kb < band_lo + BAND)
    for h in range(ATT_HEADS):
        rows = jnp.broadcast_to(rel_ref[h:h + 1, :], (LANES, REL_ROW))
        shifted = pltpu.roll(rows, 0, axis=1, stride=1, stride_axis=0)
        table = jnp.concatenate(
            [jnp.transpose(shifted[:, j * LANES:(j + 1) * LANES]) for j in range(WINDOW_TILES)],
            axis=0)
        bias_scr[h, :WINDOW, :] = jnp.where(in_band, table, MASKED)
        bias_scr[h, WINDOW:, :] = jnp.full((LEFT_ROWS, LANES), MASKED, F32)


def _band_attn_kernel(qt_ref, k_ref, vt_ref, rel_ref, o_ref, bias_scr):
    step = pl.program_id(1)

    @pl.when((pl.program_id(0) == 0) & (step == 0))
    def _():
        _build_bias_table(rel_ref, bias_scr)

    d = ATT_HEAD_DIM
    zeros_half = jnp.zeros((d, Q_TILE), BF16)
    ones_rows = jnp.ones((ONES_ROWS, LANES), BF16)

    for sub in range(ATT_TILES_PER_STEP):
        tile = step * ATT_TILES_PER_STEP + sub
        q_lanes = slice(sub * Q_TILE, (sub + 1) * Q_TILE)
        first_tile = jnp.maximum(tile - LEFT_ROWS // Q_TILE, 0)
        start = pl.multiple_of(first_tile * Q_TILE, Q_TILE)
        shift = pl.multiple_of(LEFT_ROWS - (tile - first_tile) * Q_TILE, Q_TILE)
        outs = []
        for h in range(ATT_HEADS):
            pair_dims = slice((h // 2) * 2 * d, (h // 2 + 1) * 2 * d)
            qh = qt_ref[h * d:(h + 1) * d, q_lanes]
            qm = jnp.concatenate([qh, zeros_half] if h % 2 == 0 else [zeros_half, qh], axis=0)
            tile_max, tile_out = [], []
            for j in range(WINDOW_TILES):
                keys = k_ref[0, pl.ds(start + j * LANES, LANES), pair_dims]
                st = _dot(keys, qm) + bias_scr[h, pl.ds(shift + j * LANES, LANES), :]
                m_j = jnp.max(st, axis=0, keepdims=True)
                p_j = jnp.exp2(st - m_j).astype(BF16)
                v_ones = jnp.concatenate(
                    [vt_ref[0, first_tile + j, h * d:(h + 1) * d, :], ones_rows], axis=0)
                tile_max.append(m_j)
                tile_out.append(_dot(v_ones, p_j))
            m = functools.reduce(jnp.maximum, tile_max)
            acc = functools.reduce(jnp.add, [jnp.exp2(m_j - m) * o
                                             for m_j, o in zip(tile_max, tile_out)])
            outs.append(acc[:d] / acc[d:d + 1])
        o_ref[0, q_lanes, :] = jnp.transpose(jnp.concatenate(outs, axis=0)).astype(BF16)


def _band_attn(layer, qt, k_tok, vt_tiles, rel_rows):
    return pl.pallas_call(
        _band_attn_kernel,
        grid=(BATCH, ATT_STEPS),
        in_specs=[
            pl.BlockSpec((ATT_WIDTH, ATT_TILES_PER_STEP * Q_TILE),
                         lambda b, s: (0, b * ATT_STEPS + s)),
            pl.BlockSpec((1, SEQ, ATT_WIDTH), lambda b, s: (b, 0, 0)),
            pl.BlockSpec((1, SEQ // LANES, ATT_WIDTH, LANES), lambda b, s: (b, 0, 0, 0)),
            _resident((ATT_HEADS, REL_ROW), layer),
        ],
        out_specs=pl.BlockSpec((1, ATT_TILES_PER_STEP * Q_TILE, ATT_WIDTH),
                               lambda b, s: (b, s, 0)),
        out_shape=jax.ShapeDtypeStruct((BATCH, SEQ, ATT_WIDTH), BF16),
        scratch_shapes=[pltpu.VMEM((ATT_HEADS, WINDOW + LEFT_ROWS, LANES), F32)],
        compiler_params=pltpu.CompilerParams(
            dimension_semantics=("arbitrary", "arbitrary"),
            vmem_limit_bytes=VMEM_LIMIT["band_attn"]),
        name="band_attn",
    )(qt, k_tok, vt_tiles, rel_rows)


def _scan_lanes(x, op, identity, segment):
    pos = lax.broadcasted_iota(jnp.int32, x.shape, 1) % segment
    d = 1
    while d < segment:
        x = op(x, jnp.where(pos >= d, pltpu.roll(x, d, axis=1), identity))
        d *= 2
    return x


def _split3(x):
    hi = x.astype(BF16).astype(F32)
    r1 = x - hi
    mid = r1.astype(BF16).astype(F32)
    lo = (r1 - mid).astype(BF16).astype(F32)
    return hi, mid, lo


GP_BCUM, GP_STAB, GP_W, GP_BLAST, GP_MLOC, GP_R_HI, GP_R_MID, GP_R_LO = range(8)
GP_PLANES = 8


def _gate_planes(gt_ref, lanes, gb_ref, planes_ref):
    rows_all = BATCH * GATE_ROWS
    gates = jnp.concatenate([gt_ref[b, :, lanes] for b in range(BATCH)], axis=0) + gb_ref[...]
    logf = jnp.minimum(gates, 0.0) - jnp.log1p(jnp.exp(-jnp.abs(gates)))
    bcum = pltpu.roll(_scan_lanes(logf, jnp.add, 0.0, ML_CHUNK), rows_all - MLSTM_HEADS, axis=0)
    b_last = jnp.broadcast_to(bcum[:, ML_CHUNK - 1:ML_CHUNK], (rows_all, ML_CHUNK))
    a_row = b_last - bcum + gates
    m_loc = jnp.broadcast_to(jnp.max(a_row, axis=1, keepdims=True), (rows_all, ML_CHUNK))
    r_row = gates - bcum
    planes_ref[GP_BCUM] = bcum
    planes_ref[GP_STAB] = bcum + _scan_lanes(r_row, jnp.maximum, -jnp.inf, ML_CHUNK)
    planes_ref[GP_W] = jnp.exp(a_row - m_loc)
    planes_ref[GP_BLAST] = b_last
    planes_ref[GP_MLOC] = m_loc
    for plane, part in zip((GP_R_HI, GP_R_MID, GP_R_LO), _split3(r_row)):
        planes_ref[plane] = part


def _mlstm_kernel(mqk_ref, vt_ref, mot_ref, gt_ref, gt_next_ref, gb_ref, ng_ref, o_ref,
                  c_scr, n_scr, m_scr, gp_scr):
    step = pl.program_id(0)

    @pl.when(step == 0)
    def _():
        c_scr[...] = jnp.zeros_like(c_scr)
        n_scr[...] = jnp.zeros_like(n_scr)
        m_scr[...] = jnp.zeros_like(m_scr)
        for sub in range(ML_CHUNKS_PER_STEP):
            _gate_planes(gt_ref, slice(sub * ML_CHUNK, (sub + 1) * ML_CHUNK), gb_ref,
                         gp_scr.at[sub])

    for sub in range(ML_CHUNKS_PER_STEP):
        _mlstm_chunk(slice(sub * ML_CHUNK, (sub + 1) * ML_CHUNK), mqk_ref, vt_ref, mot_ref,
                     gp_scr.at[sub], ng_ref, o_ref, c_scr, n_scr, m_scr)
    for sub in range(ML_CHUNKS_PER_STEP):
        _gate_planes(gt_next_ref, slice(sub * ML_CHUNK, (sub + 1) * ML_CHUNK), gb_ref,
                     gp_scr.at[sub])


def _mlstm_chunk(time, mqk_ref, vt_ref, mot_ref, gp_ref, ng_ref, o_ref, c_scr, n_scr, m_scr):
    hd = MLSTM_HEAD_DIM

    bcum = gp_ref[GP_BCUM]
    b_last = gp_ref[GP_BLAST]
    m_loc = gp_ref[GP_MLOC]
    w_row = gp_ref[GP_W]
    m_prev = m_scr[...]
    m_new = jnp.maximum(b_last + m_prev, m_loc)
    s_prev = jnp.exp(b_last + m_prev - m_new)
    s_loc = jnp.exp(m_loc - m_new)
    m_scr[...] = m_new
    g_row = bcum + m_prev
    m_t = jnp.maximum(g_row, gp_ref[GP_STAB])
    inter = jnp.exp(g_row - m_t)
    inv_floor = jnp.exp(-m_t)
    n_prev = n_scr[...]
    w_bf = w_row.astype(BF16)
    n_prev_bf = n_prev.astype(BF16)
    r_parts = (gp_ref[GP_R_HI], gp_ref[GP_R_MID], gp_ref[GP_R_LO])
    e_parts = _split3(bcum - m_t)
    split_row = lax.broadcasted_iota(jnp.int32, (SPLIT_ROWS, ML_CHUNK), 0)

    def outer_sum_operand(parts, one, first):
        other = 3 - first
        out = jnp.where((split_row >= other) & (split_row < other + 3), 1.0, 0.0)
        for i, part in enumerate(parts):
            out = jnp.where(split_row == first + i, part[one], out)
        return out.astype(BF16)

    src = lax.broadcasted_iota(jnp.int32, (ML_CHUNK, ML_CHUNK), 0)
    qry = lax.broadcasted_iota(jnp.int32, (ML_CHUNK, ML_CHUNK), 1)
    causal = src <= qry
    new_n = {}
    outs = {}

    def stage_matmuls(u):
        b, h = divmod(u, MLSTM_HEADS)
        one = slice(b * GATE_ROWS + h, b * GATE_ROWS + h + 1)
        grp = slice(b * GATE_ROWS, (b + 1) * GATE_ROWS)
        sl = slice(h * hd, (h + 1) * hd)
        qh = mqk_ref[b, time, sl]
        kh = mqk_ref[b, time, MLSTM_WIDTH + h * hd:MLSTM_WIDTH + (h + 1) * hd]
        vt = vt_ref[b, sl, time]
        c_prev = c_scr[u]
        lhs = outer_sum_operand(r_parts, one, 0)
        rhs = outer_sum_operand(e_parts, one, 3)
        return dict(
            one=one, sl=sl, b=b, h=h, vt=vt, c_prev=c_prev,
            c_loc=_dot((vt.astype(F32) * w_row[one]).astype(BF16), kh),
            n_loc=_dot(w_bf[grp], kh)[h:h + 1],
            exponent=_dot_tn(lhs, rhs),
            scores=_dot_nt(kh, qh),
            nq=_dot_nt(n_prev_bf[grp], qh)[h:h + 1],
            cq=_dot_nt(c_prev.astype(BF16), qh))

    def stage_decay(u, st):
        one = st["one"]
        c_scr[u] = s_prev[one] * st["c_prev"] + s_loc[one] * st["c_loc"]
        new_n[u] = s_prev[one] * n_prev[one] + s_loc[one] * st["n_loc"]
        pt = st["scores"] * jnp.exp(jnp.where(causal, st["exponent"], -jnp.inf))
        st["den"] = inter[one] * st["nq"] + jnp.sum(pt, axis=0, keepdims=True)
        st["pt"] = pt.astype(BF16)

    def stage_intra(u, st):
        st["intra"] = _dot(st["vt"], st["pt"])

    def stage_out(u, st):
        one, sl, b = st["one"], st["sl"], st["b"]
        num = inter[one] * st["cq"] + st["intra"]
        hout = num * (1.0 / jnp.maximum(jnp.abs(st["den"]), inv_floor[one]))
        ms = jnp.mean(hout * hout, axis=0, keepdims=True)
        y = hout * lax.rsqrt(ms + NORM_EPS) * ng_ref[sl, :]
        outs[u] = jax.nn.sigmoid(mot_ref[b, sl, time].astype(F32)) * y
        if st["h"] == MLSTM_HEADS - 1:
            heads = [outs.pop(b * MLSTM_HEADS + i) for i in range(MLSTM_HEADS)]
            o_ref[b, time, :] = jnp.transpose(jnp.concatenate(heads, axis=0)).astype(BF16)

    stages = ((stage_decay, 3), (stage_intra, 4), (stage_out, 7))
    state = {}
    for k in range(ML_UNITS + stages[-1][1]):
        if k < ML_UNITS:
            state[k] = stage_matmuls(k)
        for stage, lag in stages:
            if 0 <= k - lag < ML_UNITS:
                stage(k - lag, state[k - lag])

    pad = jnp.zeros((GATE_ROWS - MLSTM_HEADS, hd), F32)
    n_scr[...] = jnp.concatenate(
        [row for b in range(BATCH)
         for row in [new_n[b * MLSTM_HEADS + i] for i in range(MLSTM_HEADS)] + [pad]], axis=0)


def _mlstm(layer, mqk, mvt, mot, gt, gate_bias, norm_g):
    span = ML_CHUNKS_PER_STEP * ML_CHUNK
    tok = lambda width: pl.BlockSpec((BATCH, span, width), lambda c: (0, c, 0))
    feat = lambda height: pl.BlockSpec((BATCH, height, span), lambda c: (0, 0, c))
    rows_all = BATCH * GATE_ROWS
    return pl.pallas_call(
        _mlstm_kernel,
        grid=(ML_STEPS,),
        in_specs=[
            tok(2 * MLSTM_WIDTH), feat(MLSTM_WIDTH), feat(MLSTM_WIDTH), feat(GATE_ROWS),
            pl.BlockSpec((BATCH, GATE_ROWS, span),
                         lambda c: (0, 0, jnp.minimum(c + 1, ML_STEPS - 1))),
            _resident((rows_all, ML_CHUNK), layer),
            _resident((MLSTM_WIDTH, ML_CHUNK), layer),
        ],
        out_specs=tok(MLSTM_WIDTH),
        out_shape=jax.ShapeDtypeStruct((BATCH, SEQ, MLSTM_WIDTH), BF16),
        scratch_shapes=[
            pltpu.VMEM((ML_UNITS, MLSTM_HEAD_DIM, MLSTM_HEAD_DIM), F32),
            pltpu.VMEM((rows_all, MLSTM_HEAD_DIM), F32),
            pltpu.VMEM((rows_all, ML_CHUNK), F32),
            pltpu.VMEM((ML_CHUNKS_PER_STEP, GP_PLANES, rows_all, ML_CHUNK), F32),
        ],
        compiler_params=pltpu.CompilerParams(
            dimension_semantics=("arbitrary",), vmem_limit_bytes=VMEM_LIMIT["mlstm"]),
        name="mlstm",
    )(mqk, mvt, mot, gt, gt, gate_bias, norm_g)


def _merge_ffn_kernel(x_ref, att_ref, ml_ref, gate_ref, wa_ref, wm_ref, wo_ref,
                      g_ref, wu_ref, wd_ref, fg_ref, o_ref, *, final_norm):
    ga = gate_ref[:, :D_MODEL].astype(F32)
    gm = gate_ref[:, D_MODEL:].astype(F32)
    y = (jax.nn.sigmoid(ga) * _dot(att_ref[...], wa_ref[...])
         + jax.nn.sigmoid(gm) * _dot(ml_ref[...], wm_ref[...]))
    x = x_ref[...] + _dot(y.astype(BF16), wo_ref[...])
    hn = _rms_norm(x, g_ref[...]).astype(BF16)
    acc = x
    for j in range(D_FF // FF_TILE):
        cols = slice(j * FF_TILE, (j + 1) * FF_TILE)
        h = jnp.maximum(_dot(hn, wu_ref[:, cols]), 0.0)
        acc = acc + _dot((h * h).astype(BF16), wd_ref[cols, :])
    if final_norm:
        acc = _rms_norm(acc, fg_ref[...])
    o_ref[...] = acc


def _merge_ffn(layer, x2d, att, ml, gates, wa, wm, wo, g, wu, wd, final_g):
    tm = TOKEN_TILE
    row = lambda width: pl.BlockSpec((tm, width), lambda i: (i, 0))
    return pl.pallas_call(
        functools.partial(_merge_ffn_kernel, final_norm=(layer == DEPTH - 1)),
        grid=(TOKENS // tm,),
        in_specs=[row(D_MODEL), row(ATT_WIDTH), row(MLSTM_WIDTH), row(2 * D_MODEL),
                  _resident((ATT_WIDTH, D_MODEL)), _resident((MLSTM_WIDTH, D_MODEL)),
                  _resident((D_MODEL, D_MODEL)), _resident((1, D_MODEL), layer),
                  _resident((D_MODEL, D_FF)), _resident((D_FF, D_MODEL)),
                  _resident((1, D_MODEL))],
        out_specs=row(D_MODEL),
        out_shape=jax.ShapeDtypeStruct((TOKENS, D_MODEL), F32),
        compiler_params=pltpu.CompilerParams(
            dimension_semantics=("arbitrary",), vmem_limit_bytes=VMEM_LIMIT["merge_ffn"]),
        name="merge_ffn",
    )(x2d, att, ml, gates, wa, wm, wo, g, wu, wd, final_g)


def _rel_bias_rows(rel_bias):
    lead = rel_bias.shape[:-1]
    far = jnp.broadcast_to(rel_bias[..., 2 * MAX_REL_DIST:], lead + (MAX_REL_DIST,))
    near = rel_bias[..., MAX_REL_DIST + 1 - Q_TILE:][..., ::-1]
    rest = jnp.broadcast_to(rel_bias[..., 2 * MAX_REL_DIST:],
                            lead + (REL_ROW - MAX_REL_DIST - near.shape[-1],))
    return jnp.concatenate([far, near, rest], axis=-1).astype(F32) * LOG2E


def kernel(x, mix_norm_g, w_in, conv_w, conv_b, b_igate, b_fgate, rel_bias, mh_norm_g,
           w_att_proj, w_mlstm_proj, w_out, ffn_norm_g, w_up, w_down, final_norm_g):
    w_t = jnp.swapaxes(w_in, 1, 2).astype(BF16)
    w_gates = w_t[:, IN_GATES:]
    w_gate_t = w_t[:, IN_MAIN_COLS:IN_GATES]
    gate_bias = jnp.tile(
        jnp.broadcast_to(jnp.concatenate([b_igate, b_fgate], axis=1)[:, :, None],
                         (DEPTH, GATE_ROWS, ML_CHUNK)), (1, BATCH, 1))
    norm_g = jnp.broadcast_to(mh_norm_g[:, :, None], (DEPTH, MLSTM_WIDTH, ML_CHUNK))
    rel_rows = _rel_bias_rows(rel_bias)
    mix_g = mix_norm_g.reshape(DEPTH, 1, D_MODEL)
    ffn_g = ffn_norm_g.reshape(DEPTH, 1, D_MODEL)
    conv_bias = conv_b.reshape(DEPTH, 1, 2 * MLSTM_WIDTH)
    final_g = final_norm_g.reshape(1, D_MODEL)

    h = x.reshape(TOKENS, D_MODEL)
    for l in range(DEPTH):
        qt, k_tok, vt_tiles, mqk, mvt, mot, gates, gt, wa, wm, wo, wu, wd = _in_proj(
            l, h, mix_g, w_t, w_gates, w_gate_t, conv_w, conv_bias,
            w_att_proj, w_mlstm_proj, w_out, w_up, w_down)
        att = _band_attn(l, qt, k_tok, vt_tiles, rel_rows)
        ml = _mlstm(l, mqk.reshape(BATCH, SEQ, 2 * MLSTM_WIDTH), mvt, mot, gt, gate_bias, norm_g)
        h = _merge_ffn(l, h, att.reshape(TOKENS, ATT_WIDTH), ml.reshape(TOKENS, MLSTM_WIDTH),
                       gates, wa, wm, wo, ffn_g, wu, wd, final_g)
    return h.reshape(BATCH, SEQ, D_MODEL)
```

```python
import functools
import math

import jax
import jax.numpy as jnp
from jax import lax
from jax.experimental import pallas as pl
from jax.experimental.pallas import tpu as pltpu

D_MODEL = 1024
BATCH = 4
SEQ = 4096
DEPTH = 2
CHUNK = 64
NORM_EPS = 1e-6
ATT_HEADS = 8
ATT_HEAD_DIM = 64
ATT_WIDTH = ATT_HEADS * ATT_HEAD_DIM
LEFT_CHUNKS = 8
BAND_CHUNKS = LEFT_CHUNKS + 1
BAND = BAND_CHUNKS * CHUNK
MAX_REL_DIST = 256
MLSTM_HEADS = 4
MLSTM_HEAD_DIM = 128
MLSTM_WIDTH = MLSTM_HEADS * MLSTM_HEAD_DIM
CONV_WIDTH = 4
D_FF = 4 * D_MODEL

LANES = 128
SUBLANES = 8

TOKENS = BATCH * SEQ
LEFT_ROWS = LEFT_CHUNKS * CHUNK
IN_Q = 0
IN_K = IN_Q + ATT_WIDTH
IN_V = IN_K + ATT_WIDTH
IN_MQ = IN_V + ATT_WIDTH
IN_MK = IN_MQ + MLSTM_WIDTH
IN_MV = IN_MK + MLSTM_WIDTH
IN_MO = IN_MV + MLSTM_WIDTH
IN_MAIN_COLS = IN_MO + MLSTM_WIDTH
GATE_ROWS = 2 * MLSTM_HEADS
IN_GATES = IN_MAIN_COLS + GATE_ROWS

Q_TILE = 2 * CHUNK
WINDOW = BAND + CHUNK
WINDOW_TILES = WINDOW // LANES
ATT_TILES_PER_STEP = 8
ATT_STEPS = SEQ // (Q_TILE * ATT_TILES_PER_STEP)
REL_ROW = 1024
ONES_ROWS = ATT_HEAD_DIM
LOG2E = 1.4426950408889634
MASKED = -1e30

ML_CHUNK = LANES
ML_CHUNKS_PER_STEP = 4
ML_STEPS = SEQ // (ML_CHUNK * ML_CHUNKS_PER_STEP)
ML_UNITS = BATCH * MLSTM_HEADS
CONV_TAIL = 16
CONV_ROWS = 16
CONV_COLS = 128
SPLIT_ROWS = 16

PROJ_TILE = 1024
TOKEN_TILE = 512
FF_TILE = 1024

F32 = jnp.float32
BF16 = jnp.bfloat16
MIB = 1024 * 1024
VMEM_LIMIT = {"in_proj": 58 * MIB, "band_attn": 56 * MIB, "mlstm": 56 * MIB, "merge_ffn": 56 * MIB}


def _resident(shape, layer=None):
    nd = len(shape)
    if layer is None:
        return pl.BlockSpec(shape, lambda *_: (0,) * nd, pipeline_mode=pl.Buffered(1))
    return pl.BlockSpec((None,) + tuple(shape), lambda *_: (layer,) + (0,) * nd,
                        pipeline_mode=pl.Buffered(1))


def _rms_norm(x, g):
    ms = jnp.mean(x * x, axis=-1, keepdims=True)
    return x * lax.rsqrt(ms + NORM_EPS) * g


def _dot(a, b):
    return jnp.dot(a, b, preferred_element_type=F32)


def _dot_nt(a, b):
    return lax.dot_general(a, b, (((1,), (1,)), ((), ())), preferred_element_type=F32)


def _dot_tn(a, b):
    return lax.dot_general(a, b, (((0,), (0,)), ((), ())), preferred_element_type=F32)


def _in_proj_kernel(x_ref, g_ref, w_ref, wgate_ref, wg_ref, cw_ref, cb_ref,
                    wa_ref, wm_ref, wo_ref, wu_ref, wd_ref,
                    qt_ref, k_ref, vt_ref, mqk_ref, mvt_ref, mot_ref, gate_ref, gt_ref,
                    wa_out, wm_out, wo_out, wu_out, wd_out,
                    conv_scr):
    a = ATT_WIDTH
    m = MLSTM_WIDTH

    @pl.when(pl.program_id(0) % (SEQ // PROJ_TILE) == 0)
    def _():
        conv_scr[:CONV_TAIL, :] = jnp.zeros((CONV_TAIL, 2 * m), F32)

    for src_ref, dst_ref in ((wa_ref, wa_out), (wm_ref, wm_out), (wo_ref, wo_out),
                             (wu_ref, wu_out), (wd_ref, wd_out)):
        dst_ref[...] = src_ref[...].astype(BF16)

    xn = _rms_norm(x_ref[...], g_ref[...]).astype(BF16)

    def seg(lo, width):
        return _dot_nt(xn, w_ref[lo:lo + width, :])

    def seg_t(lo, width):
        return _dot_nt(w_ref[lo:lo + width, :], xn)

    conv_scr[CONV_TAIL:, :m] = seg(IN_MQ, m)
    conv_scr[CONV_TAIL:, m:] = seg(IN_MK, m)
    for c0 in range(0, 2 * m, CONV_COLS):
        cols = slice(c0, c0 + CONV_COLS)
        post_scale = 1.0 / math.sqrt(MLSTM_HEAD_DIM) if c0 >= m else None
        for r0 in range(0, PROJ_TILE, CONV_ROWS):
            lo = CONV_TAIL - SUBLANES + r0
            ext = conv_scr[lo:lo + SUBLANES + CONV_ROWS, cols]
            acc = cb_ref[:, cols] + ext[SUBLANES:] * cw_ref[CONV_WIDTH - 1:CONV_WIDTH, cols]
            for d in range(1, CONV_WIDTH):
                tap = CONV_WIDTH - 1 - d
                acc = acc + pltpu.roll(ext, d, axis=0)[SUBLANES:] * cw_ref[tap:tap + 1, cols]
            half = 0.5 * acc
            act = half + half * jnp.tanh(half)
            if post_scale is not None:
                act = act * post_scale
            mqk_ref[r0:r0 + CONV_ROWS, cols] = act.astype(BF16)
    conv_scr[:CONV_TAIL, :] = conv_scr[PROJ_TILE:, :]

    qt_ref[...] = (seg_t(IN_Q, a) * (LOG2E / math.sqrt(ATT_HEAD_DIM))).astype(BF16)
    vt = seg_t(IN_V, a).astype(BF16)
    for j in range(PROJ_TILE // LANES):
        vt_ref[0, j] = vt[:, j * LANES:(j + 1) * LANES]
    mv_mo = seg_t(IN_MV, 2 * m).astype(BF16)
    mvt_ref[0] = mv_mo[:m]
    mot_ref[0] = mv_mo[m:]
    gt_ref[0] = _dot_nt(wg_ref[...], xn)
    k_ref[0] = seg(IN_K, a).astype(BF16)
    for j in range(2 * D_MODEL // 512):
        cols = slice(j * 512, (j + 1) * 512)
        gate_ref[:, cols] = _dot_nt(xn, wgate_ref[cols, :]).astype(BF16)


def _in_proj(layer, x2d, g, w_main, w_gates, w_gate_t, conv_w, conv_b, wa, wm, wo, wu, wd):
    steps = TOKENS // PROJ_TILE

    def slab(rows, width):
        return (pl.BlockSpec((None, rows // steps, width), lambda i: (layer, i, 0)),
                pl.BlockSpec((rows // steps, width), lambda i: (i, 0)),
                jax.ShapeDtypeStruct((rows, width), BF16))

    slabs = [slab(ATT_WIDTH, D_MODEL), slab(MLSTM_WIDTH, D_MODEL), slab(D_MODEL, D_MODEL),
             slab(D_MODEL, D_FF), slab(D_FF, D_MODEL)]
    tm = PROJ_TILE
    tiles_per_batch = SEQ // tm
    lane_tiles = tm // LANES
    row = lambda width: pl.BlockSpec((tm, width), lambda i: (i, 0))
    col = lambda height: pl.BlockSpec((1, height, tm),
                                      lambda i: (i // tiles_per_batch, 0, i % tiles_per_batch))
    k_spec = pl.BlockSpec((1, tm, ATT_WIDTH),
                          lambda i: (i // tiles_per_batch, i % tiles_per_batch, 0))
    vt_spec = pl.BlockSpec((1, lane_tiles, ATT_WIDTH, LANES),
                           lambda i: (i // tiles_per_batch, i % tiles_per_batch, 0, 0))
    out_shape = (
        jax.ShapeDtypeStruct((ATT_WIDTH, TOKENS), BF16),
        jax.ShapeDtypeStruct((BATCH, SEQ, ATT_WIDTH), BF16),
        jax.ShapeDtypeStruct((BATCH, SEQ // LANES, ATT_WIDTH, LANES), BF16),
        jax.ShapeDtypeStruct((TOKENS, 2 * MLSTM_WIDTH), BF16),
        jax.ShapeDtypeStruct((BATCH, MLSTM_WIDTH, SEQ), BF16),
        jax.ShapeDtypeStruct((BATCH, MLSTM_WIDTH, SEQ), BF16),
        jax.ShapeDtypeStruct((TOKENS, 2 * D_MODEL), BF16),
        jax.ShapeDtypeStruct((BATCH, GATE_ROWS, SEQ), F32),
    ) + tuple(s[2] for s in slabs)
    return pl.pallas_call(
        _in_proj_kernel,
        grid=(steps,),
        in_specs=[
            row(D_MODEL),
            _resident((1, D_MODEL), layer),
            _resident((IN_MAIN_COLS, D_MODEL), layer),
            _resident((2 * D_MODEL, D_MODEL), layer),
            _resident((GATE_ROWS, D_MODEL), layer),
            _resident((CONV_WIDTH, 2 * MLSTM_WIDTH), layer),
            _resident((1, 2 * MLSTM_WIDTH), layer),
        ] + [s[0] for s in slabs],
        out_specs=(pl.BlockSpec((ATT_WIDTH, tm), lambda i: (0, i)), k_spec, vt_spec,
                   row(2 * MLSTM_WIDTH), col(MLSTM_WIDTH), col(MLSTM_WIDTH), row(2 * D_MODEL),
                   col(GATE_ROWS)) + tuple(s[1] for s in slabs),
        out_shape=out_shape,
        scratch_shapes=[pltpu.VMEM((CONV_TAIL + tm, 2 * MLSTM_WIDTH), F32)],
        compiler_params=pltpu.CompilerParams(
            dimension_semantics=("arbitrary",), vmem_limit_bytes=VMEM_LIMIT["in_proj"]),
        name="in_proj",
    )(x2d, g, w_main, w_gates, w_gate_t, conv_w, conv_b, wa, wm, wo, wu, wd)


def _build_bias_table(rel_ref, bias_scr):
    kb = lax.broadcasted_iota(jnp.int32, (WINDOW, LANES), 0)
    lane = lax.broadcasted_iota(jnp.int32, (WINDOW, LANES), 1)
    band_lo = jnp.where(lane >= CHUNK, CHUNK, 0)
    in_band = (kb >= band_lo) & (kb < band_lo + BAND)
    for h in range(ATT_HEADS):
        rows = jnp.broadcast_to(rel_ref[h:h + 1, :], (LANES, REL_ROW))
        shifted = pltpu.roll(rows, 0, axis=1, stride=1, stride_axis=0)
        table = jnp.concatenate(
            [jnp.transpose(shifted[:, j * LANES:(j + 1) * LANES]) for j in range(WINDOW_TILES)],
            axis=0)
        bias_scr[h, :WINDOW, :] = jnp.where(in_band, table, MASKED)
        bias_scr[h, WINDOW:, :] = jnp.full((LEFT_ROWS, LANES), MASKED, F32)


def _band_attn_kernel(qt_ref, k_ref, vt_ref, rel_ref, o_ref, bias_scr):
    step = pl.program_id(1)

    @pl.when((pl.program_id(0) == 0) & (step == 0))
    def _():
        _build_bias_table(rel_ref, bias_scr)

    d = ATT_HEAD_DIM
    zeros_half = jnp.zeros((d, Q_TILE), BF16)
    ones_rows = jnp.ones((ONES_ROWS, LANES), BF16)

    for sub in range(ATT_TILES_PER_STEP):
        tile = step * ATT_TILES_PER_STEP + sub
        q_lanes = slice(sub * Q_TILE, (sub + 1) * Q_TILE)
        first_tile = jnp.maximum(tile - LEFT_ROWS // Q_TILE, 0)
        start = pl.multiple_of(first_tile * Q_TILE, Q_TILE)
        shift = pl.multiple_of(LEFT_ROWS - (tile - first_tile) * Q_TILE, Q_TILE)
        outs = []
        for h in range(ATT_HEADS):
            pair_dims = slice((h // 2) * 2 * d, (h // 2 + 1) * 2 * d)
            qh = qt_ref[h * d:(h + 1) * d, q_lanes]
            qm = jnp.concatenate([qh, zeros_half] if h % 2 == 0 else [zeros_half, qh], axis=0)
            tile_max, tile_out = [], []
            for j in range(WINDOW_TILES):
                keys = k_ref[0, pl.ds(start + j * LANES, LANES), pair_dims]
                st = _dot(keys, qm) + bias_scr[h, pl.ds(shift + j * LANES, LANES), :]
                m_j = jnp.max(st, axis=0, keepdims=True)
                p_j = jnp.exp2(st - m_j).astype(BF16)
                v_ones = jnp.concatenate(
                    [vt_ref[0, first_tile + j, h * d:(h + 1) * d, :], ones_rows], axis=0)
                tile_max.append(m_j)
                tile_out.append(_dot(v_ones, p_j))
            m = functools.reduce(jnp.maximum, tile_max)
            acc = functools.reduce(jnp.add, [jnp.exp2(m_j - m) * o
                                             for m_j, o in zip(tile_max, tile_out)])
            outs.append(acc[:d] / acc[d:d + 1])
        o_ref[0, q_lanes, :] = jnp.transpose(jnp.concatenate(outs, axis=0)).astype(BF16)


def _band_attn(layer, qt, k_tok, vt_tiles, rel_rows):
    return pl.pallas_call(
        _band_attn_kernel,
        grid=(BATCH, ATT_STEPS),
        in_specs=[
            pl.BlockSpec((ATT_WIDTH, ATT_TILES_PER_STEP * Q_TILE),
                         lambda b, s: (0, b * ATT_STEPS + s)),
            pl.BlockSpec((1, SEQ, ATT_WIDTH), lambda b, s: (b, 0, 0)),
            pl.BlockSpec((1, SEQ // LANES, ATT_WIDTH, LANES), lambda b, s: (b, 0, 0, 0)),
            _resident((ATT_HEADS, REL_ROW), layer),
        ],
        out_specs=pl.BlockSpec((1, ATT_TILES_PER_STEP * Q_TILE, ATT_WIDTH),
                               lambda b, s: (b, s, 0)),
        out_shape=jax.ShapeDtypeStruct((BATCH, SEQ, ATT_WIDTH), BF16),
        scratch_shapes=[pltpu.VMEM((ATT_HEADS, WINDOW + LEFT_ROWS, LANES), F32)],
        compiler_params=pltpu.CompilerParams(
            dimension_semantics=("arbitrary", "arbitrary"),
            vmem_limit_bytes=VMEM_LIMIT["band_attn"]),
        name="band_attn",
    )(qt, k_tok, vt_tiles, rel_rows)


def _scan_lanes(x, op, identity, segment):
    pos = lax.broadcasted_iota(jnp.int32, x.shape, 1) % segment
    d = 1
    while d < segment:
        x = op(x, jnp.where(pos >= d, pltpu.roll(x, d, axis=1), identity))
        d *= 2
    return x


def _split3(x):
    hi = x.astype(BF16).astype(F32)
    r1 = x - hi
    mid = r1.astype(BF16).astype(F32)
    lo = (r1 - mid).astype(BF16).astype(F32)
    return hi, mid, lo


GP_BCUM, GP_STAB, GP_W, GP_BLAST, GP_MLOC, GP_R_HI, GP_R_MID, GP_R_LO = range(8)
GP_PLANES = 8


def _gate_planes(gt_ref, lanes, gb_ref, planes_ref):
    rows_all = BATCH * GATE_ROWS
    gates = jnp.concatenate([gt_ref[b, :, lanes] for b in range(BATCH)], axis=0) + gb_ref[...]
    logf = jnp.minimum(gates, 0.0) - jnp.log1p(jnp.exp(-jnp.abs(gates)))
    bcum = pltpu.roll(_scan_lanes(logf, jnp.add, 0.0, ML_CHUNK), rows_all - MLSTM_HEADS, axis=0)
    b_last = jnp.broadcast_to(bcum[:, ML_CHUNK - 1:ML_CHUNK], (rows_all, ML_CHUNK))
    a_row = b_last - bcum + gates
    m_loc = jnp.broadcast_to(jnp.max(a_row, axis=1, keepdims=True), (rows_all, ML_CHUNK))
    r_row = gates - bcum
    planes_ref[GP_BCUM] = bcum
    planes_ref[GP_STAB] = bcum + _scan_lanes(r_row, jnp.maximum, -jnp.inf, ML_CHUNK)
    planes_ref[GP_W] = jnp.exp(a_row - m_loc)
    planes_ref[GP_BLAST] = b_last
    planes_ref[GP_MLOC] = m_loc
    for plane, part in zip((GP_R_HI, GP_R_MID, GP_R_LO), _split3(r_row)):
        planes_ref[plane] = part


def _mlstm_kernel(mqk_ref, vt_ref, mot_ref, gt_ref, gt_next_ref, gb_ref, ng_ref, o_ref,
                  c_scr, n_scr, m_scr, gp_scr):
    step = pl.program_id(0)

    @pl.when(step == 0)
    def _():
        c_scr[...] = jnp.zeros_like(c_scr)
        n_scr[...] = jnp.zeros_like(n_scr)
        m_scr[...] = jnp.zeros_like(m_scr)
        for sub in range(ML_CHUNKS_PER_STEP):
            _gate_planes(gt_ref, slice(sub * ML_CHUNK, (sub + 1) * ML_CHUNK), gb_ref,
                         gp_scr.at[sub])

    for sub in range(ML_CHUNKS_PER_STEP):
        _mlstm_chunk(slice(sub * ML_CHUNK, (sub + 1) * ML_CHUNK), mqk_ref, vt_ref, mot_ref,
                     gp_scr.at[sub], ng_ref, o_ref, c_scr, n_scr, m_scr)
    for sub in range(ML_CHUNKS_PER_STEP):
        _gate_planes(gt_next_ref, slice(sub * ML_CHUNK, (sub + 1) * ML_CHUNK), gb_ref,
                     gp_scr.at[sub])


def _mlstm_chunk(time, mqk_ref, vt_ref, mot_ref, gp_ref, ng_ref, o_ref, c_scr, n_scr, m_scr):
    hd = MLSTM_HEAD_DIM

    bcum = gp_ref[GP_BCUM]
    b_last = gp_ref[GP_BLAST]
    m_loc = gp_ref[GP_MLOC]
    w_row = gp_ref[GP_W]
    m_prev = m_scr[...]
    m_new = jnp.maximum(b_last + m_prev, m_loc)
    s_prev = jnp.exp(b_last + m_prev - m_new)
    s_loc = jnp.exp(m_loc - m_new)
    m_scr[...] = m_new
    g_row = bcum + m_prev
    m_t = jnp.maximum(g_row, gp_ref[GP_STAB])
    inter = jnp.exp(g_row - m_t)
    inv_floor = jnp.exp(-m_t)
    n_prev = n_scr[...]
    w_bf = w_row.astype(BF16)
    n_prev_bf = n_prev.astype(BF16)
    r_parts = (gp_ref[GP_R_HI], gp_ref[GP_R_MID], gp_ref[GP_R_LO])
    e_parts = _split3(bcum - m_t)
    split_row = lax.broadcasted_iota(jnp.int32, (SPLIT_ROWS, ML_CHUNK), 0)

    def outer_sum_operand(parts, one, first):
        other = 3 - first
        out = jnp.where((split_row >= other) & (split_row < other + 3), 1.0, 0.0)
        for i, part in enumerate(parts):
            out = jnp.where(split_row == first + i, part[one], out)
        return out.astype(BF16)

    src = lax.broadcasted_iota(jnp.int32, (ML_CHUNK, ML_CHUNK), 0)
    qry = lax.broadcasted_iota(jnp.int32, (ML_CHUNK, ML_CHUNK), 1)
    causal = src <= qry
    new_n = {}
    outs = {}

    def stage_matmuls(u):
        b, h = divmod(u, MLSTM_HEADS)
        one = slice(b * GATE_ROWS + h, b * GATE_ROWS + h + 1)
        grp = slice(b * GATE_ROWS, (b + 1) * GATE_ROWS)
        sl = slice(h * hd, (h + 1) * hd)
        qh = mqk_ref[b, time, sl]
        kh = mqk_ref[b, time, MLSTM_WIDTH + h * hd:MLSTM_WIDTH + (h + 1) * hd]
        vt = vt_ref[b, sl, time]
        c_prev = c_scr[u]
        lhs = outer_sum_operand(r_parts, one, 0)
        rhs = outer_sum_operand(e_parts, one, 3)
        return dict(
            one=one, sl=sl, b=b, h=h, vt=vt, c_prev=c_prev,
            c_loc=_dot((vt.astype(F32) * w_row[one]).astype(BF16), kh),
            n_loc=_dot(w_bf[grp], kh)[h:h + 1],
            exponent=_dot_tn(lhs, rhs),
            scores=_dot_nt(kh, qh),
            nq=_dot_nt(n_prev_bf[grp], qh)[h:h + 1],
            cq=_dot_nt(c_prev.astype(BF16), qh))

    def stage_decay(u, st):
        one = st["one"]
        c_scr[u] = s_prev[one] * st["c_prev"] + s_loc[one] * st["c_loc"]
        new_n[u] = s_prev[one] * n_prev[one] + s_loc[one] * st["n_loc"]
        pt = st["scores"] * jnp.exp(jnp.where(causal, st["exponent"], -jnp.inf))
        st["den"] = inter[one] * st["nq"] + jnp.sum(pt, axis=0, keepdims=True)
        st["pt"] = pt.astype(BF16)

    def stage_intra(u, st):
        st["intra"] = _dot(st["vt"], st["pt"])

    def stage_out(u, st):
        one, sl, b = st["one"], st["sl"], st["b"]
        num = inter[one] * st["cq"] + st["intra"]
        hout = num * (1.0 / jnp.maximum(jnp.abs(st["den"]), inv_floor[one]))
        ms = jnp.mean(hout * hout, axis=0, keepdims=True)
        y = hout * lax.rsqrt(ms + NORM_EPS) * ng_ref[sl, :]
        outs[u] = jax.nn.sigmoid(mot_ref[b, sl, time].astype(F32)) * y
        if st["h"] == MLSTM_HEADS - 1:
            heads = [outs.pop(b * MLSTM_HEADS + i) for i in range(MLSTM_HEADS)]
            o_ref[b, time, :] = jnp.transpose(jnp.concatenate(heads, axis=0)).astype(BF16)

    stages = ((stage_decay, 3), (stage_intra, 4), (stage_out, 7))
    state = {}
    for k in range(ML_UNITS + stages[-1][1]):
        if k < ML_UNITS:
            state[k] = stage_matmuls(k)
        for stage, lag in stages:
            if 0 <= k - lag < ML_UNITS:
                stage(k - lag, state[k - lag])

    pad = jnp.zeros((GATE_ROWS - MLSTM_HEADS, hd), F32)
    n_scr[...] = jnp.concatenate(
        [row for b in range(BATCH)
         for row in [new_n[b * MLSTM_HEADS + i] for i in range(MLSTM_HEADS)] + [pad]], axis=0)


def _mlstm(layer, mqk, mvt, mot, gt, gate_bias, norm_g):
    span = ML_CHUNKS_PER_STEP * ML_CHUNK
    tok = lambda width: pl.BlockSpec((BATCH, span, width), lambda c: (0, c, 0))
    feat = lambda height: pl.BlockSpec((BATCH, height, span), lambda c: (0, 0, c))
    rows_all = BATCH * GATE_ROWS
    return pl.pallas_call(
        _mlstm_kernel,
        grid=(ML_STEPS,),
        in_specs=[
            tok(2 * MLSTM_WIDTH), feat(MLSTM_WIDTH), feat(MLSTM_WIDTH), feat(GATE_ROWS),
            pl.BlockSpec((BATCH, GATE_ROWS, span),
                         lambda c: (0, 0, jnp.minimum(c + 1, ML_STEPS - 1))),
            _resident((rows_all, ML_CHUNK), layer),
            _resident((MLSTM_WIDTH, ML_CHUNK), layer),
        ],
        out_specs=tok(MLSTM_WIDTH),
        out_shape=jax.ShapeDtypeStruct((BATCH, SEQ, MLSTM_WIDTH), BF16),
        scratch_shapes=[
            pltpu.VMEM((ML_UNITS, MLSTM_HEAD_DIM, MLSTM_HEAD_DIM), F32),
            pltpu.VMEM((rows_all, MLSTM_HEAD_DIM), F32),
            pltpu.VMEM((rows_all, ML_CHUNK), F32),
            pltpu.VMEM((ML_CHUNKS_PER_STEP, GP_PLANES, rows_all, ML_CHUNK), F32),
        ],
        compiler_params=pltpu.CompilerParams(
            dimension_semantics=("arbitrary",), vmem_limit_bytes=VMEM_LIMIT["mlstm"]),
        name="mlstm",
    )(mqk, mvt, mot, gt, gt, gate_bias, norm_g)


def _merge_ffn_kernel(x_ref, att_ref, ml_ref, gate_ref, wa_ref, wm_ref, wo_ref,
                      g_ref, wu_ref, wd_ref, fg_ref, o_ref, *, final_norm):
    ga = gate_ref[:, :D_MODEL].astype(F32)
    gm = gate_ref[:, D_MODEL:].astype(F32)
    y = (jax.nn.sigmoid(ga) * _dot(att_ref[...], wa_ref[...])
         + jax.nn.sigmoid(gm) * _dot(ml_ref[...], wm_ref[...]))
    x = x_ref[...] + _dot(y.astype(BF16), wo_ref[...])
    hn = _rms_norm(x, g_ref[...]).astype(BF16)
    acc = x
    for j in range(D_FF // FF_TILE):
        cols = slice(j * FF_TILE, (j + 1) * FF_TILE)
        h = jnp.maximum(_dot(hn, wu_ref[:, cols]), 0.0)
        acc = acc + _dot((h * h).astype(BF16), wd_ref[cols, :])
    if final_norm:
        acc = _rms_norm(acc, fg_ref[...])
    o_ref[...] = acc


def _merge_ffn(layer, x2d, att, ml, gates, wa, wm, wo, g, wu, wd, final_g):
    tm = TOKEN_TILE
    row = lambda width: pl.BlockSpec((tm, width), lambda i: (i, 0))
    return pl.pallas_call(
        functools.partial(_merge_ffn_kernel, final_norm=(layer == DEPTH - 1)),
        grid=(TOKENS // tm,),
        in_specs=[row(D_MODEL), row(ATT_WIDTH), row(MLSTM_WIDTH), row(2 * D_MODEL),
                  _resident((ATT_WIDTH, D_MODEL)), _resident((MLSTM_WIDTH, D_MODEL)),
                  _resident((D_MODEL, D_MODEL)), _resident((1, D_MODEL), layer),
                  _resident((D_MODEL, D_FF)), _resident((D_FF, D_MODEL)),
                  _resident((1, D_MODEL))],
        out_specs=row(D_MODEL),
        out_shape=jax.ShapeDtypeStruct((TOKENS, D_MODEL), F32),
        compiler_params=pltpu.CompilerParams(
            dimension_semantics=("arbitrary",), vmem_limit_bytes=VMEM_LIMIT["merge_ffn"]),
        name="merge_ffn",
    )(x2d, att, ml, gates, wa, wm, wo, g, wu, wd, final_g)


def _rel_bias_rows(rel_bias):
    lead = rel_bias.shape[:-1]
    far = jnp.broadcast_to(rel_bias[..., 2 * MAX_REL_DIST:], lead + (MAX_REL_DIST,))
    near = rel_bias[..., MAX_REL_DIST + 1 - Q_TILE:][..., ::-1]
    rest = jnp.broadcast_to(rel_bias[..., 2 * MAX_REL_DIST:],
                            lead + (REL_ROW - MAX_REL_DIST - near.shape[-1],))
    return jnp.concatenate([far, near, rest], axis=-1).astype(F32) * LOG2E


def kernel(x, mix_norm_g, w_in, conv_w, conv_b, b_igate, b_fgate, rel_bias, mh_norm_g,
           w_att_proj, w_mlstm_proj, w_out, ffn_norm_g, w_up, w_down, final_norm_g):
    def feature_major(lo, hi):
        return jnp.swapaxes(w_in[:, :, lo:hi], 1, 2).astype(BF16)

    w_main = feature_major(0, IN_MAIN_COLS)
    w_gate_t = feature_major(IN_MAIN_COLS, IN_GATES)
    w_gates = feature_major(IN_GATES, w_in.shape[-1])
    gate_bias = jnp.tile(
        jnp.broadcast_to(jnp.concatenate([b_igate, b_fgate], axis=1)[:, :, None],
                         (DEPTH, GATE_ROWS, ML_CHUNK)), (1, BATCH, 1))
    norm_g = jnp.broadcast_to(mh_norm_g[:, :, None], (DEPTH, MLSTM_WIDTH, ML_CHUNK))
    rel_rows = _rel_bias_rows(rel_bias)
    mix_g = mix_norm_g.reshape(DEPTH, 1, D_MODEL)
    ffn_g = ffn_norm_g.reshape(DEPTH, 1, D_MODEL)
    conv_bias = conv_b.reshape(DEPTH, 1, 2 * MLSTM_WIDTH)
    final_g = final_norm_g.reshape(1, D_MODEL)

    h = x.reshape(TOKENS, D_MODEL)
    for l in range(DEPTH):
        qt, k_tok, vt_tiles, mqk, mvt, mot, gates, gt, wa, wm, wo, wu, wd = _in_proj(
            l, h, mix_g, w_main, w_gates, w_gate_t, conv_w, conv_bias,
            w_att_proj, w_mlstm_proj, w_out, w_up, w_down)
        att = _band_attn(l, qt, k_tok, vt_tiles, rel_rows)
        ml = _mlstm(l, mqk.reshape(BATCH, SEQ, 2 * MLSTM_WIDTH), mvt, mot, gt, gate_bias, norm_g)
        h = _merge_ffn(l, h, att.reshape(TOKENS, ATT_WIDTH), ml.reshape(TOKENS, MLSTM_WIDTH),
                       gates, wa, wm, wo, ffn_g, wu, wd, final_g)
    return h.reshape(BATCH, SEQ, D_MODEL)
```

```python
import functools
import math

import jax
import jax.numpy as jnp
from jax import lax
from jax.experimental import pallas as pl
from jax.experimental.pallas import tpu as pltpu

D_MODEL = 1024
BATCH = 4
SEQ = 4096
DEPTH = 2
CHUNK = 64
NORM_EPS = 1e-6
ATT_HEADS = 8
ATT_HEAD_DIM = 64
ATT_WIDTH = ATT_HEADS * ATT_HEAD_DIM
LEFT_CHUNKS = 8
BAND_CHUNKS = LEFT_CHUNKS + 1
BAND = BAND_CHUNKS * CHUNK
MAX_REL_DIST = 256
MLSTM_HEADS = 4
MLSTM_HEAD_DIM = 128
MLSTM_WIDTH = MLSTM_HEADS * MLSTM_HEAD_DIM
CONV_WIDTH = 4
D_FF = 4 * D_MODEL

LANES = 128
SUBLANES = 8

TOKENS = BATCH * SEQ
LEFT_ROWS = LEFT_CHUNKS * CHUNK
IN_Q = 0
IN_K = IN_Q + ATT_WIDTH
IN_V = IN_K + ATT_WIDTH
IN_MQ = IN_V + ATT_WIDTH
IN_MK = IN_MQ + MLSTM_WIDTH
IN_MV = IN_MK + MLSTM_WIDTH
IN_MO = IN_MV + MLSTM_WIDTH
IN_MAIN_COLS = IN_MO + MLSTM_WIDTH
GATE_ROWS = 2 * MLSTM_HEADS
IN_GATES = IN_MAIN_COLS + GATE_ROWS

Q_TILE = 2 * CHUNK
WINDOW = BAND + CHUNK
WINDOW_TILES = WINDOW // LANES
ATT_TILES_PER_STEP = 16
ATT_STEPS = SEQ // (Q_TILE * ATT_TILES_PER_STEP)
REL_ROW = 1024
ONES_ROWS = ATT_HEAD_DIM
LOG2E = 1.4426950408889634
MASKED = -1e30

ML_CHUNK = LANES
ML_CHUNKS_PER_STEP = 8
ML_STEPS = SEQ // (ML_CHUNK * ML_CHUNKS_PER_STEP)
ML_UNITS = BATCH * MLSTM_HEADS
CONV_TAIL = 16
CONV_ROWS = 16
CONV_COLS = 128
SPLIT_ROWS = 16

PROJ_TILE = 1024
TOKEN_TILE = 512
FF_TILE = 1024

F32 = jnp.float32
BF16 = jnp.bfloat16
MIB = 1024 * 1024
VMEM_LIMIT = {"in_proj": 58 * MIB, "band_attn": 56 * MIB, "mlstm": 56 * MIB, "merge_ffn": 56 * MIB}


def _resident(shape, layer=None):
    nd = len(shape)
    if layer is None:
        return pl.BlockSpec(shape, lambda *_: (0,) * nd, pipeline_mode=pl.Buffered(1))
    return pl.BlockSpec((None,) + tuple(shape), lambda *_: (layer,) + (0,) * nd,
                        pipeline_mode=pl.Buffered(1))


def _rms_norm(x, g):
    ms = jnp.mean(x * x, axis=-1, keepdims=True)
    return x * lax.rsqrt(ms + NORM_EPS) * g


def _dot(a, b):
    return jnp.dot(a, b, preferred_element_type=F32)


def _dot_nt(a, b):
    return lax.dot_general(a, b, (((1,), (1,)), ((), ())), preferred_element_type=F32)


def _dot_tn(a, b):
    return lax.dot_general(a, b, (((0,), (0,)), ((), ())), preferred_element_type=F32)


def _in_proj_kernel(x_ref, g_ref, w_ref, wgate_ref, wg_ref, cw_ref, cb_ref,
                    wa_ref, wm_ref, wo_ref, wu_ref, wd_ref,
                    qt_ref, k_ref, vt_ref, mqk_ref, mvt_ref, mot_ref, gate_ref, gt_ref,
                    wa_out, wm_out, wo_out, wu_out, wd_out,
                    conv_scr):
    a = ATT_WIDTH
    m = MLSTM_WIDTH

    @pl.when(pl.program_id(0) % (SEQ // PROJ_TILE) == 0)
    def _():
        conv_scr[:CONV_TAIL, :] = jnp.zeros((CONV_TAIL, 2 * m), F32)

    for src_ref, dst_ref in ((wa_ref, wa_out), (wm_ref, wm_out), (wo_ref, wo_out),
                             (wu_ref, wu_out), (wd_ref, wd_out)):
        dst_ref[...] = src_ref[...].astype(BF16)

    xn = _rms_norm(x_ref[...], g_ref[...]).astype(BF16)

    def seg(lo, width):
        return _dot_nt(xn, w_ref[lo:lo + width, :])

    def seg_t(lo, width):
        return _dot_nt(w_ref[lo:lo + width, :], xn)

    conv_scr[CONV_TAIL:, :m] = seg(IN_MQ, m)
    conv_scr[CONV_TAIL:, m:] = seg(IN_MK, m)
    for c0 in range(0, 2 * m, CONV_COLS):
        cols = slice(c0, c0 + CONV_COLS)
        post_scale = 1.0 / math.sqrt(MLSTM_HEAD_DIM) if c0 >= m else None
        for r0 in range(0, PROJ_TILE, CONV_ROWS):
            lo = CONV_TAIL - SUBLANES + r0
            ext = conv_scr[lo:lo + SUBLANES + CONV_ROWS, cols]
            acc = cb_ref[:, cols] + ext[SUBLANES:] * cw_ref[CONV_WIDTH - 1:CONV_WIDTH, cols]
            for d in range(1, CONV_WIDTH):
                tap = CONV_WIDTH - 1 - d
                acc = acc + pltpu.roll(ext, d, axis=0)[SUBLANES:] * cw_ref[tap:tap + 1, cols]
            half = 0.5 * acc
            act = half + half * jnp.tanh(half)
            if post_scale is not None:
                act = act * post_scale
            mqk_ref[r0:r0 + CONV_ROWS, cols] = act.astype(BF16)
    conv_scr[:CONV_TAIL, :] = conv_scr[PROJ_TILE:, :]

    qt_ref[...] = (seg_t(IN_Q, a) * (LOG2E / math.sqrt(ATT_HEAD_DIM))).astype(BF16)
    vt = seg_t(IN_V, a).astype(BF16)
    for j in range(PROJ_TILE // LANES):
        vt_ref[0, j] = vt[:, j * LANES:(j + 1) * LANES]
    mv_mo = seg_t(IN_MV, 2 * m).astype(BF16)
    mvt_ref[0] = mv_mo[:m]
    mot_ref[0] = mv_mo[m:]
    gt_ref[0] = _dot_nt(wg_ref[...], xn)
    k_ref[0] = seg(IN_K, a).astype(BF16)
    for j in range(2 * D_MODEL // 512):
        cols = slice(j * 512, (j + 1) * 512)
        gate_ref[:, cols] = _dot_nt(xn, wgate_ref[cols, :]).astype(BF16)


def _in_proj(layer, x2d, g, w_main, w_gates, w_gate_t, conv_w, conv_b, wa, wm, wo, wu, wd):
    steps = TOKENS // PROJ_TILE

    def slab(rows, width):
        return (pl.BlockSpec((None, rows // steps, width), lambda i: (layer, i, 0)),
                pl.BlockSpec((rows // steps, width), lambda i: (i, 0)),
                jax.ShapeDtypeStruct((rows, width), BF16))

    slabs = [slab(ATT_WIDTH, D_MODEL), slab(MLSTM_WIDTH, D_MODEL), slab(D_MODEL, D_MODEL),
             slab(D_MODEL, D_FF), slab(D_FF, D_MODEL)]
    tm = PROJ_TILE
    tiles_per_batch = SEQ // tm
    lane_tiles = tm // LANES
    row = lambda width: pl.BlockSpec((tm, width), lambda i: (i, 0))
    col = lambda height: pl.BlockSpec((1, height, tm),
                                      lambda i: (i // tiles_per_batch, 0, i % tiles_per_batch))
    k_spec = pl.BlockSpec((1, tm, ATT_WIDTH),
                          lambda i: (i // tiles_per_batch, i % tiles_per_batch, 0))
    vt_spec = pl.BlockSpec((1, lane_tiles, ATT_WIDTH, LANES),
                           lambda i: (i // tiles_per_batch, i % tiles_per_batch, 0, 0))
    out_shape = (
        jax.ShapeDtypeStruct((ATT_WIDTH, TOKENS), BF16),
        jax.ShapeDtypeStruct((BATCH, SEQ, ATT_WIDTH), BF16),
        jax.ShapeDtypeStruct((BATCH, SEQ // LANES, ATT_WIDTH, LANES), BF16),
        jax.ShapeDtypeStruct((TOKENS, 2 * MLSTM_WIDTH), BF16),
        jax.ShapeDtypeStruct((BATCH, MLSTM_WIDTH, SEQ), BF16),
        jax.ShapeDtypeStruct((BATCH, MLSTM_WIDTH, SEQ), BF16),
        jax.ShapeDtypeStruct((TOKENS, 2 * D_MODEL), BF16),
        jax.ShapeDtypeStruct((BATCH, GATE_ROWS, SEQ), F32),
    ) + tuple(s[2] for s in slabs)
    return pl.pallas_call(
        _in_proj_kernel,
        grid=(steps,),
        in_specs=[
            row(D_MODEL),
            _resident((1, D_MODEL), layer),
            _resident((IN_MAIN_COLS, D_MODEL), layer),
            _resident((2 * D_MODEL, D_MODEL), layer),
            _resident((GATE_ROWS, D_MODEL), layer),
            _resident((CONV_WIDTH, 2 * MLSTM_WIDTH), layer),
            _resident((1, 2 * MLSTM_WIDTH), layer),
        ] + [s[0] for s in slabs],
        out_specs=(pl.BlockSpec((ATT_WIDTH, tm), lambda i: (0, i)), k_spec, vt_spec,
                   row(2 * MLSTM_WIDTH), col(MLSTM_WIDTH), col(MLSTM_WIDTH), row(2 * D_MODEL),
                   col(GATE_ROWS)) + tuple(s[1] for s in slabs),
        out_shape=out_shape,
        scratch_shapes=[pltpu.VMEM((CONV_TAIL + tm, 2 * MLSTM_WIDTH), F32)],
        compiler_params=pltpu.CompilerParams(
            dimension_semantics=("arbitrary",), vmem_limit_bytes=VMEM_LIMIT["in_proj"]),
        name="in_proj",
    )(x2d, g, w_main, w_gates, w_gate_t, conv_w, conv_b, wa, wm, wo, wu, wd)


def _build_bias_table(rel_ref, bias_scr):
    kb = lax.broadcasted_iota(jnp.int32, (WINDOW, LANES), 0)
    lane = lax.broadcasted_iota(jnp.int32, (WINDOW, LANES), 1)
    band_lo = jnp.where(lane >= CHUNK, CHUNK, 0)
    in_band = (kb >= band_lo) & (kb < band_lo + BAND)
    for h in range(ATT_HEADS):
        rows = jnp.broadcast_to(rel_ref[h:h + 1, :], (LANES, REL_ROW))
        shifted = pltpu.roll(rows, 0, axis=1, stride=1, stride_axis=0)
        table = jnp.concatenate(
            [jnp.transpose(shifted[:, j * LANES:(j + 1) * LANES]) for j in range(WINDOW_TILES)],
            axis=0)
        bias_scr[h, :WINDOW, :] = jnp.where(in_band, table, MASKED)
        bias_scr[h, WINDOW:, :] = jnp.full((LEFT_ROWS, LANES), MASKED, F32)


def _band_attn_kernel(qt_ref, k_ref, vt_ref, rel_ref, o_ref, bias_scr):
    step = pl.program_id(1)

    @pl.when((pl.program_id(0) == 0) & (step == 0))
    def _():
        _build_bias_table(rel_ref, bias_scr)

    d = ATT_HEAD_DIM
    zeros_half = jnp.zeros((d, Q_TILE), BF16)
    ones_rows = jnp.ones((ONES_ROWS, LANES), BF16)

    for sub in range(ATT_TILES_PER_STEP):
        tile = step * ATT_TILES_PER_STEP + sub
        q_lanes = slice(sub * Q_TILE, (sub + 1) * Q_TILE)
        first_tile = jnp.maximum(tile - LEFT_ROWS // Q_TILE, 0)
        start = pl.multiple_of(first_tile * Q_TILE, Q_TILE)
        shift = pl.multiple_of(LEFT_ROWS - (tile - first_tile) * Q_TILE, Q_TILE)
        outs = []
        for h in range(ATT_HEADS):
            pair_dims = slice((h // 2) * 2 * d, (h // 2 + 1) * 2 * d)
            qh = qt_ref[h * d:(h + 1) * d, q_lanes]
            qm = jnp.concatenate([qh, zeros_half] if h % 2 == 0 else [zeros_half, qh], axis=0)
            tile_max, tile_out = [], []
            for j in range(WINDOW_TILES):
                keys = k_ref[0, pl.ds(start + j * LANES, LANES), pair_dims]
                st = _dot(keys, qm) + bias_scr[h, pl.ds(shift + j * LANES, LANES), :]
                m_j = jnp.max(st, axis=0, keepdims=True)
                p_j = jnp.exp2(st - m_j).astype(BF16)
                v_ones = jnp.concatenate(
                    [vt_ref[0, first_tile + j, h * d:(h + 1) * d, :], ones_rows], axis=0)
                tile_max.append(m_j)
                tile_out.append(_dot(v_ones, p_j))
            m = functools.reduce(jnp.maximum, tile_max)
            acc = functools.reduce(jnp.add, [jnp.exp2(m_j - m) * o
                                             for m_j, o in zip(tile_max, tile_out)])
            outs.append(acc[:d] / acc[d:d + 1])
        o_ref[0, q_lanes, :] = jnp.transpose(jnp.concatenate(outs, axis=0)).astype(BF16)


def _band_attn(layer, qt, k_tok, vt_tiles, rel_rows):
    return pl.pallas_call(
        _band_attn_kernel,
        grid=(BATCH, ATT_STEPS),
        in_specs=[
            pl.BlockSpec((ATT_WIDTH, ATT_TILES_PER_STEP * Q_TILE),
                         lambda b, s: (0, b * ATT_STEPS + s)),
            pl.BlockSpec((1, SEQ, ATT_WIDTH), lambda b, s: (b, 0, 0)),
            pl.BlockSpec((1, SEQ // LANES, ATT_WIDTH, LANES), lambda b, s: (b, 0, 0, 0)),
            _resident((ATT_HEADS, REL_ROW), layer),
        ],
        out_specs=pl.BlockSpec((1, ATT_TILES_PER_STEP * Q_TILE, ATT_WIDTH),
                               lambda b, s: (b, s, 0)),
        out_shape=jax.ShapeDtypeStruct((BATCH, SEQ, ATT_WIDTH), BF16),
        scratch_shapes=[pltpu.VMEM((ATT_HEADS, WINDOW + LEFT_ROWS, LANES), F32)],
        compiler_params=pltpu.CompilerParams(
            dimension_semantics=("arbitrary", "arbitrary"),
            vmem_limit_bytes=VMEM_LIMIT["band_attn"]),
        name="band_attn",
    )(qt, k_tok, vt_tiles, rel_rows)


def _scan_lanes(x, op, identity, segment):
    pos = lax.broadcasted_iota(jnp.int32, x.shape, 1) % segment
    d = 1
    while d < segment:
        x = op(x, jnp.where(pos >= d, pltpu.roll(x, d, axis=1), identity))
        d *= 2
    return x


def _split3(x):
    hi = x.astype(BF16).astype(F32)
    r1 = x - hi
    mid = r1.astype(BF16).astype(F32)
    lo = (r1 - mid).astype(BF16).astype(F32)
    return hi, mid, lo


GP_BCUM, GP_STAB, GP_W, GP_BLAST, GP_MLOC, GP_R_HI, GP_R_MID, GP_R_LO = range(8)
GP_PLANES = 8


def _gate_planes(gt_ref, lanes, gb_ref, planes_ref):
    rows_all = BATCH * GATE_ROWS
    gates = jnp.concatenate([gt_ref[b, :, lanes] for b in range(BATCH)], axis=0) + gb_ref[...]
    logf = jnp.minimum(gates, 0.0) - jnp.log1p(jnp.exp(-jnp.abs(gates)))
    bcum = pltpu.roll(_scan_lanes(logf, jnp.add, 0.0, ML_CHUNK), rows_all - MLSTM_HEADS, axis=0)
    b_last = jnp.broadcast_to(bcum[:, ML_CHUNK - 1:ML_CHUNK], (rows_all, ML_CHUNK))
    a_row = b_last - bcum + gates
    m_loc = jnp.broadcast_to(jnp.max(a_row, axis=1, keepdims=True), (rows_all, ML_CHUNK))
    r_row = gates - bcum
    planes_ref[GP_BCUM] = bcum
    planes_ref[GP_STAB] = bcum + _scan_lanes(r_row, jnp.maximum, -jnp.inf, ML_CHUNK)
    planes_ref[GP_W] = jnp.exp(a_row - m_loc)
    planes_ref[GP_BLAST] = b_last
    planes_ref[GP_MLOC] = m_loc
    for plane, part in zip((GP_R_HI, GP_R_MID, GP_R_LO), _split3(r_row)):
        planes_ref[plane] = part


def _mlstm_kernel(mqk_ref, vt_ref, mot_ref, gt_ref, gt_next_ref, gb_ref, ng_ref, o_ref,
                  c_scr, n_scr, m_scr, gp_scr):
    step = pl.program_id(0)

    @pl.when(step == 0)
    def _():
        c_scr[...] = jnp.zeros_like(c_scr)
        n_scr[...] = jnp.zeros_like(n_scr)
        m_scr[...] = jnp.zeros_like(m_scr)
        for sub in range(ML_CHUNKS_PER_STEP):
            _gate_planes(gt_ref, slice(sub * ML_CHUNK, (sub + 1) * ML_CHUNK), gb_ref,
                         gp_scr.at[sub])

    for sub in range(ML_CHUNKS_PER_STEP):
        _mlstm_chunk(slice(sub * ML_CHUNK, (sub + 1) * ML_CHUNK), mqk_ref, vt_ref, mot_ref,
                     gp_scr.at[sub], ng_ref, o_ref, c_scr, n_scr, m_scr)
    for sub in range(ML_CHUNKS_PER_STEP):
        _gate_planes(gt_next_ref, slice(sub * ML_CHUNK, (sub + 1) * ML_CHUNK), gb_ref,
                     gp_scr.at[sub])


def _mlstm_chunk(time, mqk_ref, vt_ref, mot_ref, gp_ref, ng_ref, o_ref, c_scr, n_scr, m_scr):
    hd = MLSTM_HEAD_DIM

    bcum = gp_ref[GP_BCUM]
    b_last = gp_ref[GP_BLAST]
    m_loc = gp_ref[GP_MLOC]
    w_row = gp_ref[GP_W]
    m_prev = m_scr[...]
    m_new = jnp.maximum(b_last + m_prev, m_loc)
    s_prev = jnp.exp(b_last + m_prev - m_new)
    s_loc = jnp.exp(m_loc - m_new)
    m_scr[...] = m_new
    g_row = bcum + m_prev
    m_t = jnp.maximum(g_row, gp_ref[GP_STAB])
    inter = jnp.exp(g_row - m_t)
    inv_floor = jnp.exp(-m_t)
    n_prev = n_scr[...]
    w_bf = w_row.astype(BF16)
    n_prev_bf = n_prev.astype(BF16)
    r_parts = (gp_ref[GP_R_HI], gp_ref[GP_R_MID], gp_ref[GP_R_LO])
    e_parts = _split3(bcum - m_t)
    split_row = lax.broadcasted_iota(jnp.int32, (SPLIT_ROWS, ML_CHUNK), 0)

    def outer_sum_operand(parts, one, first):
        other = 3 - first
        out = jnp.where((split_row >= other) & (split_row < other + 3), 1.0, 0.0)
        for i, part in enumerate(parts):
            out = jnp.where(split_row == first + i, part[one], out)
        return out.astype(BF16)

    src = lax.broadcasted_iota(jnp.int32, (ML_CHUNK, ML_CHUNK), 0)
    qry = lax.broadcasted_iota(jnp.int32, (ML_CHUNK, ML_CHUNK), 1)
    causal = src <= qry
    new_n = {}
    outs = {}

    def stage_matmuls(u):
        b, h = divmod(u, MLSTM_HEADS)
        one = slice(b * GATE_ROWS + h, b * GATE_ROWS + h + 1)
        grp = slice(b * GATE_ROWS, (b + 1) * GATE_ROWS)
        sl = slice(h * hd, (h + 1) * hd)
        qh = mqk_ref[b, time, sl]
        kh = mqk_ref[b, time, MLSTM_WIDTH + h * hd:MLSTM_WIDTH + (h + 1) * hd]
        vt = vt_ref[b, sl, time]
        c_prev = c_scr[u]
        lhs = outer_sum_operand(r_parts, one, 0)
        rhs = outer_sum_operand(e_parts, one, 3)
        return dict(
            one=one, sl=sl, b=b, h=h, vt=vt, c_prev=c_prev,
            c_loc=_dot((vt.astype(F32) * w_row[one]).astype(BF16), kh),
            n_loc=_dot(w_bf[grp], kh)[h:h + 1],
            exponent=_dot_tn(lhs, rhs),
            scores=_dot_nt(kh, qh),
            nq=_dot_nt(n_prev_bf[grp], qh)[h:h + 1],
            cq=_dot_nt(c_prev.astype(BF16), qh))

    def stage_decay(u, st):
        one = st["one"]
        c_scr[u] = s_prev[one] * st["c_prev"] + s_loc[one] * st["c_loc"]
        new_n[u] = s_prev[one] * n_prev[one] + s_loc[one] * st["n_loc"]
        pt = st["scores"] * jnp.exp(jnp.where(causal, st["exponent"], -jnp.inf))
        st["den"] = inter[one] * st["nq"] + jnp.sum(pt, axis=0, keepdims=True)
        st["pt"] = pt.astype(BF16)

    def stage_intra(u, st):
        st["intra"] = _dot(st["vt"], st["pt"])

    def stage_out(u, st):
        one, sl, b = st["one"], st["sl"], st["b"]
        num = inter[one] * st["cq"] + st["intra"]
        hout = num * (1.0 / jnp.maximum(jnp.abs(st["den"]), inv_floor[one]))
        ms = jnp.mean(hout * hout, axis=0, keepdims=True)
        y = hout * lax.rsqrt(ms + NORM_EPS) * ng_ref[sl, :]
        outs[u] = jax.nn.sigmoid(mot_ref[b, sl, time].astype(F32)) * y
        if st["h"] == MLSTM_HEADS - 1:
            heads = [outs.pop(b * MLSTM_HEADS + i) for i in range(MLSTM_HEADS)]
            o_ref[b, time, :] = jnp.transpose(jnp.concatenate(heads, axis=0)).astype(BF16)

    stages = ((stage_decay, 3), (stage_intra, 4), (stage_out, 7))
    state = {}
    for k in range(ML_UNITS + stages[-1][1]):
        if k < ML_UNITS:
            state[k] = stage_matmuls(k)
        for stage, lag in stages:
            if 0 <= k - lag < ML_UNITS:
                stage(k - lag, state[k - lag])

    pad = jnp.zeros((GATE_ROWS - MLSTM_HEADS, hd), F32)
    n_scr[...] = jnp.concatenate(
        [row for b in range(BATCH)
         for row in [new_n[b * MLSTM_HEADS + i] for i in range(MLSTM_HEADS)] + [pad]], axis=0)


def _mlstm(layer, mqk, mvt, mot, gt, gate_bias, norm_g):
    span = ML_CHUNKS_PER_STEP * ML_CHUNK
    tok = lambda width: pl.BlockSpec((BATCH, span, width), lambda c: (0, c, 0))
    feat = lambda height: pl.BlockSpec((BATCH, height, span), lambda c: (0, 0, c))
    rows_all = BATCH * GATE_ROWS
    return pl.pallas_call(
        _mlstm_kernel,
        grid=(ML_STEPS,),
        in_specs=[
            tok(2 * MLSTM_WIDTH), feat(MLSTM_WIDTH), feat(MLSTM_WIDTH), feat(GATE_ROWS),
            pl.BlockSpec((BATCH, GATE_ROWS, span),
                         lambda c: (0, 0, jnp.minimum(c + 1, ML_STEPS - 1))),
            _resident((rows_all, ML_CHUNK), layer),
            _resident((MLSTM_WIDTH, ML_CHUNK), layer),
        ],
        out_specs=tok(MLSTM_WIDTH),
        out_shape=jax.ShapeDtypeStruct((BATCH, SEQ, MLSTM_WIDTH), BF16),
        scratch_shapes=[
            pltpu.VMEM((ML_UNITS, MLSTM_HEAD_DIM, MLSTM_HEAD_DIM), F32),
            pltpu.VMEM((rows_all, MLSTM_HEAD_DIM), F32),
            pltpu.VMEM((rows_all, ML_CHUNK), F32),
            pltpu.VMEM((ML_CHUNKS_PER_STEP, GP_PLANES, rows_all, ML_CHUNK), F32),
        ],
        compiler_params=pltpu.CompilerParams(
            dimension_semantics=("arbitrary",), vmem_limit_bytes=VMEM_LIMIT["mlstm"]),
        name="mlstm",
    )(mqk, mvt, mot, gt, gt, gate_bias, norm_g)


def _merge_ffn_kernel(x_ref, att_ref, ml_ref, gate_ref, wa_ref, wm_ref, wo_ref,
                      g_ref, wu_ref, wd_ref, fg_ref, o_ref, *, final_norm):
    ga = gate_ref[:, :D_MODEL].astype(F32)
    gm = gate_ref[:, D_MODEL:].astype(F32)
    y = (jax.nn.sigmoid(ga) * _dot(att_ref[...], wa_ref[...])
         + jax.nn.sigmoid(gm) * _dot(ml_ref[...], wm_ref[...]))
    x = x_ref[...] + _dot(y.astype(BF16), wo_ref[...])
    hn = _rms_norm(x, g_ref[...]).astype(BF16)
    acc = x
    for j in range(D_FF // FF_TILE):
        cols = slice(j * FF_TILE, (j + 1) * FF_TILE)
        h = jnp.maximum(_dot(hn, wu_ref[:, cols]), 0.0)
        acc = acc + _dot((h * h).astype(BF16), wd_ref[cols, :])
    if final_norm:
        acc = _rms_norm(acc, fg_ref[...])
    o_ref[...] = acc


def _merge_ffn(layer, x2d, att, ml, gates, wa, wm, wo, g, wu, wd, final_g):
    tm = TOKEN_TILE
    row = lambda width: pl.BlockSpec((tm, width), lambda i: (i, 0))
    return pl.pallas_call(
        functools.partial(_merge_ffn_kernel, final_norm=(layer == DEPTH - 1)),
        grid=(TOKENS // tm,),
        in_specs=[row(D_MODEL), row(ATT_WIDTH), row(MLSTM_WIDTH), row(2 * D_MODEL),
                  _resident((ATT_WIDTH, D_MODEL)), _resident((MLSTM_WIDTH, D_MODEL)),
                  _resident((D_MODEL, D_MODEL)), _resident((1, D_MODEL), layer),
                  _resident((D_MODEL, D_FF)), _resident((D_FF, D_MODEL)),
                  _resident((1, D_MODEL))],
        out_specs=row(D_MODEL),
        out_shape=jax.ShapeDtypeStruct((TOKENS, D_MODEL), F32),
        compiler_params=pltpu.CompilerParams(
            dimension_semantics=("arbitrary",), vmem_limit_bytes=VMEM_LIMIT["merge_ffn"]),
        name="merge_ffn",
    )(x2d, att, ml, gates, wa, wm, wo, g, wu, wd, final_g)


def _rel_bias_rows(rel_bias):
    lead = rel_bias.shape[:-1]
    far = jnp.broadcast_to(rel_bias[..., 2 * MAX_REL_DIST:], lead + (MAX_REL_DIST,))
    near = rel_bias[..., MAX_REL_DIST + 1 - Q_TILE:][..., ::-1]
    rest = jnp.broadcast_to(rel_bias[..., 2 * MAX_REL_DIST:],
                            lead + (REL_ROW - MAX_REL_DIST - near.shape[-1],))
    return jnp.concatenate([far, near, rest], axis=-1).astype(F32) * LOG2E


def kernel(x, mix_norm_g, w_in, conv_w, conv_b, b_igate, b_fgate, rel_bias, mh_norm_g,
           w_att_proj, w_mlstm_proj, w_out, ffn_norm_g, w_up, w_down, final_norm_g):
    w_t = jnp.swapaxes(w_in, 1, 2).astype(BF16)
    w_gates = w_t[:, IN_GATES:]
    w_gate_t = w_t[:, IN_MAIN_COLS:IN_GATES]
    gate_bias = jnp.tile(
        jnp.broadcast_to(jnp.concatenate([b_igate, b_fgate], axis=1)[:, :, None],
                         (DEPTH, GATE_ROWS, ML_CHUNK)), (1, BATCH, 1))
    norm_g = jnp.broadcast_to(mh_norm_g[:, :, None], (DEPTH, MLSTM_WIDTH, ML_CHUNK))
    rel_rows = _rel_bias_rows(rel_bias)
    mix_g = mix_norm_g.reshape(DEPTH, 1, D_MODEL)
    ffn_g = ffn_norm_g.reshape(DEPTH, 1, D_MODEL)
    conv_bias = conv_b.reshape(DEPTH, 1, 2 * MLSTM_WIDTH)
    final_g = final_norm_g.reshape(1, D_MODEL)

    h = x.reshape(TOKENS, D_MODEL)
    for l in range(DEPTH):
        qt, k_tok, vt_tiles, mqk, mvt, mot, gates, gt, wa, wm, wo, wu, wd = _in_proj(
            l, h, mix_g, w_t, w_gates, w_gate_t, conv_w, conv_bias,
            w_att_proj, w_mlstm_proj, w_out, w_up, w_down)
        att = _band_attn(l, qt, k_tok, vt_tiles, rel_rows)
        ml = _mlstm(l, mqk.reshape(BATCH, SEQ, 2 * MLSTM_WIDTH), mvt, mot, gt, gate_bias, norm_g)
        h = _merge_ffn(l, h, att.reshape(TOKENS, ATT_WIDTH), ml.reshape(TOKENS, MLSTM_WIDTH),
                       gates, wa, wm, wo, ffn_g, wu, wd, final_g)
    return h.reshape(BATCH, SEQ, D_MODEL)
```

```python
import functools
import math

import jax
import jax.numpy as jnp
from jax import lax
from jax.experimental import pallas as pl
from jax.experimental.pallas import tpu as pltpu

D_MODEL = 1024
BATCH = 4
SEQ = 4096
DEPTH = 2
CHUNK = 64
NORM_EPS = 1e-6
ATT_HEADS = 8
ATT_HEAD_DIM = 64
ATT_WIDTH = ATT_HEADS * ATT_HEAD_DIM
LEFT_CHUNKS = 8
BAND_CHUNKS = LEFT_CHUNKS + 1
BAND = BAND_CHUNKS * CHUNK
MAX_REL_DIST = 256
MLSTM_HEADS = 4
MLSTM_HEAD_DIM = 128
MLSTM_WIDTH = MLSTM_HEADS * MLSTM_HEAD_DIM
CONV_WIDTH = 4
D_FF = 4 * D_MODEL

LANES = 128
SUBLANES = 8

TOKENS = BATCH * SEQ
LEFT_ROWS = LEFT_CHUNKS * CHUNK
IN_Q = 0
IN_K = IN_Q + ATT_WIDTH
IN_V = IN_K + ATT_WIDTH
IN_MQ = IN_V + ATT_WIDTH
IN_MK = IN_MQ + MLSTM_WIDTH
IN_MV = IN_MK + MLSTM_WIDTH
IN_MO = IN_MV + MLSTM_WIDTH
IN_MAIN_COLS = IN_MO + MLSTM_WIDTH
GATE_ROWS = 2 * MLSTM_HEADS
IN_GATES = IN_MAIN_COLS + GATE_ROWS

Q_TILE = 2 * CHUNK
WINDOW = BAND + CHUNK
WINDOW_TILES = WINDOW // LANES
ATT_TILES_PER_STEP = 16
ATT_STEPS = SEQ // (Q_TILE * ATT_TILES_PER_STEP)
REL_ROW = 1024
ONES_ROWS = ATT_HEAD_DIM
LOG2E = 1.4426950408889634
MASKED = -1e30

ML_CHUNK = LANES
ML_CHUNKS_PER_STEP = 4
ML_STEPS = SEQ // (ML_CHUNK * ML_CHUNKS_PER_STEP)
ML_UNITS = BATCH * MLSTM_HEADS
CONV_TAIL = 16
CONV_ROWS = 16
CONV_COLS = 128
SPLIT_ROWS = 16

PROJ_TILE = 1024
TOKEN_TILE = 512
FF_TILE = 1024

F32 = jnp.float32
BF16 = jnp.bfloat16
MIB = 1024 * 1024
VMEM_LIMIT = {"in_proj": 58 * MIB, "band_attn": 56 * MIB, "mlstm": 56 * MIB, "merge_ffn": 56 * MIB}


def _resident(shape, layer=None):
    nd = len(shape)
    if layer is None:
        return pl.BlockSpec(shape, lambda *_: (0,) * nd, pipeline_mode=pl.Buffered(1))
    return pl.BlockSpec((None,) + tuple(shape), lambda *_: (layer,) + (0,) * nd,
                        pipeline_mode=pl.Buffered(1))


def _rms_norm(x, g):
    ms = jnp.mean(x * x, axis=-1, keepdims=True)
    return x * lax.rsqrt(ms + NORM_EPS) * g


def _dot(a, b):
    return jnp.dot(a, b, preferred_element_type=F32)


def _dot_nt(a, b):
    return lax.dot_general(a, b, (((1,), (1,)), ((), ())), preferred_element_type=F32)


def _dot_tn(a, b):
    return lax.dot_general(a, b, (((0,), (0,)), ((), ())), preferred_element_type=F32)


def _in_proj_kernel(x_ref, g_ref, w_ref, wgate_ref, wg_ref, cw_ref, cb_ref,
                    wa_ref, wm_ref, wo_ref, wu_ref, wd_ref,
                    qt_ref, k_ref, vt_ref, mqk_ref, mvt_ref, mot_ref, gate_ref, gt_ref,
                    wa_out, wm_out, wo_out, wu_out, wd_out,
                    conv_scr):
    a = ATT_WIDTH
    m = MLSTM_WIDTH

    @pl.when(pl.program_id(0) % (SEQ // PROJ_TILE) == 0)
    def _():
        conv_scr[:CONV_TAIL, :] = jnp.zeros((CONV_TAIL, 2 * m), F32)

    for src_ref, dst_ref in ((wa_ref, wa_out), (wm_ref, wm_out), (wo_ref, wo_out),
                             (wu_ref, wu_out), (wd_ref, wd_out)):
        dst_ref[...] = src_ref[...].astype(BF16)

    xn = _rms_norm(x_ref[...], g_ref[...]).astype(BF16)

    def seg(lo, width):
        return _dot_nt(xn, w_ref[lo:lo + width, :])

    def seg_t(lo, width):
        return _dot_nt(w_ref[lo:lo + width, :], xn)

    conv_scr[CONV_TAIL:, :m] = seg(IN_MQ, m)
    conv_scr[CONV_TAIL:, m:] = seg(IN_MK, m)
    for c0 in range(0, 2 * m, CONV_COLS):
        cols = slice(c0, c0 + CONV_COLS)
        post_scale = 1.0 / math.sqrt(MLSTM_HEAD_DIM) if c0 >= m else None
        for r0 in range(0, PROJ_TILE, CONV_ROWS):
            lo = CONV_TAIL - SUBLANES + r0
            ext = conv_scr[lo:lo + SUBLANES + CONV_ROWS, cols]
            acc = cb_ref[:, cols] + ext[SUBLANES:] * cw_ref[CONV_WIDTH - 1:CONV_WIDTH, cols]
            for d in range(1, CONV_WIDTH):
                tap = CONV_WIDTH - 1 - d
                acc = acc + pltpu.roll(ext, d, axis=0)[SUBLANES:] * cw_ref[tap:tap + 1, cols]
            half = 0.5 * acc
            act = half + half * jnp.tanh(half)
            if post_scale is not None:
                act = act * post_scale
            mqk_ref[r0:r0 + CONV_ROWS, cols] = act.astype(BF16)
    conv_scr[:CONV_TAIL, :] = conv_scr[PROJ_TILE:, :]

    qt_ref[...] = (seg_t(IN_Q, a) * (LOG2E / math.sqrt(ATT_HEAD_DIM))).astype(BF16)
    vt = seg_t(IN_V, a).astype(BF16)
    for j in range(PROJ_TILE // LANES):
        vt_ref[0, j] = vt[:, j * LANES:(j + 1) * LANES]
    mv_mo = seg_t(IN_MV, 2 * m).astype(BF16)
    mvt_ref[0] = mv_mo[:m]
    mot_ref[0] = mv_mo[m:]
    gt_ref[0] = _dot_nt(wg_ref[...], xn)
    k_ref[0] = seg(IN_K, a).astype(BF16)
    for j in range(2 * D_MODEL // 512):
        cols = slice(j * 512, (j + 1) * 512)
        gate_ref[:, cols] = _dot_nt(xn, wgate_ref[cols, :]).astype(BF16)


def _in_proj(layer, x2d, g, w_main, w_gates, w_gate_t, conv_w, conv_b, wa, wm, wo, wu, wd):
    steps = TOKENS // PROJ_TILE

    def slab(rows, width):
        return (pl.BlockSpec((None, rows // steps, width), lambda i: (layer, i, 0)),
                pl.BlockSpec((rows // steps, width), lambda i: (i, 0)),
                jax.ShapeDtypeStruct((rows, width), BF16))

    slabs = [slab(ATT_WIDTH, D_MODEL), slab(MLSTM_WIDTH, D_MODEL), slab(D_MODEL, D_MODEL),
             slab(D_MODEL, D_FF), slab(D_FF, D_MODEL)]
    tm = PROJ_TILE
    tiles_per_batch = SEQ // tm
    lane_tiles = tm // LANES
    row = lambda width: pl.BlockSpec((tm, width), lambda i: (i, 0))
    col = lambda height: pl.BlockSpec((1, height, tm),
                                      lambda i: (i // tiles_per_batch, 0, i % tiles_per_batch))
    k_spec = pl.BlockSpec((1, tm, ATT_WIDTH),
                          lambda i: (i // tiles_per_batch, i % tiles_per_batch, 0))
    vt_spec = pl.BlockSpec((1, lane_tiles, ATT_WIDTH, LANES),
                           lambda i: (i // tiles_per_batch, i % tiles_per_batch, 0, 0))
    out_shape = (
        jax.ShapeDtypeStruct((ATT_WIDTH, TOKENS), BF16),
        jax.ShapeDtypeStruct((BATCH, SEQ, ATT_WIDTH), BF16),
        jax.ShapeDtypeStruct((BATCH, SEQ // LANES, ATT_WIDTH, LANES), BF16),
        jax.ShapeDtypeStruct((TOKENS, 2 * MLSTM_WIDTH), BF16),
        jax.ShapeDtypeStruct((BATCH, MLSTM_WIDTH, SEQ), BF16),
        jax.ShapeDtypeStruct((BATCH, MLSTM_WIDTH, SEQ), BF16),
        jax.ShapeDtypeStruct((TOKENS, 2 * D_MODEL), BF16),
        jax.ShapeDtypeStruct((BATCH, GATE_ROWS, SEQ), F32),
    ) + tuple(s[2] for s in slabs)
    return pl.pallas_call(
        _in_proj_kernel,
        grid=(steps,),
        in_specs=[
            row(D_MODEL),
            _resident((1, D_MODEL), layer),
            _resident((IN_MAIN_COLS, D_MODEL), layer),
            _resident((2 * D_MODEL, D_MODEL), layer),
            _resident((GATE_ROWS, D_MODEL), layer),
            _resident((CONV_WIDTH, 2 * MLSTM_WIDTH), layer),
            _resident((1, 2 * MLSTM_WIDTH), layer),
        ] + [s[0] for s in slabs],
        out_specs=(pl.BlockSpec((ATT_WIDTH, tm), lambda i: (0, i)), k_spec, vt_spec,
                   row(2 * MLSTM_WIDTH), col(MLSTM_WIDTH), col(MLSTM_WIDTH), row(2 * D_MODEL),
                   col(GATE_ROWS)) + tuple(s[1] for s in slabs),
        out_shape=out_shape,
        scratch_shapes=[pltpu.VMEM((CONV_TAIL + tm, 2 * MLSTM_WIDTH), F32)],
        compiler_params=pltpu.CompilerParams(
            dimension_semantics=("arbitrary",), vmem_limit_bytes=VMEM_LIMIT["in_proj"]),
        name="in_proj",
    )(x2d, g, w_main, w_gates, w_gate_t, conv_w, conv_b, wa, wm, wo, wu, wd)


def _build_bias_table(rel_ref, bias_scr):
    kb = lax.broadcasted_iota(jnp.int32, (WINDOW, LANES), 0)
    lane = lax.broadcasted_iota(jnp.int32, (WINDOW, LANES), 1)
    band_lo = jnp.where(lane >= CHUNK, CHUNK, 0)
    in_band = (kb >= band_lo) & (kb < band_lo + BAND)
    for h in range(ATT_HEADS):
        rows = jnp.broadcast_to(rel_ref[h:h + 1, :], (LANES, REL_ROW))
        shifted = pltpu.roll(rows, 0, axis=1, stride=1, stride_axis=0)
        table = jnp.concatenate(
            [jnp.transpose(shifted[:, j * LANES:(j + 1) * LANES]) for j in range(WINDOW_TILES)],
            axis=0)
        bias_scr[h, :WINDOW, :] = jnp.where(in_band, table, MASKED)
        bias_scr[h, WINDOW:, :] = jnp.full((LEFT_ROWS, LANES), MASKED, F32)


def _band_attn_kernel(qt_ref, k_ref, vt_ref, rel_ref, o_ref, bias_scr):
    step = pl.program_id(1)

    @pl.when((pl.program_id(0) == 0) & (step == 0))
    def _():
        _build_bias_table(rel_ref, bias_scr)

    d = ATT_HEAD_DIM
    zeros_half = jnp.zeros((d, Q_TILE), BF16)
    ones_rows = jnp.ones((ONES_ROWS, LANES), BF16)

    for sub in range(ATT_TILES_PER_STEP):
        tile = step * ATT_TILES_PER_STEP + sub
        q_lanes = slice(sub * Q_TILE, (sub + 1) * Q_TILE)
        first_tile = jnp.maximum(tile - LEFT_ROWS // Q_TILE, 0)
        start = pl.multiple_of(first_tile * Q_TILE, Q_TILE)
        shift = pl.multiple_of(LEFT_ROWS - (tile - first_tile) * Q_TILE, Q_TILE)
        outs = []
        for h in range(ATT_HEADS):
            pair_dims = slice((h // 2) * 2 * d, (h // 2 + 1) * 2 * d)
            qh = qt_ref[h * d:(h + 1) * d, q_lanes]
            qm = jnp.concatenate([qh, zeros_half] if h % 2 == 0 else [zeros_half, qh], axis=0)
            tile_max, tile_out = [], []
            for j in range(WINDOW_TILES):
                keys = k_ref[0, pl.ds(start + j * LANES, LANES), pair_dims]
                st = _dot(keys, qm) + bias_scr[h, pl.ds(shift + j * LANES, LANES), :]
                m_j = jnp.max(st, axis=0, keepdims=True)
                p_j = jnp.exp2(st - m_j).astype(BF16)
                v_ones = jnp.concatenate(
                    [vt_ref[0, first_tile + j, h * d:(h + 1) * d, :], ones_rows], axis=0)
                tile_max.append(m_j)
                tile_out.append(_dot(v_ones, p_j))
            m = functools.reduce(jnp.maximum, tile_max)
            acc = functools.reduce(jnp.add, [jnp.exp2(m_j - m) * o
                                             for m_j, o in zip(tile_max, tile_out)])
            outs.append(acc[:d] / acc[d:d + 1])
        o_ref[0, q_lanes, :] = jnp.transpose(jnp.concatenate(outs, axis=0)).astype(BF16)


def _band_attn(layer, qt, k_tok, vt_tiles, rel_rows):
    return pl.pallas_call(
        _band_attn_kernel,
        grid=(BATCH, ATT_STEPS),
        in_specs=[
            pl.BlockSpec((ATT_WIDTH, ATT_TILES_PER_STEP * Q_TILE),
                         lambda b, s: (0, b * ATT_STEPS + s)),
            pl.BlockSpec((1, SEQ, ATT_WIDTH), lambda b, s: (b, 0, 0)),
            pl.BlockSpec((1, SEQ // LANES, ATT_WIDTH, LANES), lambda b, s: (b, 0, 0, 0)),
            _resident((ATT_HEADS, REL_ROW), layer),
        ],
        out_specs=pl.BlockSpec((1, ATT_TILES_PER_STEP * Q_TILE, ATT_WIDTH),
                               lambda b, s: (b, s, 0)),
        out_shape=jax.ShapeDtypeStruct((BATCH, SEQ, ATT_WIDTH), BF16),
        scratch_shapes=[pltpu.VMEM((ATT_HEADS, WINDOW + LEFT_ROWS, LANES), F32)],
        compiler_params=pltpu.CompilerParams(
            dimension_semantics=("arbitrary", "arbitrary"),
            vmem_limit_bytes=VMEM_LIMIT["band_attn"]),
        name="band_attn",
    )(qt, k_tok, vt_tiles, rel_rows)


def _scan_lanes(x, op, identity, segment):
    pos = lax.broadcasted_iota(jnp.int32, x.shape, 1) % segment
    d = 1
    while d < segment:
        x = op(x, jnp.where(pos >= d, pltpu.roll(x, d, axis=1), identity))
        d *= 2
    return x


def _split3(x):
    hi = x.astype(BF16).astype(F32)
    r1 = x - hi
    mid = r1.astype(BF16).astype(F32)
    lo = (r1 - mid).astype(BF16).astype(F32)
    return hi, mid, lo


GP_BCUM, GP_STAB, GP_W, GP_BLAST, GP_MLOC, GP_R_HI, GP_R_MID, GP_R_LO = range(8)
GP_PLANES = 8


def _gate_planes(gt_ref, lanes, gb_ref, planes_ref):
    rows_all = BATCH * GATE_ROWS
    gates = jnp.concatenate([gt_ref[b, :, lanes] for b in range(BATCH)], axis=0) + gb_ref[...]
    logf = jnp.minimum(gates, 0.0) - jnp.log1p(jnp.exp(-jnp.abs(gates)))
    bcum = pltpu.roll(_scan_lanes(logf, jnp.add, 0.0, ML_CHUNK), rows_all - MLSTM_HEADS, axis=0)
    b_last = jnp.broadcast_to(bcum[:, ML_CHUNK - 1:ML_CHUNK], (rows_all, ML_CHUNK))
    a_row = b_last - bcum + gates
    m_loc = jnp.broadcast_to(jnp.max(a_row, axis=1, keepdims=True), (rows_all, ML_CHUNK))
    r_row = gates - bcum
    planes_ref[GP_BCUM] = bcum
    planes_ref[GP_STAB] = bcum + _scan_lanes(r_row, jnp.maximum, -jnp.inf, ML_CHUNK)
    planes_ref[GP_W] = jnp.exp(a_row - m_loc)
    planes_ref[GP_BLAST] = b_last
    planes_ref[GP_MLOC] = m_loc
    for plane, part in zip((GP_R_HI, GP_R_MID, GP_R_LO), _split3(r_row)):
        planes_ref[plane] = part


def _mlstm_kernel(mqk_ref, vt_ref, mot_ref, gt_ref, gt_next_ref, gb_ref, ng_ref, o_ref,
                  c_scr, n_scr, m_scr, gp_scr):
    step = pl.program_id(0)

    @pl.when(step == 0)
    def _():
        c_scr[...] = jnp.zeros_like(c_scr)
        n_scr[...] = jnp.zeros_like(n_scr)
        m_scr[...] = jnp.zeros_like(m_scr)
        for sub in range(ML_CHUNKS_PER_STEP):
            _gate_planes(gt_ref, slice(sub * ML_CHUNK, (sub + 1) * ML_CHUNK), gb_ref,
                         gp_scr.at[sub])

    for sub in range(ML_CHUNKS_PER_STEP):
        _mlstm_chunk(slice(sub * ML_CHUNK, (sub + 1) * ML_CHUNK), mqk_ref, vt_ref, mot_ref,
                     gp_scr.at[sub], ng_ref, o_ref, c_scr, n_scr, m_scr)
    for sub in range(ML_CHUNKS_PER_STEP):
        _gate_planes(gt_next_ref, slice(sub * ML_CHUNK, (sub + 1) * ML_CHUNK), gb_ref,
                     gp_scr.at[sub])


def _mlstm_chunk(time, mqk_ref, vt_ref, mot_ref, gp_ref, ng_ref, o_ref, c_scr, n_scr, m_scr):
    hd = MLSTM_HEAD_DIM

    bcum = gp_ref[GP_BCUM]
    b_last = gp_ref[GP_BLAST]
    m_loc = gp_ref[GP_MLOC]
    w_row = gp_ref[GP_W]
    m_prev = m_scr[...]
    m_new = jnp.maximum(b_last + m_prev, m_loc)
    s_prev = jnp.exp(b_last + m_prev - m_new)
    s_loc = jnp.exp(m_loc - m_new)
    m_scr[...] = m_new
    g_row = bcum + m_prev
    m_t = jnp.maximum(g_row, gp_ref[GP_STAB])
    inter = jnp.exp(g_row - m_t)
    inv_floor = jnp.exp(-m_t)
    n_prev = n_scr[...]
    w_bf = w_row.astype(BF16)
    n_prev_bf = n_prev.astype(BF16)
    r_parts = (gp_ref[GP_R_HI], gp_ref[GP_R_MID], gp_ref[GP_R_LO])
    e_parts = _split3(bcum - m_t)
    split_row = lax.broadcasted_iota(jnp.int32, (SPLIT_ROWS, ML_CHUNK), 0)

    def outer_sum_operand(parts, one, first):
        other = 3 - first
        out = jnp.where((split_row >= other) & (split_row < other + 3), 1.0, 0.0)
        for i, part in enumerate(parts):
            out = jnp.where(split_row == first + i, part[one], out)
        return out.astype(BF16)

    src = lax.broadcasted_iota(jnp.int32, (ML_CHUNK, ML_CHUNK), 0)
    qry = lax.broadcasted_iota(jnp.int32, (ML_CHUNK, ML_CHUNK), 1)
    causal = src <= qry
    new_n = {}
    outs = {}

    def stage_matmuls(u):
        b, h = divmod(u, MLSTM_HEADS)
        one = slice(b * GATE_ROWS + h, b * GATE_ROWS + h + 1)
        grp = slice(b * GATE_ROWS, (b + 1) * GATE_ROWS)
        sl = slice(h * hd, (h + 1) * hd)
        qh = mqk_ref[b, time, sl]
        kh = mqk_ref[b, time, MLSTM_WIDTH + h * hd:MLSTM_WIDTH + (h + 1) * hd]
        vt = vt_ref[b, sl, time]
        c_prev = c_scr[u]
        lhs = outer_sum_operand(r_parts, one, 0)
        rhs = outer_sum_operand(e_parts, one, 3)
        return dict(
            one=one, sl=sl, b=b, h=h, vt=vt, c_prev=c_prev,
            c_loc=_dot((vt.astype(F32) * w_row[one]).astype(BF16), kh),
            n_loc=_dot(w_bf[grp], kh)[h:h + 1],
            exponent=_dot_tn(lhs, rhs),
            scores=_dot_nt(kh, qh),
            nq=_dot_nt(n_prev_bf[grp], qh)[h:h + 1],
            cq=_dot_nt(c_prev.astype(BF16), qh))

    def stage_decay(u, st):
        one = st["one"]
        c_scr[u] = s_prev[one] * st["c_prev"] + s_loc[one] * st["c_loc"]
        new_n[u] = s_prev[one] * n_prev[one] + s_loc[one] * st["n_loc"]
        pt = st["scores"] * jnp.exp(jnp.where(causal, st["exponent"], -jnp.inf))
        st["den"] = inter[one] * st["nq"] + jnp.sum(pt, axis=0, keepdims=True)
        st["pt"] = pt.astype(BF16)

    def stage_intra(u, st):
        st["intra"] = _dot(st["vt"], st["pt"])

    def stage_out(u, st):
        one, sl, b = st["one"], st["sl"], st["b"]
        num = inter[one] * st["cq"] + st["intra"]
        hout = num * (1.0 / jnp.maximum(jnp.abs(st["den"]), inv_floor[one]))
        ms = jnp.mean(hout * hout, axis=0, keepdims=True)
        y = hout * lax.rsqrt(ms + NORM_EPS) * ng_ref[sl, :]
        outs[u] = jax.nn.sigmoid(mot_ref[b, sl, time].astype(F32)) * y
        if st["h"] == MLSTM_HEADS - 1:
            heads = [outs.pop(b * MLSTM_HEADS + i) for i in range(MLSTM_HEADS)]
            o_ref[b, time, :] = jnp.transpose(jnp.concatenate(heads, axis=0)).astype(BF16)

    stages = ((stage_decay, 3), (stage_intra, 4), (stage_out, 7))
    state = {}
    for k in range(ML_UNITS + stages[-1][1]):
        if k < ML_UNITS:
            state[k] = stage_matmuls(k)
        for stage, lag in stages:
            if 0 <= k - lag < ML_UNITS:
                stage(k - lag, state[k - lag])

    pad = jnp.zeros((GATE_ROWS - MLSTM_HEADS, hd), F32)
    n_scr[...] = jnp.concatenate(
        [row for b in range(BATCH)
         for row in [new_n[b * MLSTM_HEADS + i] for i in range(MLSTM_HEADS)] + [pad]], axis=0)


def _mlstm(layer, mqk, mvt, mot, gt, gate_bias, norm_g):
    span = ML_CHUNKS_PER_STEP * ML_CHUNK
    tok = lambda width: pl.BlockSpec((BATCH, span, width), lambda c: (0, c, 0))
    feat = lambda height: pl.BlockSpec((BATCH, height, span), lambda c: (0, 0, c))
    rows_all = BATCH * GATE_ROWS
    return pl.pallas_call(
        _mlstm_kernel,
        grid=(ML_STEPS,),
        in_specs=[
            tok(2 * MLSTM_WIDTH), feat(MLSTM_WIDTH), feat(MLSTM_WIDTH), feat(GATE_ROWS),
            pl.BlockSpec((BATCH, GATE_ROWS, span),
                         lambda c: (0, 0, jnp.minimum(c + 1, ML_STEPS - 1))),
            _resident((rows_all, ML_CHUNK), layer),
            _resident((MLSTM_WIDTH, ML_CHUNK), layer),
        ],
        out_specs=tok(MLSTM_WIDTH),
        out_shape=jax.ShapeDtypeStruct((BATCH, SEQ, MLSTM_WIDTH), BF16),
        scratch_shapes=[
            pltpu.VMEM((ML_UNITS, MLSTM_HEAD_DIM, MLSTM_HEAD_DIM), F32),
            pltpu.VMEM((rows_all, MLSTM_HEAD_DIM), F32),
            pltpu.VMEM((rows_all, ML_CHUNK), F32),
            pltpu.VMEM((ML_CHUNKS_PER_STEP, GP_PLANES, rows_all, ML_CHUNK), F32),
        ],
        compiler_params=pltpu.CompilerParams(
            dimension_semantics=("arbitrary",), vmem_limit_bytes=VMEM_LIMIT["mlstm"]),
        name="mlstm",
    )(mqk, mvt, mot, gt, gt, gate_bias, norm_g)


def _merge_ffn_kernel(x_ref, att_ref, ml_ref, gate_ref, wa_ref, wm_ref, wo_ref,
                      g_ref, wu_ref, wd_ref, fg_ref, o_ref, *, final_norm):
    ga = gate_ref[:, :D_MODEL].astype(F32)
    gm = gate_ref[:, D_MODEL:].astype(F32)
    y = (jax.nn.sigmoid(ga) * _dot(att_ref[...], wa_ref[...])
         + jax.nn.sigmoid(gm) * _dot(ml_ref[...], wm_ref[...]))
    x = x_ref[...] + _dot(y.astype(BF16), wo_ref[...])
    hn = _rms_norm(x, g_ref[...]).astype(BF16)
    acc = x
    for j in range(D_FF // FF_TILE):
        cols = slice(j * FF_TILE, (j + 1) * FF_TILE)
        h = jnp.maximum(_dot(hn, wu_ref[:, cols]), 0.0)
        acc = acc + _dot((h * h).astype(BF16), wd_ref[cols, :])
    if final_norm:
        acc = _rms_norm(acc, fg_ref[...])
    o_ref[...] = acc


def _merge_ffn(layer, x2d, att, ml, gates, wa, wm, wo, g, wu, wd, final_g):
    tm = TOKEN_TILE
    row = lambda width: pl.BlockSpec((tm, width), lambda i: (i, 0))
    return pl.pallas_call(
        functools.partial(_merge_ffn_kernel, final_norm=(layer == DEPTH - 1)),
        grid=(TOKENS // tm,),
        in_specs=[row(D_MODEL), row(ATT_WIDTH), row(MLSTM_WIDTH), row(2 * D_MODEL),
                  _resident((ATT_WIDTH, D_MODEL)), _resident((MLSTM_WIDTH, D_MODEL)),
                  _resident((D_MODEL, D_MODEL)), _resident((1, D_MODEL), layer),
                  _resident((D_MODEL, D_FF)), _resident((D_FF, D_MODEL)),
                  _resident((1, D_MODEL))],
        out_specs=row(D_MODEL),
        out_shape=jax.ShapeDtypeStruct((TOKENS, D_MODEL), F32),
        compiler_params=pltpu.CompilerParams(
            dimension_semantics=("arbitrary",), vmem_limit_bytes=VMEM_LIMIT["merge_ffn"]),
        name="merge_ffn",
    )(x2d, att, ml, gates, wa, wm, wo, g, wu, wd, final_g)


def _rel_bias_rows(rel_bias):
    lead = rel_bias.shape[:-1]
    far = jnp.broadcast_to(rel_bias[..., 2 * MAX_REL_DIST:], lead + (MAX_REL_DIST,))
    near = rel_bias[..., MAX_REL_DIST + 1 - Q_TILE:][..., ::-1]
    rest = jnp.broadcast_to(rel_bias[..., 2 * MAX_REL_DIST:],
                            lead + (REL_ROW - MAX_REL_DIST - near.shape[-1],))
    return jnp.concatenate([far, near, rest], axis=-1).astype(F32) * LOG2E


def kernel(x, mix_norm_g, w_in, conv_w, conv_b, b_igate, b_fgate, rel_bias, mh_norm_g,
           w_att_proj, w_mlstm_proj, w_out, ffn_norm_g, w_up, w_down, final_norm_g):
    w_t = jnp.swapaxes(w_in, 1, 2).astype(BF16)
    w_gates = w_t[:, IN_GATES:]
    w_gate_t = w_t[:, IN_MAIN_COLS:IN_GATES]
    gate_bias = jnp.tile(
        jnp.broadcast_to(jnp.concatenate([b_igate, b_fgate], axis=1)[:, :, None],
                         (DEPTH, GATE_ROWS, ML_CHUNK)), (1, BATCH, 1))
    norm_g = jnp.broadcast_to(mh_norm_g[:, :, None], (DEPTH, MLSTM_WIDTH, ML_CHUNK))
    rel_rows = _rel_bias_rows(rel_bias)
    mix_g = mix_norm_g.reshape(DEPTH, 1, D_MODEL)
    ffn_g = ffn_norm_g.reshape(DEPTH, 1, D_MODEL)
    conv_bias = conv_b.reshape(DEPTH, 1, 2 * MLSTM_WIDTH)
    final_g = final_norm_g.reshape(1, D_MODEL)

    h = x.reshape(TOKENS, D_MODEL)
    for l in range(DEPTH):
        qt, k_tok, vt_tiles, mqk, mvt, mot, gates, gt, wa, wm, wo, wu, wd = _in_proj(
            l, h, mix_g, w_t, w_gates, w_gate_t, conv_w, conv_bias,
            w_att_proj, w_mlstm_proj, w_out, w_up, w_down)
        att = _band_attn(l, qt, k_tok, vt_tiles, rel_rows)
        ml = _mlstm(l, mqk.reshape(BATCH, SEQ, 2 * MLSTM_WIDTH), mvt, mot, gt, gate_bias, norm_g)
        h = _merge_ffn(l, h, att.reshape(TOKENS, ATT_WIDTH), ml.reshape(TOKENS, MLSTM_WIDTH),
                       gates, wa, wm, wo, ffn_g, wu, wd, final_g)
    return h.reshape(BATCH, SEQ, D_MODEL)
```
